```python
import math
import functools
import jax
import jax.numpy as jnp
from jax import lax
import numpy as np

D_MODEL = 1024
BATCH = 2
SEQ = 8192
DEPTH = 2
DEC_BATCH = 32
DEC_SEQ = 1
PAST_LEN = 8192
PAGE_SIZE = 128

BRANCH_W = D_MODEL // 2
N_BRANCH = 4
D_CONV = BRANCH_W
CONV_A_W = 31
N_HEADS = 8
HEAD_DIM = BRANCH_W // N_HEADS
N_KV_HEADS = 2
N_IDX_HEADS = 8
D_IDX = 64
TOP_K = 256
Q_BLOCK = 128
IDX_SCALE = (D_IDX ** -0.5) * (N_IDX_HEADS ** -0.5)
SSM_D_INNER = BRANCH_W
SSM_HEADS = 8
SSM_HEAD_DIM = SSM_D_INNER // SSM_HEADS
SSM_GROUPS = 2
D_STATE = 64
SSM_CONV_W = 4
SSM_CHUNK = 128
SSM_CONV_DIM = SSM_D_INNER + 2 * SSM_GROUPS * D_STATE
N_MEM = 256
MEM_HEADS = 4
MEM_HEAD_DIM = BRANCH_W // MEM_HEADS
D_FF = 2816
FFN_CONV_W = 3
ROPE_THETA = 500000.0
EPS = 1e-6

IN_SIZES = (
    2 * D_CONV,
    N_HEADS * HEAD_DIM,
    N_KV_HEADS * HEAD_DIM,
    N_KV_HEADS * HEAD_DIM,
    N_IDX_HEADS * D_IDX,
    D_IDX,
    N_IDX_HEADS,
    SSM_D_INNER,
    SSM_CONV_DIM,
    SSM_HEADS,
    MEM_HEADS * MEM_HEAD_DIM,
    N_BRANCH * D_MODEL,
)
IN_COLS = sum(IN_SIZES)

kernel_name = 'hybrid_conformer_dsa_ssd_memory_step'


def split_cols(proj):
    outs = []
    off = 0
    for n in IN_SIZES:
        outs.append(proj[..., off:off + n])
        off += n
    return outs


def rmsnorm(x, g):
    xf = x.astype(jnp.float32)
    y = xf * lax.rsqrt(jnp.mean(xf * xf, axis=-1, keepdims=True) + EPS)
    return (y * g.astype(jnp.float32)).astype(x.dtype)


def layernorm(x, g, b):
    xf = x.astype(jnp.float32)
    xc = xf - jnp.mean(xf, axis=-1, keepdims=True)
    var = jnp.mean(xc * xc, axis=-1, keepdims=True)
    return (xc * lax.rsqrt(var + EPS) * g.astype(jnp.float32) + b.astype(jnp.float32)).astype(x.dtype)


def rope_partial(x, pos):
    rot = x.shape[-1] // 4
    half = rot // 2
    inv_freq = ROPE_THETA ** (-jnp.arange(half, dtype=jnp.float32) * (2.0 / rot))
    ang = pos.astype(jnp.float32)[:, None] * inv_freq[None, :]
    cos = jnp.cos(ang)[:, None, :]
    sin = jnp.sin(ang)[:, None, :]
    xr = x[..., :rot].astype(jnp.float32)
    x1, x2 = xr[..., :half], xr[..., half:]
    out = jnp.concatenate([x1 * cos - x2 * sin, x2 * cos + x1 * sin], axis=-1)
    return jnp.concatenate([out.astype(x.dtype), x[..., rot:]], axis=-1)


def causal_dwconv(x, prefix, w, b):
    xp = jnp.concatenate([prefix.astype(x.dtype), x], axis=1)
    y = lax.conv_general_dilated(xp, w[:, None, :].astype(x.dtype), window_strides=(1,), padding='VALID',
                                 dimension_numbers=('NWC', 'WIO', 'NWC'), feature_group_count=x.shape[-1])
    return y + b.astype(x.dtype), xp[:, -(w.shape[0] - 1):, :]


def gather_rows(t, idx):
    return jax.vmap(lambda tb, ib: tb[ib])(t, idx)


def indexer_scores(qi, ki, wi, qpos):
    s = jnp.einsum('bqhd,bsd->bqhs', qi.astype(jnp.float32), ki.astype(jnp.float32))
    sc = jnp.einsum('bqhs,bqh->bqs', jax.nn.relu(s), wi.astype(jnp.float32)) * IDX_SCALE
    kpos = jnp.arange(ki.shape[1])
    return jnp.where(kpos[None, None, :] <= qpos[None, :, None], sc, -jnp.inf)


def sparse_attend(q, ks, vs, valid):
    bsz, nq = q.shape[:2]
    qg = q.reshape(bsz, nq, N_KV_HEADS, N_HEADS // N_KV_HEADS, HEAD_DIM).astype(jnp.float32)
    s = jnp.einsum('bqvgd,bqkvd->bqvgk', qg, ks.astype(jnp.float32)) * (HEAD_DIM ** -0.5)
    s = jnp.where(valid[:, :, None, None, :], s, -jnp.inf)
    p = jax.nn.softmax(s, axis=-1)
    o = jnp.einsum('bqvgk,bqkvd->bqvgd', p, vs.astype(jnp.float32))
    return o.reshape(bsz, nq, N_HEADS * HEAD_DIM).astype(q.dtype)


def dsa_prompt(q, k, v, qi, ki, wi):
    bsz, seq = q.shape[:2]
    nb = seq // Q_BLOCK
    ksel = min(TOP_K, seq // 4)

    def to_blocks(t):
        return jnp.moveaxis(t.reshape((bsz, nb, Q_BLOCK) + t.shape[2:]), 1, 0)

    def block(inp):
        i, qb, qib, wib = inp
        qpos = i * Q_BLOCK + jnp.arange(Q_BLOCK)
        _, idx = lax.top_k(indexer_scores(qib, ki, wib, qpos), ksel)
        valid = idx <= qpos[None, :, None]
        return sparse_attend(qb, gather_rows(k, idx), gather_rows(v, idx), valid)

    out = lax.map(block, (jnp.arange(nb), to_blocks(q), to_blocks(qi), to_blocks(wi)))
    return jnp.moveaxis(out, 0, 1).reshape(bsz, seq, N_HEADS * HEAD_DIM)


def dsa_sample(q, k, v, qi, ki, wi, ck, cv, cki, page_table):
    dbsz, nsq = q.shape[:2]
    n_pages = PAST_LEN // PAGE_SIZE
    ki_past = cki[page_table].reshape(dbsz, n_pages * PAGE_SIZE, D_IDX)
    ki_all = jnp.concatenate([ki_past.astype(ki.dtype), ki], axis=1)
    qpos = PAST_LEN + jnp.arange(nsq)
    ksel = min(TOP_K, (PAST_LEN + nsq) // 4)
    _, idx = lax.top_k(indexer_scores(qi, ki_all, wi, qpos), ksel)
    ip = jnp.minimum(idx, PAST_LEN - 1)
    phys = jax.vmap(lambda pt, pg: pt[pg])(page_table, ip // PAGE_SIZE)
    off = ip % PAGE_SIZE
    inew = jnp.clip(idx - PAST_LEN, 0, nsq - 1)
    is_past = (idx < PAST_LEN)[..., None, None]
    ks = jnp.where(is_past, ck[phys, off].astype(k.dtype), gather_rows(k, inew))
    vs = jnp.where(is_past, cv[phys, off].astype(v.dtype), gather_rows(v, inew))
    valid = idx <= qpos[None, :, None]
    return sparse_attend(q, ks, vs, valid)


def memory_kv(mem, prm):
    bsz, nm, _ = mem.shape
    m = rmsnorm(mem, prm['mem_norm_g']) @ prm['w_mem_kv']
    mk, mv = jnp.split(m, 2, axis=-1)
    mk = rmsnorm(mk.reshape(bsz, nm, MEM_HEADS, MEM_HEAD_DIM), prm['mk_norm_g'])
    return mk, mv.reshape(bsz, nm, MEM_HEADS, MEM_HEAD_DIM)


def mem_attend(q, mk, mv):
    bsz, seq = q.shape[:2]
    s = jnp.einsum('blhd,bmhd->blhm', q.astype(jnp.float32), mk.astype(jnp.float32)) * (MEM_HEAD_DIM ** -0.5)
    p = jax.nn.softmax(s, axis=-1)
    o = jnp.einsum('blhm,bmhd->blhd', p, mv.astype(jnp.float32))
    return o.reshape(bsz, seq, MEM_HEADS * MEM_HEAD_DIM).astype(q.dtype)


def ssd_scan(x, dt, a, bmat, cmat, h0):
    bsz, seq = x.shape[:2]
    q = min(SSM_CHUNK, seq)
    pad = (-seq) % q
    nc = (seq + pad) // q
    rep = SSM_HEADS // SSM_GROUPS

    def pad_t(t):
        return jnp.pad(t, [(0, 0), (0, pad)] + [(0, 0)] * (t.ndim - 2))

    xdt = pad_t(x.astype(jnp.float32) * dt[..., None])
    adt = pad_t(dt * a)
    bh = pad_t(jnp.repeat(bmat.astype(jnp.float32), rep, axis=2))
    ch = pad_t(jnp.repeat(cmat.astype(jnp.float32), rep, axis=2))
    xc = xdt.reshape(bsz, nc, q, SSM_HEADS, SSM_HEAD_DIM)
    bc = bh.reshape(bsz, nc, q, SSM_HEADS, D_STATE)
    cc = ch.reshape(bsz, nc, q, SSM_HEADS, D_STATE)
    a_cs = jnp.cumsum(adt.reshape(bsz, nc, q, SSM_HEADS).transpose(0, 3, 1, 2), axis=-1)
    seg = a_cs[..., :, None] - a_cs[..., None, :]
    causal = jnp.tril(jnp.ones((q, q), dtype=bool))
    lmat = jnp.exp(jnp.where(causal, seg, -jnp.inf))
    scores = jnp.einsum('bclhn,bcshn->bhcls', cc, bc) * lmat
    y_diag = jnp.einsum('bhcls,bcshp->bclhp', scores, xc)
    decay = jnp.exp(a_cs[..., -1:] - a_cs)
    chunk_states = jnp.einsum('bclhn,bhcl,bclhp->bchpn', bc, decay, xc)
    chunk_decay = jnp.exp(a_cs[..., -1])

    def step(h, inp):
        st, dec = inp
        return h * dec[:, :, None, None] + st, h

    h_final, h_prev = lax.scan(step, h0.astype(jnp.float32),
                               (jnp.moveaxis(chunk_states, 1, 0), jnp.moveaxis(chunk_decay, 2, 0)))
    h_prev = jnp.moveaxis(h_prev, 0, 1)
    y_off = jnp.einsum('bclhn,bchpn,bhcl->bclhp', cc, h_prev, jnp.exp(a_cs))
    y = (y_diag + y_off).reshape(bsz, nc * q, SSM_HEADS, SSM_HEAD_DIM)[:, :seq]
    return y, h_final


def ssd_branch(z, xbc, dt_raw, conv_buf, h0, prm):
    bsz, seq, _ = z.shape
    xbc, conv_new = causal_dwconv(xbc, conv_buf, prm['ssm_conv_w'], prm['ssm_conv_b'])
    xbc = jax.nn.silu(xbc)
    nb = SSM_GROUPS * D_STATE
    xs = xbc[..., :SSM_D_INNER].reshape(bsz, seq, SSM_HEADS, SSM_HEAD_DIM)
    bm = xbc[..., SSM_D_INNER:SSM_D_INNER + nb].reshape(bsz, seq, SSM_GROUPS, D_STATE)
    cm = xbc[..., SSM_D_INNER + nb:].reshape(bsz, seq, SSM_GROUPS, D_STATE)
    dt = jax.nn.softplus(dt_raw.astype(jnp.float32) + prm['dt_bias'].astype(jnp.float32))
    a = -jnp.exp(prm['a_log'].astype(jnp.float32))
    y, h = ssd_scan(xs, dt, a, bm, cm, h0)
    y = y + prm['d_skip'].astype(jnp.float32)[:, None] * xs.astype(jnp.float32)
    y = y.reshape(bsz, seq, SSM_D_INNER).astype(z.dtype)
    return rmsnorm(y * jax.nn.silu(z), prm['ssm_norm_g']), conv_new, h.astype(h0.dtype)


def layer_forward(x, pos, prm, attn_fn, mem_k, mem_v, conf_buf, sconv_buf, ssm_h, ffn_buf):
    bsz, seq, _ = x.shape
    h = rmsnorm(x, prm['norm_mix_g'])
    glu, q, k, v, qi, ki, wi, z, xbc, dt_raw, mq, gates = split_cols(h @ prm['w_in'])
    a_val, a_gate = jnp.split(glu, 2, axis=-1)
    u, conf_new = causal_dwconv(a_val * jax.nn.sigmoid(a_gate), conf_buf, prm['conv_a_w'], prm['conv_a_b'])
    br_a = jax.nn.silu(layernorm(u, prm['ln_a_g'], prm['ln_a_b']))
    q = rope_partial(rmsnorm(q.reshape(bsz, seq, N_HEADS, HEAD_DIM), prm['q_norm_g']), pos)
    k = rope_partial(rmsnorm(k.reshape(bsz, seq, N_KV_HEADS, HEAD_DIM), prm['k_norm_g']), pos)
    v = v.reshape(bsz, seq, N_KV_HEADS, HEAD_DIM)
    qi = rope_partial(qi.reshape(bsz, seq, N_IDX_HEADS, D_IDX), pos)
    ki = rope_partial(ki[:, :, None, :], pos)[:, :, 0, :]
    br_b = attn_fn(q, k, v, qi, ki, wi)
    br_c, sconv_new, ssm_new = ssd_branch(z, xbc, dt_raw, sconv_buf, ssm_h, prm)
    mq = rmsnorm(mq.reshape(bsz, seq, MEM_HEADS, MEM_HEAD_DIM), prm['mq_norm_g'])
    br_m = mem_attend(mq, mem_k, mem_v)
    br = jnp.stack([br_a, br_b, br_c, br_m], axis=2)
    proj_br = jnp.einsum('blnc,ncd->blnd', br, prm['w_branch'])
    gate = jax.nn.sigmoid(gates.reshape(bsz, seq, N_BRANCH, D_MODEL))
    x = x + jnp.sum(gate * proj_br, axis=2) @ prm['w_out']
    u2, ffn_new = causal_dwconv(rmsnorm(x, prm['norm_ffn_g']) @ prm['w_ffn_up'], ffn_buf,
                                prm['ffn_conv_w'], prm['ffn_conv_b'])
    f_gate, f_up = jnp.split(u2, 2, axis=-1)
    x = x + (jax.nn.silu(f_gate) * f_up) @ prm['w_ffn_down']
    return x, (k, v, ki, conf_new, sconv_new, ssm_new, ffn_new)


def setup_inputs(seed: int = 0) -> dict:
    key = jax.random.key(seed)
    ks = iter(jax.random.split(key, 64))

    def nrm(shape, scale=1.0):
        return jax.random.normal(next(ks), shape, jnp.float32) * scale

    def gain(shape):
        return 1.0 + nrm(shape, 0.02)

    n_pages = PAST_LEN // PAGE_SIZE
    n_used = DEC_BATCH * n_pages
    n_phys = n_used + max(1, n_used // 4)
    page_table = jax.random.permutation(next(ks), n_phys)[:n_used].reshape(DEC_BATCH, n_pages).astype(jnp.int32)
    dt0 = jnp.exp(jax.random.uniform(next(ks), (DEPTH, SSM_HEADS), jnp.float32, math.log(1e-3), math.log(1e-1)))
    dt_bias = dt0 + jnp.log(-jnp.expm1(-dt0))
    a_log = jnp.log(jax.random.uniform(next(ks), (DEPTH, SSM_HEADS), jnp.float32, 1.0, 16.0))
    return {
        'x_prompt': nrm((BATCH, SEQ, D_MODEL)),
        'x_sample': nrm((DEC_BATCH, DEC_SEQ, D_MODEL)),
        'cache_k': nrm((DEPTH, n_phys, PAGE_SIZE, N_KV_HEADS, HEAD_DIM)),
        'cache_v': nrm((DEPTH, n_phys, PAGE_SIZE, N_KV_HEADS, HEAD_DIM)),
        'cache_kidx': nrm((DEPTH, n_phys, PAGE_SIZE, D_IDX)),
        'cache_mem_k': nrm((DEPTH, DEC_BATCH, N_MEM, MEM_HEADS, MEM_HEAD_DIM)),
        'cache_mem_v': nrm((DEPTH, DEC_BATCH, N_MEM, MEM_HEADS, MEM_HEAD_DIM)),
        'state_conformer': nrm((DEPTH, DEC_BATCH, CONV_A_W - 1, D_CONV), 0.5),
        'state_ssm_conv': nrm((DEPTH, DEC_BATCH, SSM_CONV_W - 1, SSM_CONV_DIM)),
        'state_ssm': nrm((DEPTH, DEC_BATCH, SSM_HEADS, SSM_HEAD_DIM, D_STATE), 0.3),
        'state_ffn_conv': nrm((DEPTH, DEC_BATCH, FFN_CONV_W - 1, 2 * D_FF)),
        'page_table': page_table,
        'mem_prompt': nrm((BATCH, N_MEM, D_MODEL)),
        'norm_mix_g': gain((DEPTH, D_MODEL)),
        'w_in': nrm((DEPTH, D_MODEL, IN_COLS), D_MODEL ** -0.5),
        'conv_a_w': nrm((DEPTH, CONV_A_W, D_CONV), CONV_A_W ** -0.5),
        'conv_a_b': nrm((DEPTH, D_CONV), 0.01),
        'ln_a_g': gain((DEPTH, D_CONV)),
        'ln_a_b': nrm((DEPTH, D_CONV), 0.01),
        'q_norm_g': gain((DEPTH, HEAD_DIM)),
        'k_norm_g': gain((DEPTH, HEAD_DIM)),
        'ssm_conv_w': nrm((DEPTH, SSM_CONV_W, SSM_CONV_DIM), SSM_CONV_W ** -0.5),
        'ssm_conv_b': nrm((DEPTH, SSM_CONV_DIM), 0.01),
        'dt_bias': dt_bias,
        'a_log': a_log,
        'd_skip': gain((DEPTH, SSM_HEADS)),
        'ssm_norm_g': gain((DEPTH, SSM_D_INNER)),
        'mem_norm_g': gain((DEPTH, D_MODEL)),
        'w_mem_kv': nrm((DEPTH, D_MODEL, 2 * MEM_HEADS * MEM_HEAD_DIM), D_MODEL ** -0.5),
        'mq_norm_g': gain((DEPTH, MEM_HEAD_DIM)),
        'mk_norm_g': gain((DEPTH, MEM_HEAD_DIM)),
        'w_branch': nrm((DEPTH, N_BRANCH, BRANCH_W, D_MODEL), BRANCH_W ** -0.5),
        'w_out': nrm((DEPTH, D_MODEL, D_MODEL), D_MODEL ** -0.5),
        'norm_ffn_g': gain((DEPTH, D_MODEL)),
        'w_ffn_up': nrm((DEPTH, D_MODEL, 2 * D_FF), D_MODEL ** -0.5),
        'ffn_conv_w': nrm((DEPTH, FFN_CONV_W, 2 * D_FF), FFN_CONV_W ** -0.5),
        'ffn_conv_b': nrm((DEPTH, 2 * D_FF), 0.01),
        'w_ffn_down': nrm((DEPTH, D_FF, D_MODEL), D_FF ** -0.5),
    }


def reference(x_prompt, x_sample, cache_k, cache_v, cache_kidx, cache_mem_k, cache_mem_v,
              state_conformer, state_ssm_conv, state_ssm, state_ffn_conv, page_table, mem_prompt,
              norm_mix_g, w_in, conv_a_w, conv_a_b, ln_a_g, ln_a_b, q_norm_g, k_norm_g,
              ssm_conv_w, ssm_conv_b, dt_bias, a_log, d_skip, ssm_norm_g,
              mem_norm_g, w_mem_kv, mq_norm_g, mk_norm_g, w_branch, w_out,
              norm_ffn_g, w_ffn_up, ffn_conv_w, ffn_conv_b, w_ffn_down):
    bsz, seq, _ = x_prompt.shape
    dseq = x_sample.shape[1]
    dt = x_prompt.dtype
    pos_p = jnp.arange(seq, dtype=jnp.int32)
    pos_s = PAST_LEN + jnp.arange(dseq, dtype=jnp.int32)
    xp, xs = x_prompt, x_sample
    p_k, p_v, p_ki, p_mk, p_mv, p_conf, p_sc, p_ssm, p_ffn = [], [], [], [], [], [], [], [], []
    s_k, s_v, s_ki, s_conf, s_sc, s_ssm, s_ffn = [], [], [], [], [], [], []
    for l in range(DEPTH):
        prm = dict(norm_mix_g=norm_mix_g[l], w_in=w_in[l], conv_a_w=conv_a_w[l], conv_a_b=conv_a_b[l],
                   ln_a_g=ln_a_g[l], ln_a_b=ln_a_b[l], q_norm_g=q_norm_g[l], k_norm_g=k_norm_g[l],
                   ssm_conv_w=ssm_conv_w[l], ssm_conv_b=ssm_conv_b[l], dt_bias=dt_bias[l], a_log=a_log[l],
                   d_skip=d_skip[l], ssm_norm_g=ssm_norm_g[l], mem_norm_g=mem_norm_g[l], w_mem_kv=w_mem_kv[l],
                   mq_norm_g=mq_norm_g[l], mk_norm_g=mk_norm_g[l], w_branch=w_branch[l], w_out=w_out[l],
                   norm_ffn_g=norm_ffn_g[l], w_ffn_up=w_ffn_up[l], ffn_conv_w=ffn_conv_w[l],
                   ffn_conv_b=ffn_conv_b[l], w_ffn_down=w_ffn_down[l])
        mk, mv = memory_kv(mem_prompt, prm)
        xp, st = layer_forward(
            xp, pos_p, prm, dsa_prompt, mk, mv,
            jnp.zeros((bsz, CONV_A_W - 1, D_CONV), dt),
            jnp.zeros((bsz, SSM_CONV_W - 1, SSM_CONV_DIM), dt),
            jnp.zeros((bsz, SSM_HEADS, SSM_HEAD_DIM, D_STATE), dt),
            jnp.zeros((bsz, FFN_CONV_W - 1, 2 * D_FF), dt))
        p_k.append(st[0]); p_v.append(st[1]); p_ki.append(st[2]); p_mk.append(mk); p_mv.append(mv)
        p_conf.append(st[3]); p_sc.append(st[4]); p_ssm.append(st[5]); p_ffn.append(st[6])
        attn_s = functools.partial(dsa_sample, ck=cache_k[l], cv=cache_v[l], cki=cache_kidx[l],
                                   page_table=page_table)
        xs, st = layer_forward(xs, pos_s, prm, attn_s, cache_mem_k[l], cache_mem_v[l],
                               state_conformer[l], state_ssm_conv[l], state_ssm[l], state_ffn_conv[l])
        s_k.append(st[0]); s_v.append(st[1]); s_ki.append(st[2])
        s_conf.append(st[3]); s_sc.append(st[4]); s_ssm.append(st[5]); s_ffn.append(st[6])
    return (xp, xs,
            jnp.stack(p_k), jnp.stack(p_v), jnp.stack(p_ki), jnp.stack(p_mk), jnp.stack(p_mv),
            jnp.stack(p_conf), jnp.stack(p_sc), jnp.stack(p_ssm), jnp.stack(p_ffn),
            jnp.stack(s_k), jnp.stack(s_v), jnp.stack(s_ki),
            jnp.stack(s_conf), jnp.stack(s_sc), jnp.stack(s_ssm), jnp.stack(s_ffn))
```

```python
import functools
import math

import numpy as np
import jax
import jax.numpy as jnp
from jax import lax
from jax.experimental import pallas as pl
from jax.experimental.pallas import tpu as pltpu

F32 = jnp.float32
BF16 = jnp.bfloat16
I32 = jnp.int32

EPS = 1e-6
ROPE_THETA = 500000.0
LANES = 128
V7X_VMEM_LIMIT = 56 * 1024 * 1024

N_HEADS = 8
HEAD_DIM = 64
N_KV_HEADS = 2
N_IDX_HEADS = 8
D_IDX = 64
TOP_K = 256
SSM_HEADS = 8
SSM_HEAD_DIM = 64
SSM_GROUPS = 2
D_STATE = 64
SSM_CHUNK = 128
MEM_HEADS = 4
MEM_HEAD_DIM = 128
IDX_SCALE = (D_IDX ** -0.5) * (N_IDX_HEADS ** -0.5)
NEG_BIG = -1e30
INT_MIN = -2 ** 31

SM_KI = 0
SM_WI = 64
SM_DT = 72

OFF = dict(glu=0, q=1024, qi=1536, z=2048, mq=2560, xbc=3072, k=3840, v=3968, gates=4096, small=8192)
PROJ_COLS = 8320


def _cp(sem):
    return pltpu.CompilerParams(dimension_semantics=sem, vmem_limit_bytes=V7X_VMEM_LIMIT)


def _nt_dot(a, b):
    return lax.dot_general(a, b, (((1,), (1,)), ((), ())), preferred_element_type=F32)


def _dot(a, b):
    return jnp.dot(a, b, preferred_element_type=F32)


def _split3(x):
    hi = x.astype(BF16)
    r = x - hi.astype(F32)
    mid = r.astype(BF16)
    lo = (r - mid.astype(F32)).astype(BF16)
    return hi, mid, lo


def _dot_sel(x, e):
    hi, mid, lo = _split3(x)
    return _dot(hi, e) + _dot(mid, e) + _dot(lo, e)


def _sel_dot(e, x):
    hi, mid, lo = _split3(x)
    return _dot(e, hi) + _dot(e, mid) + _dot(e, lo)


def _silu(x):
    return x * jax.nn.sigmoid(x)


def _softplus(x):
    return jnp.maximum(x, 0.0) + jnp.log1p(jnp.exp(-jnp.abs(x)))


def _rms(x):
    return x * lax.rsqrt(jnp.mean(x * x, axis=-1, keepdims=True) + EPS)


def _sortable(x):
    b = lax.bitcast_convert_type(x, I32)
    b = jnp.where(x == 0.0, 0, b)
    return jnp.where(b < 0, b ^ 0x7FFFFFFF, b)


def _norm_proj_kernel(x_ref, g_ref, w_ref, o_ref, h_ref):
    @pl.when(pl.program_id(1) == 0)
    def _():
        h_ref[...] = _rms(x_ref[...]) * g_ref[...]

    o_ref[...] = _dot(h_ref[...], w_ref[...])


def _norm_proj(x, g, w, tm, tn):
    t, d = x.shape
    n = w.shape[1]
    return pl.pallas_call(
        _norm_proj_kernel,
        grid=(t // tm, n // tn),
        in_specs=[pl.BlockSpec((tm, d), lambda i, j: (i, 0)),
                  pl.BlockSpec((1, d), lambda i, j: (0, 0)),
                  pl.BlockSpec((d, tn), lambda i, j: (0, j))],
        out_specs=pl.BlockSpec((tm, tn), lambda i, j: (i, j)),
        out_shape=jax.ShapeDtypeStruct((t, n), F32),
        scratch_shapes=[pltpu.VMEM((tm, d), F32)],
        compiler_params=_cp(("parallel", "arbitrary")),
        name="norm_proj",
    )(x, g, w)


def _layernorm_silu(u, g, b):
    xc = u - jnp.mean(u, axis=-1, keepdims=True)
    var = jnp.mean(xc * xc, axis=-1, keepdims=True)
    return _silu(xc * lax.rsqrt(var + EPS) * g + b)


def _conf_prompt_kernel(glu_ref, w_ref, b_ref, lg_ref, lb_ref, o_ref, cnew_ref, abuf, *, tm, cw, dc):
    halo = 32

    @pl.when(pl.program_id(1) == 0)
    def _():
        abuf[0:halo, :] = jnp.zeros((halo, dc), F32)

    glu = glu_ref[...]
    abuf[halo:halo + tm, :] = glu[:, :dc] * jax.nn.sigmoid(glu[:, dc:])
    first = halo - (cw - 1)
    u = b_ref[...] + w_ref[0:1, :] * abuf[first:first + tm, :]
    for j in range(1, cw):
        u = u + w_ref[j:j + 1, :] * abuf[first + j:first + j + tm, :]
    o_ref[...] = _layernorm_silu(u, lg_ref[...], lb_ref[...])
    cnew_ref[0] = abuf[tm + first:tm + halo, :]
    abuf[0:halo, :] = abuf[tm:tm + halo, :]


def _conf_prompt(proj, w, b, lg, lb, nseq, seq, tm):
    cw, dc = w.shape
    nt = seq // tm
    kern = functools.partial(_conf_prompt_kernel, tm=tm, cw=cw, dc=dc)
    return pl.pallas_call(
        kern,
        grid=(nseq, nt),
        in_specs=[pl.BlockSpec((tm, 2 * dc), lambda s, i: (s * nt + i, OFF["glu"] // (2 * dc))),
                  pl.BlockSpec((cw, dc), lambda s, i: (0, 0)),
                  pl.BlockSpec((1, dc), lambda s, i: (0, 0)),
                  pl.BlockSpec((1, dc), lambda s, i: (0, 0)),
                  pl.BlockSpec((1, dc), lambda s, i: (0, 0))],
        out_specs=[pl.BlockSpec((tm, dc), lambda s, i: (s * nt + i, 0)),
                   pl.BlockSpec((1, cw - 1, dc), lambda s, i: (s, 0, 0))],
        out_shape=[jax.ShapeDtypeStruct((nseq * seq, dc), F32),
                   jax.ShapeDtypeStruct((nseq, cw - 1, dc), F32)],
        scratch_shapes=[pltpu.VMEM((tm + 32, dc), F32)],
        compiler_params=_cp(("arbitrary", "arbitrary")),
        name="conformer_prompt",
    )(proj, w, b, lg, lb)


def _conf_sample_kernel(glu_ref, st_ref, w_ref, b_ref, lg_ref, lb_ref, o_ref, a_ref, *, cw, dc):
    glu = glu_ref[...]
    a = glu[:, :dc] * jax.nn.sigmoid(glu[:, dc:])
    u = b_ref[...] + w_ref[cw - 1:cw, :] * a
    for j in range(cw - 1):
        u = u + w_ref[j:j + 1, :] * st_ref[j]
    o_ref[...] = _layernorm_silu(u, lg_ref[...], lb_ref[...])
    a_ref[...] = a


def _conf_sample(proj, st_t, w, b, lg, lb):
    cw, dc = w.shape
    n = proj.shape[0]
    kern = functools.partial(_conf_sample_kernel, cw=cw, dc=dc)
    return pl.pallas_call(
        kern,
        grid=(1,),
        in_specs=[pl.BlockSpec((n, 2 * dc), lambda i: (0, OFF["glu"] // (2 * dc))),
                  pl.BlockSpec((cw - 1, n, dc), lambda i: (0, 0, 0)),
                  pl.BlockSpec((cw, dc), lambda i: (0, 0)),
                  pl.BlockSpec((1, dc), lambda i: (0, 0)),
                  pl.BlockSpec((1, dc), lambda i: (0, 0)),
                  pl.BlockSpec((1, dc), lambda i: (0, 0))],
        out_specs=[pl.BlockSpec((n, dc), lambda i: (0, 0)),
                   pl.BlockSpec((n, dc), lambda i: (0, 0))],
        out_shape=[jax.ShapeDtypeStruct((n, dc), F32), jax.ShapeDtypeStruct((n, dc), F32)],
        compiler_params=_cp(("arbitrary",)),
        name="conformer_sample",
    )(proj, st_t, w, b, lg, lb)


def _rope128(x, c, s):
    lane = lax.broadcasted_iota(I32, x.shape, 1) % HEAD_DIM
    partner = jnp.where(lane < 8, pltpu.roll(x, LANES - 8, 1), pltpu.roll(x, 8, 1))
    return x * c + partner * s


def _dsa_prep_kernel(q_ref, qi_ref, k_ref, v_ref, sm_ref, qg_ref, kg_ref, invf_ref, sgn_ref,
                     invf_s_ref, sgn_s_ref, bd_ref,
                     qn_ref, kn_ref, vo_ref, qir_ref, smr_ref, kir_ref, *, tm, seq, const_pos):
    if const_pos is None:
        base = (pl.program_id(0) * tm) % seq
        pos = (base + lax.broadcasted_iota(I32, (tm, LANES), 0)).astype(F32)
    else:
        pos = jnp.full((tm, LANES), const_pos, F32)
    ang = pos * invf_ref[...]
    c = jnp.cos(ang)
    s = jnp.sin(ang) * sgn_ref[...]
    ang_s = pos * invf_s_ref[...]
    c_s = jnp.cos(ang_s)
    s_s = jnp.sin(ang_s) * sgn_s_ref[...]

    def seg_rms(x, bd):
        x2 = x * x
        hi = x2.astype(BF16)
        lo = (x2 - hi.astype(F32)).astype(BF16)
        ms = (_dot(hi, bd) + _dot(lo, bd)) * (1.0 / HEAD_DIM)
        return x * lax.rsqrt(ms + EPS)

    qn = seg_rms(q_ref[...], bd_ref[...]) * qg_ref[...]
    qi = qi_ref[...]
    for j in range(q_ref.shape[1] // LANES):
        sl = slice(j * LANES, (j + 1) * LANES)
        qn_ref[:, sl] = _rope128(qn[:, sl], c, s)
        qir_ref[:, sl] = _rope128(qi[:, sl], c, s)
    kn = seg_rms(k_ref[...], bd_ref[0:LANES, 0:LANES]) * kg_ref[...]
    kn_ref[...] = _rope128(kn, c, s)
    vo_ref[...] = v_ref[...]
    smr = _rope128(sm_ref[...], c_s, s_s)
    smr_ref[...] = smr
    kir_ref[...] = smr[:, 0:D_IDX]


def _dsa_prep(proj, qg, kg, consts, tm, seq, const_pos):
    t = proj.shape[0]
    dq = N_HEADS * HEAD_DIM
    dk = N_KV_HEADS * HEAD_DIM
    kern = functools.partial(_dsa_prep_kernel, tm=tm, seq=seq, const_pos=const_pos)
    row = lambda w: pl.BlockSpec((1, w), lambda i: (0, 0))
    return pl.pallas_call(
        kern,
        grid=(t // tm,),
        in_specs=[pl.BlockSpec((tm, dq), lambda i: (i, OFF["q"] // dq)),
                  pl.BlockSpec((tm, dq), lambda i: (i, OFF["qi"] // dq)),
                  pl.BlockSpec((tm, dk), lambda i: (i, OFF["k"] // dk)),
                  pl.BlockSpec((tm, dk), lambda i: (i, OFF["v"] // dk)),
                  pl.BlockSpec((tm, LANES), lambda i: (i, OFF["small"] // LANES)),
                  row(dq), row(dk), row(LANES), row(LANES), row(LANES), row(LANES),
                  pl.BlockSpec((dq, dq), lambda i: (0, 0))],
        out_specs=[pl.BlockSpec((tm, dq), lambda i: (i, 0)),
                   pl.BlockSpec((tm, dk), lambda i: (i, 0)),
                   pl.BlockSpec((tm, dk), lambda i: (i, 0)),
                   pl.BlockSpec((tm, dq), lambda i: (i, 0)),
                   pl.BlockSpec((tm, LANES), lambda i: (i, 0)),
                   pl.BlockSpec((tm, D_IDX), lambda i: (i, 0))],
        out_shape=[jax.ShapeDtypeStruct((t, dq), F32), jax.ShapeDtypeStruct((t, dk), F32),
                   jax.ShapeDtypeStruct((t, dk), F32), jax.ShapeDtypeStruct((t, dq), F32),
                   jax.ShapeDtypeStruct((t, LANES), F32), jax.ShapeDtypeStruct((t, D_IDX), F32)],
        compiler_params=_cp(("parallel",)),
        name="dsa_prep",
    )(proj, proj, proj, proj, proj, qg, kg, consts["invf"], consts["sgn"], consts["invf_s"],
      consts["sgn_s"], consts["bd"])


def _dsa_prompt_kernel(q_ref, qi_ref, smq_ref, smk_ref, k_ref, v_ref, o_ref,
                       keys_ref, qs_ref, qis_ref, m_ref, l_ref, acc_ref, thr_ref, y_ref,
                       *, tq, ck, ksel, nbits):
    i = pl.program_id(1)
    nc = (i * tq + tq + ck - 1) // ck
    ng = ck // LANES
    lane = lax.broadcasted_iota(I32, (tq, LANES), 1)
    lo = lane < HEAD_DIM
    rowg = i * tq + lax.broadcasted_iota(I32, (tq, LANES), 0)
    scale = HEAD_DIM ** -0.5

    for j in range(N_HEADS // 2):
        sl = slice(j * LANES, (j + 1) * LANES)
        qi2 = qi_ref[:, sl]
        qis_ref[2 * j] = jnp.where(lo, qi2, 0.0)
        qis_ref[2 * j + 1] = jnp.where(lo, pltpu.roll(qi2, HEAD_DIM, 1), 0.0)
        q2 = q_ref[:, sl] * scale
        q2r = pltpu.roll(q2, HEAD_DIM, 1)
        if (2 * j) // (N_HEADS // N_KV_HEADS) == 0:
            qs_ref[2 * j] = jnp.where(lo, q2, 0.0)
            qs_ref[2 * j + 1] = jnp.where(lo, q2r, 0.0)
        else:
            qs_ref[2 * j] = jnp.where(lo, 0.0, q2r)
            qs_ref[2 * j + 1] = jnp.where(lo, 0.0, q2)

    def p1(c, carry):
        kc = smk_ref[pl.ds(pl.multiple_of(c * ck, ck), ck), :]
        acc = jnp.zeros((tq, ck), F32)
        for h in range(N_IDX_HEADS):
            s = _nt_dot(qis_ref[h], kc)
            acc = acc + jnp.maximum(s, 0.0) * smq_ref[:, SM_WI + h:SM_WI + h + 1]
        sc = acc * IDX_SCALE
        for g in range(ng):
            colg = c * ck + g * LANES + lane
            scg = jnp.where(colg <= rowg, sc[:, g * LANES:(g + 1) * LANES], -jnp.inf)
            keys_ref[c, :, g * LANES:(g + 1) * LANES] = _sortable(scg)
        return carry

    lax.fori_loop(0, nc, p1, 0)

    def count(pred):
        def body(c, part):
            for g in range(ng):
                colg = c * ck + g * LANES + lane
                part = part + pred(keys_ref[c, :, g * LANES:(g + 1) * LANES], colg)
            return part

        part = lax.fori_loop(0, nc, body, jnp.zeros((tq, LANES), I32))
        return jnp.sum(part, axis=1, keepdims=True)

    def bit_step(t, thr):
        cand = thr + lax.shift_left(jnp.int32(1), 31 - t)
        cnt = count(lambda kv, colg: jnp.where(kv >= cand, 1, 0))
        return jnp.where(cnt >= ksel, cand, thr)

    thr = lax.fori_loop(0, 32, bit_step, jnp.full((tq, LANES), INT_MIN, I32))
    thr_ref[...] = thr

    n_gt = count(lambda kv, colg: jnp.where(kv > thr, 1, 0))
    n_eq = count(lambda kv, colg: jnp.where(kv == thr, 1, 0))
    need = ksel - n_gt
    y_ref[...] = jnp.full((tq, LANES), 2 ** 30, I32)
    excess = jnp.max(jnp.where(n_eq > need, 1, 0))

    @pl.when(excess > 0)
    def _():
        def y_step(t, y):
            cand = y + lax.shift_left(jnp.int32(1), nbits - 1 - t)
            g = count(lambda kv, colg: jnp.where(kv == thr, jnp.where(colg < cand, 1, 0), 0))
            return jnp.where(g < need, cand, y)

        y_ref[...] = lax.fori_loop(0, nbits, y_step, jnp.zeros((tq, LANES), I32))

    m_ref[...] = jnp.full(m_ref.shape, NEG_BIG, F32)
    l_ref[...] = jnp.zeros(l_ref.shape, F32)
    acc_ref[...] = jnp.zeros(acc_ref.shape, F32)

    def p3(c, carry):
        thr_v = thr_ref[...]
        y_v = y_ref[...]
        biases = []
        for g in range(ng):
            colg = c * ck + g * LANES + lane
            kv = keys_ref[c, :, g * LANES:(g + 1) * LANES]
            sel = jnp.where(kv > thr_v, 1, jnp.where(kv == thr_v, jnp.where(colg <= y_v, 1, 0), 0))
            sel = jnp.where(colg <= rowg, sel, 0)
            biases.append(jnp.where(sel > 0, 0.0, NEG_BIG))
        bias = jnp.concatenate(biases, axis=1)
        off = pl.multiple_of(c * ck, ck)
        kk = k_ref[pl.ds(off, ck), :]
        vv = v_ref[pl.ds(off, ck), :]
        for h in range(N_HEADS):
            s = _nt_dot(qs_ref[h], kk) + bias
            m_old = m_ref[h]
            m_new = jnp.maximum(m_old, jnp.max(s, axis=1, keepdims=True))
            alpha = jnp.exp(m_old - m_new)
            p = jnp.exp(s - m_new)
            l_ref[h] = alpha * l_ref[h] + jnp.sum(p, axis=1, keepdims=True)
            acc_ref[h] = alpha * acc_ref[h] + _dot(p, vv)
            m_ref[h] = m_new
        return carry

    lax.fori_loop(0, nc, p3, 0)

    for j in range(N_HEADS // 2):
        ea = acc_ref[2 * j] / l_ref[2 * j]
        eb = acc_ref[2 * j + 1] / l_ref[2 * j + 1]
        if (2 * j) // (N_HEADS // N_KV_HEADS) == 0:
            out2 = jnp.where(lo, ea, pltpu.roll(eb, HEAD_DIM, 1))
        else:
            out2 = jnp.where(lo, pltpu.roll(ea, HEAD_DIM, 1), eb)
        o_ref[:, j * LANES:(j + 1) * LANES] = out2


def _dsa_prompt(qn, qir, smr, kn, v, nseq, seq, tq, ck):
    dq = N_HEADS * HEAD_DIM
    dk = N_KV_HEADS * HEAD_DIM
    nq = seq // tq
    ksel = min(TOP_K, seq // 4)
    assert ck >= ksel and ck % LANES == 0 and seq % ck == 0 and seq % tq == 0
    nbits = max(1, int(math.ceil(math.log2(seq))))
    kern = functools.partial(_dsa_prompt_kernel, tq=tq, ck=ck, ksel=ksel, nbits=nbits)
    return pl.pallas_call(
        kern,
        grid=(nseq, nq),
        in_specs=[pl.BlockSpec((tq, dq), lambda s, i: (s * nq + i, 0)),
                  pl.BlockSpec((tq, dq), lambda s, i: (s * nq + i, 0)),
                  pl.BlockSpec((tq, LANES), lambda s, i: (s * nq + i, 0)),
                  pl.BlockSpec((seq, LANES), lambda s, i: (s, 0)),
                  pl.BlockSpec((seq, dk), lambda s, i: (s, 0)),
                  pl.BlockSpec((seq, dk), lambda s, i: (s, 0))],
        out_specs=pl.BlockSpec((tq, dq), lambda s, i: (s * nq + i, 0)),
        out_shape=jax.ShapeDtypeStruct((nseq * seq, dq), F32),
        scratch_shapes=[pltpu.VMEM((seq // ck, tq, ck), I32),
                        pltpu.VMEM((N_HEADS, tq, LANES), F32),
                        pltpu.VMEM((N_IDX_HEADS, tq, LANES), F32),
                        pltpu.VMEM((N_HEADS, tq, 1), F32),
                        pltpu.VMEM((N_HEADS, tq, 1), F32),
                        pltpu.VMEM((N_HEADS, tq, LANES), F32),
                        pltpu.VMEM((tq, LANES), I32),
                        pltpu.VMEM((tq, LANES), I32)],
        compiler_params=_cp(("arbitrary", "arbitrary")),
        name="dsa_prompt",
    )(qn, qir, smr, smr, kn, v)


def _dsa_sample_kernel(pt_ref, q8_ref, qi8_ref, wi_ref, kin_ref, kn_ref, vn_ref,
                       cki_hbm, ck_hbm, cv_hbm, o_ref, kib, kb, vb, sem,
                       *, n_pages, page, ksel, nbits):
    b = pl.program_id(0)
    past = n_pages * page

    def copies(p):
        pg = pt_ref[b, p]
        rows = pl.ds(pl.multiple_of(p * page, page), page)
        return (pltpu.make_async_copy(cki_hbm.at[pg], kib.at[rows, :], sem.at[0]),
                pltpu.make_async_copy(ck_hbm.at[pg], kb.at[rows, :], sem.at[1]),
                pltpu.make_async_copy(cv_hbm.at[pg], vb.at[rows, :], sem.at[2]))

    def start(p, carry):
        for cp in copies(p):
            cp.start()
        return carry

    def wait(p, carry):
        for cp in copies(p):
            cp.wait()
        return carry

    lax.fori_loop(0, n_pages, start, 0)
    lax.fori_loop(0, n_pages, wait, 0)

    wcol = wi_ref[0]
    qi8 = qi8_ref[0]
    s = _nt_dot(qi8, kib[...])
    sc = jnp.sum(jnp.maximum(s, 0.0) * wcol, axis=0, keepdims=True) * IDX_SCALE
    s_new = jnp.sum(qi8 * kin_ref[0], axis=1, keepdims=True)
    sc_new = jnp.sum(jnp.maximum(s_new, 0.0) * wcol, axis=0, keepdims=True) * IDX_SCALE
    keys = _sortable(sc)
    key_new = _sortable(sc_new)
    col = lax.broadcasted_iota(I32, (1, past), 1)

    def cnt(main, new):
        return jnp.sum(main, axis=1, keepdims=True) + new

    def bit_step(t, thr):
        cand = thr + lax.shift_left(jnp.int32(1), 31 - t)
        c = cnt(jnp.where(keys >= cand, 1, 0), jnp.where(key_new >= cand, 1, 0))
        return jnp.where(c >= ksel, cand, thr)

    thr = lax.fori_loop(0, 32, bit_step, jnp.full((1, 1), INT_MIN, I32))
    need = ksel - cnt(jnp.where(keys > thr, 1, 0), jnp.where(key_new > thr, 1, 0))

    def y_step(t, y):
        cand = y + lax.shift_left(jnp.int32(1), nbits - 1 - t)
        g = cnt(jnp.where(keys == thr, jnp.where(col < cand, 1, 0), 0),
                jnp.where(key_new == thr, jnp.where(past < cand, 1, 0), 0))
        return jnp.where(g < need, cand, y)

    y = lax.fori_loop(0, nbits, y_step, jnp.zeros((1, 1), I32))
    sel = jnp.where(keys > thr, 1, jnp.where(keys == thr, jnp.where(col <= y, 1, 0), 0))
    sel_new = jnp.where(key_new > thr, 1, jnp.where(key_new == thr, jnp.where(past <= y, 1, 0), 0))
    bias = jnp.where(sel > 0, 0.0, NEG_BIG)
    bias_new = jnp.where(sel_new > 0, 0.0, NEG_BIG)

    q8 = q8_ref[0] * (HEAD_DIM ** -0.5)
    sa = _nt_dot(q8, kb[...]) + bias
    sa_new = jnp.sum(q8 * kn_ref[0], axis=1, keepdims=True) + bias_new
    m = jnp.maximum(jnp.max(sa, axis=1, keepdims=True), sa_new)
    p = jnp.exp(sa - m)
    p_new = jnp.exp(sa_new - m)
    l = jnp.sum(p, axis=1, keepdims=True) + p_new
    o_ref[0] = (_dot(p, vb[...]) + p_new * vn_ref[0]) / l


def _dsa_sample(page_table, q8, qi8, wi, ki_new, k_new, v_new, cki, ck, cv):
    nb, n_pages = page_table.shape
    page = cki.shape[1]
    past = n_pages * page
    dk = N_KV_HEADS * HEAD_DIM
    ksel = min(TOP_K, (past + 1) // 4)
    nbits = int(math.floor(math.log2(past))) + 1
    kern = functools.partial(_dsa_sample_kernel, n_pages=n_pages, page=page, ksel=ksel, nbits=nbits)
    grid_spec = pltpu.PrefetchScalarGridSpec(
        num_scalar_prefetch=1,
        grid=(nb,),
        in_specs=[pl.BlockSpec((1, N_HEADS, dk), lambda b, pt: (b, 0, 0)),
                  pl.BlockSpec((1, N_IDX_HEADS, D_IDX), lambda b, pt: (b, 0, 0)),
                  pl.BlockSpec((1, N_IDX_HEADS, 1), lambda b, pt: (b, 0, 0)),
                  pl.BlockSpec((1, 1, D_IDX), lambda b, pt: (b, 0, 0)),
                  pl.BlockSpec((1, 1, dk), lambda b, pt: (b, 0, 0)),
                  pl.BlockSpec((1, 1, dk), lambda b, pt: (b, 0, 0)),
                  pl.BlockSpec(memory_space=pl.ANY),
                  pl.BlockSpec(memory_space=pl.ANY),
                  pl.BlockSpec(memory_space=pl.ANY)],
        out_specs=pl.BlockSpec((1, N_HEADS, dk), lambda b, pt: (b, 0, 0)),
        scratch_shapes=[pltpu.VMEM((past, D_IDX), F32),
                        pltpu.VMEM((past, dk), F32),
                        pltpu.VMEM((past, dk), F32),
                        pltpu.SemaphoreType.DMA((3,))],
    )
    return pl.pallas_call(
        kern,
        grid_spec=grid_spec,
        out_shape=jax.ShapeDtypeStruct((nb, N_HEADS, dk), F32),
        compiler_params=_cp(("arbitrary",)),
        name="dsa_sample",
    )(page_table, q8, qi8, wi, ki_new, k_new, v_new, cki, ck, cv)


def _ssd_prompt_kernel(xbc_ref, z_ref, sm_ref, cw_ref, cb_ref, dtb_ref, alog_ref, dskip_ref, ng_ref,
                       tril_ref, e64_ref, e128_ref, bmask_ref,
                       o_ref, cnew_ref, sst_ref, xbuf, st_ref, *, ts, kw, dxbc):
    halo = 8
    di = SSM_HEADS * SSM_HEAD_DIM
    dbc = SSM_GROUPS * D_STATE

    @pl.when(pl.program_id(1) == 0)
    def _():
        xbuf[0:halo, :] = jnp.zeros((halo, dxbc), F32)
        st_ref[...] = jnp.zeros(st_ref.shape, F32)

    xbuf[halo:halo + ts, :] = xbc_ref[...]
    first = halo - (kw - 1)
    conv = cb_ref[...] + cw_ref[0:1, :] * xbuf[first:first + ts, :]
    for j in range(1, kw):
        conv = conv + cw_ref[j:j + 1, :] * xbuf[first + j:first + j + ts, :]
    xc = _silu(conv)
    cnew_ref[0] = xbuf[ts + first:ts + halo, :]
    xbuf[0:halo, :] = xbuf[ts:ts + halo, :]

    lane = lax.broadcasted_iota(I32, (SSM_CHUNK, LANES), 1)
    head_lane = (lane[0:1, :] >= SM_DT) & (lane[0:1, :] < SM_DT + SSM_HEADS)
    a_row = jnp.where(head_lane, -jnp.exp(alog_ref[...]), 0.0)
    tri = lax.broadcasted_iota(I32, (SSM_CHUNK, SSM_CHUNK), 0) >= lax.broadcasted_iota(
        I32, (SSM_CHUNK, SSM_CHUNK), 1)
    glo = lane < D_STATE

    for k in range(ts // SSM_CHUNK):
        rows = slice(k * SSM_CHUNK, (k + 1) * SSM_CHUNK)
        dtf = _softplus(sm_ref[rows, :] + dtb_ref[...])
        adt = dtf * a_row
        a_cs = _sel_dot(tril_ref[...], adt)
        a_cs_t = a_cs.T
        acs_b = _dot_sel(a_cs, e128_ref[...])
        acs_f = _dot_sel(a_cs, e64_ref[...])
        dt_f = _dot_sel(dtf, e64_ref[...])
        alast_f = acs_f[SSM_CHUNK - 1:SSM_CHUNK, :]
        xs = xc[rows, 0:di]
        bm = xc[rows, di:di + dbc]
        cm = xc[rows, di + dbc:di + 2 * dbc]
        xdt = xs * dt_f
        xd = xdt * jnp.exp(alast_f - acs_f)
        bt = bm.T
        cb = (_dot(jnp.where(glo, cm, 0.0), bt), _dot(jnp.where(glo, 0.0, cm), bt))
        pairs = []
        for j in range(SSM_HEADS // 2):
            x2 = xdt[:, j * LANES:(j + 1) * LANES]
            acc = None
            for hh in range(2):
                h = 2 * j + hh
                seg = acs_b[:, h * LANES:(h + 1) * LANES] - a_cs_t[SM_DT + h:SM_DT + h + 1, :]
                lm = jnp.exp(jnp.where(tri, seg, -jnp.inf))
                sc = cb[h // (SSM_HEADS // SSM_GROUPS)] * lm
                xm = jnp.where(glo, x2, 0.0) if hh == 0 else jnp.where(glo, 0.0, x2)
                part = _dot(sc, xm)
                acc = part if acc is None else acc + part
            pairs.append(acc)
        y = jnp.concatenate(pairs, axis=1)
        y = y + _dot(cm, st_ref[...]) * jnp.exp(acs_f) + dskip_ref[...] * xs
        st_ref[...] = st_ref[...] * jnp.exp(alast_f) + bmask_ref[...] * _dot(bt, xd)
        yg = y * _silu(z_ref[rows, :])
        o_ref[rows, :] = _rms(yg) * ng_ref[...]
    sst_ref[0] = st_ref[...]


def _ssd_prompt(proj, cw, cb, dtb_row, alog_row, dskip_f, ng, consts, nseq, seq, ts):
    kw, dxbc = cw.shape
    di = SSM_HEADS * SSM_HEAD_DIM
    nt = seq // ts
    kern = functools.partial(_ssd_prompt_kernel, ts=ts, kw=kw, dxbc=dxbc)
    full = lambda a: pl.BlockSpec(a.shape, lambda s, i: (0,) * a.ndim)
    cs = (consts["tril"], consts["e64"], consts["e128"], consts["bmask"])
    return pl.pallas_call(
        kern,
        grid=(nseq, nt),
        in_specs=[pl.BlockSpec((ts, dxbc), lambda s, i: (s * nt + i, OFF["xbc"] // dxbc)),
                  pl.BlockSpec((ts, di), lambda s, i: (s * nt + i, OFF["z"] // di)),
                  pl.BlockSpec((ts, LANES), lambda s, i: (s * nt + i, OFF["small"] // LANES)),
                  full(cw), full(cb), full(dtb_row), full(alog_row), full(dskip_f), full(ng)]
                 + [full(c) for c in cs],
        out_specs=[pl.BlockSpec((ts, di), lambda s, i: (s * nt + i, 0)),
                   pl.BlockSpec((1, kw - 1, dxbc), lambda s, i: (s, 0, 0)),
                   pl.BlockSpec((1, SSM_GROUPS * D_STATE, di), lambda s, i: (s, 0, 0))],
        out_shape=[jax.ShapeDtypeStruct((nseq * seq, di), F32),
                   jax.ShapeDtypeStruct((nseq, kw - 1, dxbc), F32),
                   jax.ShapeDtypeStruct((nseq, SSM_GROUPS * D_STATE, di), F32)],
        scratch_shapes=[pltpu.VMEM((ts + 8, dxbc), F32),
                        pltpu.VMEM((SSM_GROUPS * D_STATE, di), F32)],
        compiler_params=_cp(("arbitrary", "arbitrary")),
        name="ssd_prompt",
    )(proj, proj, proj, cw, cb, dtb_row, alog_row, dskip_f, ng, *cs)


def _ssd_sample_kernel(xbc_ref, z_ref, sm_ref, st_ref, h0_ref, cw_ref, cb_ref, dtb_ref, alog_ref,
                       dskip_ref, ng_ref, e64_ref, o_ref, hn_ref, y_ref, *, nb, kw):
    di = SSM_HEADS * SSM_HEAD_DIM
    dbc = SSM_GROUPS * D_STATE
    conv = cb_ref[...] + cw_ref[kw - 1:kw, :] * xbc_ref[...]
    for j in range(kw - 1):
        conv = conv + cw_ref[j:j + 1, :] * st_ref[j]
    xc = _silu(conv)
    lane = lax.broadcasted_iota(I32, (1, LANES), 1)
    head_lane = (lane >= SM_DT) & (lane < SM_DT + SSM_HEADS)
    a_row = jnp.where(head_lane, -jnp.exp(alog_ref[...]), 0.0)
    dtf = _softplus(sm_ref[...] + dtb_ref[...])
    dec = jnp.exp(dtf * a_row)
    dt_f = _dot_sel(dtf, e64_ref[...])
    dec_f = _dot_sel(dec, e64_ref[...])
    xs = xc[:, 0:di]
    bm = xc[:, di:di + dbc]
    cm = xc[:, di + dbc:di + 2 * dbc]
    pad = jnp.zeros((LANES - nb, di), F32)
    xdt_t = jnp.concatenate([xs * dt_f, pad], axis=0).T
    dec_t = jnp.concatenate([dec_f, pad], axis=0).T
    bm_r = pltpu.roll(bm, D_STATE, 1)
    cm_r = pltpu.roll(cm, D_STATE, 1)
    rowi = lax.broadcasted_iota(I32, (di, D_STATE), 0)
    g0 = rowi < (SSM_HEADS // SSM_GROUPS) * SSM_HEAD_DIM
    lane_y = lax.broadcasted_iota(I32, (1, di), 1) < (SSM_HEADS // SSM_GROUPS) * SSM_HEAD_DIM
    row8 = lax.broadcasted_iota(I32, (8, D_STATE), 0)
    for b in range(nb):
        bsel = jnp.where(g0, bm[b:b + 1, 0:D_STATE], bm_r[b:b + 1, 0:D_STATE])
        hn = h0_ref[b] * dec_t[:, b:b + 1] + xdt_t[:, b:b + 1] * bsel
        hn_ref[b] = hn
        c2 = jnp.where(row8 == 0, cm[b:b + 1, 0:D_STATE],
                       jnp.where(row8 == 1, cm_r[b:b + 1, 0:D_STATE], 0.0))
        yr = _nt_dot(c2, hn)
        y_ref[b:b + 1, :] = jnp.where(lane_y, yr[0:1, :], yr[1:2, :])
    y = y_ref[...] + dskip_ref[...] * xs
    yg = y * _silu(z_ref[...])
    o_ref[...] = _rms(yg) * ng_ref[...]


def _ssd_sample(proj, st_t, h0, cw, cb, dtb_row, alog_row, dskip_f, ng, consts):
    kw, dxbc = cw.shape
    nb = proj.shape[0]
    di = SSM_HEADS * SSM_HEAD_DIM
    kern = functools.partial(_ssd_sample_kernel, nb=nb, kw=kw)
    full = lambda a: pl.BlockSpec(a.shape, lambda i: (0,) * a.ndim)
    return pl.pallas_call(
        kern,
        grid=(1,),
        in_specs=[pl.BlockSpec((nb, dxbc), lambda i: (0, OFF["xbc"] // dxbc)),
                  pl.BlockSpec((nb, di), lambda i: (0, OFF["z"] // di)),
                  pl.BlockSpec((nb, LANES), lambda i: (0, OFF["small"] // LANES)),
                  full(st_t), full(h0), full(cw), full(cb), full(dtb_row), full(alog_row),
                  full(dskip_f), full(ng), full(consts["e64"])],
        out_specs=[pl.BlockSpec((nb, di), lambda i: (0, 0)),
                   pl.BlockSpec(h0.shape, lambda i: (0, 0, 0))],
        out_shape=[jax.ShapeDtypeStruct((nb, di), F32), jax.ShapeDtypeStruct(h0.shape, F32)],
        scratch_shapes=[pltpu.VMEM((nb, di), F32)],
        compiler_params=_cp(("arbitrary",)),
        name="ssd_sample",
    )(proj, proj, proj, st_t, h0, cw, cb, dtb_row, alog_row, dskip_f, ng, consts["e64"])


def _mem_kv_kernel(x_ref, g_ref, w_ref, kg_ref, mk_ref, mv_ref):
    dm = mk_ref.shape[1]
    m = _dot(_rms(x_ref[...]) * g_ref[...], w_ref[...])
    for h in range(MEM_HEADS):
        sl = slice(h * MEM_HEAD_DIM, (h + 1) * MEM_HEAD_DIM)
        mk_ref[:, sl] = _rms(m[:, sl]) * kg_ref[...]
    mv_ref[...] = m[:, dm:]


def _mem_kv(mem2d, g, w, kg, rows):
    t, d = mem2d.shape
    dm = MEM_HEADS * MEM_HEAD_DIM
    return pl.pallas_call(
        _mem_kv_kernel,
        grid=(t // rows,),
        in_specs=[pl.BlockSpec((rows, d), lambda i: (i, 0)),
                  pl.BlockSpec((1, d), lambda i: (0, 0)),
                  pl.BlockSpec((d, 2 * dm), lambda i: (0, 0)),
                  pl.BlockSpec((1, MEM_HEAD_DIM), lambda i: (0, 0))],
        out_specs=[pl.BlockSpec((rows, dm), lambda i: (i, 0)),
                   pl.BlockSpec((rows, dm), lambda i: (i, 0))],
        out_shape=[jax.ShapeDtypeStruct((t, dm), F32), jax.ShapeDtypeStruct((t, dm), F32)],
        compiler_params=_cp(("parallel",)),
        name="mem_kv",
    )(mem2d, g, w, kg)


def _mem_attend_kernel(mq_ref, mk_ref, mv_ref, g_ref, o_ref, *, tm):
    mq = mq_ref[0]
    rows = max(tm, 8)
    if tm < rows:
        mq = jnp.broadcast_to(mq, (rows, mq.shape[1]))
    for h in range(MEM_HEADS):
        sl = slice(h * MEM_HEAD_DIM, (h + 1) * MEM_HEAD_DIM)
        qn = _rms(mq[:, sl]) * g_ref[...]
        s = _nt_dot(qn, mk_ref[0, :, sl]) * (MEM_HEAD_DIM ** -0.5)
        p = jnp.exp(s - jnp.max(s, axis=1, keepdims=True))
        o = _dot(p, mv_ref[0, :, sl]) / jnp.sum(p, axis=1, keepdims=True)
        o_ref[0, :, sl] = o[0:tm, :]


def _mem_attend(proj3, mk3, mv3, g, tm):
    nseq, seq, _ = proj3.shape
    nm = mk3.shape[1]
    dm = MEM_HEADS * MEM_HEAD_DIM
    kern = functools.partial(_mem_attend_kernel, tm=tm)
    return pl.pallas_call(
        kern,
        grid=(nseq, seq // tm),
        in_specs=[pl.BlockSpec((1, tm, dm), lambda s, i: (s, i, OFF["mq"] // dm)),
                  pl.BlockSpec((1, nm, dm), lambda s, i: (s, 0, 0)),
                  pl.BlockSpec((1, nm, dm), lambda s, i: (s, 0, 0)),
                  pl.BlockSpec((1, MEM_HEAD_DIM), lambda s, i: (0, 0))],
        out_specs=pl.BlockSpec((1, tm, dm), lambda s, i: (s, i, 0)),
        out_shape=jax.ShapeDtypeStruct((nseq, seq, dm), F32),
        compiler_params=_cp(("parallel", "arbitrary")),
        name="mem_attend",
    )(proj3, mk3, mv3, g)


def _merge_kernel(x_ref, ba_ref, bb_ref, bc_ref, bm_ref, gt_ref, wb_ref, wo_ref, o_ref):
    d = x_ref.shape[1]
    acc = None
    for n, br in enumerate((ba_ref, bb_ref, bc_ref, bm_ref)):
        term = jax.nn.sigmoid(gt_ref[:, n * d:(n + 1) * d]) * _dot(br[...], wb_ref[n])
        acc = term if acc is None else acc + term
    o_ref[...] = x_ref[...] + _dot(acc, wo_ref[...])


def _merge(x, proj, brs, wb, wo, tm):
    t, d = x.shape
    nbr, bw, _ = wb.shape
    return pl.pallas_call(
        _merge_kernel,
        grid=(t // tm,),
        in_specs=[pl.BlockSpec((tm, d), lambda i: (i, 0))]
                 + [pl.BlockSpec((tm, bw), lambda i: (i, 0))] * nbr
                 + [pl.BlockSpec((tm, nbr * d), lambda i: (i, OFF["gates"] // (nbr * d))),
                    pl.BlockSpec((nbr, bw, d), lambda i: (0, 0, 0)),
                    pl.BlockSpec((d, d), lambda i: (0, 0))],
        out_specs=pl.BlockSpec((tm, d), lambda i: (i, 0)),
        out_shape=jax.ShapeDtypeStruct((t, d), F32),
        compiler_params=_cp(("parallel",)),
        name="merge",
    )(x, *brs, proj, wb, wo)


def _ffn_prompt_kernel(x_ref, g_ref, wg_ref, wu_ref, cwg_ref, cwu_ref, cbg_ref, cbu_ref, wd_ref,
                       o_ref, unew_ref, h_ref, acc_ref, ubuf, carry, *, tm, tc, kw, nff):
    i = pl.program_id(1)
    c = pl.program_id(2)
    halo = 8
    first = halo - (kw - 1)

    @pl.when(c == 0)
    def _():
        h_ref[...] = _rms(x_ref[...]) * g_ref[...]
        acc_ref[...] = jnp.zeros(acc_ref.shape, F32)

    @pl.when(i == 0)
    def _():
        carry[c] = jnp.zeros((halo, 2 * tc), F32)

    h = h_ref[...]
    ubuf[0:halo, :] = carry[c]
    ubuf[halo:halo + tm, 0:tc] = _dot(h, wg_ref[...])
    ubuf[halo:halo + tm, tc:2 * tc] = _dot(h, wu_ref[...])
    carry[c] = ubuf[tm:tm + halo, :]
    fg =cbg_ref[...] + cwg_ref[0:1, :] * ubuf[first:first + tm, 0:tc]
    fu = cbu_ref[...] + cwu_ref[0:1, :] * ubuf[first:first + tm, tc:2 * tc]
    for j in range(1, kw):
        fg = fg + cwg_ref[j:j + 1, :] * ubuf[first + j:first + j + tm, 0:tc]
        fu = fu + cwu_ref[j:j + 1, :] * ubuf[first + j:first + j + tm, tc:2 * tc]
    acc_ref[...] += _dot(_silu(fg) * fu, wd_ref[...])

    @pl.when(c == nff - 1)
    def _():
        o_ref[...] = x_ref[...] + acc_ref[...]

    @pl.when((c == nff - 1) & (i == pl.num_programs(1) - 1))
    def _():
        for cc in range(nff):
            unew_ref[0, :, cc * tc:(cc + 1) * tc] = carry[cc, first:halo, 0:tc]
            unew_ref[0, :, (nff + cc) * tc:(nff + cc + 1) * tc] = carry[cc, first:halo, tc:2 * tc]


def _ffn_prompt(x, g, wup, cw, cb, wd, nseq, seq, tm, tc):
    t, d = x.shape
    dff = wd.shape[0]
    kw = cw.shape[0]
    nt = seq // tm
    nff = dff // tc
    kern = functools.partial(_ffn_prompt_kernel, tm=tm, tc=tc, kw=kw, nff=nff)
    return pl.pallas_call(
        kern,
        grid=(nseq, nt, nff),
        in_specs=[pl.BlockSpec((tm, d), lambda s, i, c: (s * nt + i, 0)),
                  pl.BlockSpec((1, d), lambda s, i, c: (0, 0)),
                  pl.BlockSpec((d, tc), lambda s, i, c: (0, c)),
                  pl.BlockSpec((d, tc), lambda s, i, c: (0, nff + c)),
                  pl.BlockSpec((kw, tc), lambda s, i, c: (0, c)),
                  pl.BlockSpec((kw, tc), lambda s, i, c: (0, nff + c)),
                  pl.BlockSpec((1, tc), lambda s, i, c: (0, c)),
                  pl.BlockSpec((1, tc), lambda s, i, c: (0, nff + c)),
                  pl.BlockSpec((tc, d), lambda s, i, c: (c, 0))],
        out_specs=[pl.BlockSpec((tm, d), lambda s, i, c: (s * nt + i, 0)),
                   pl.BlockSpec((1, kw - 1, 2 * dff), lambda s, i, c: (s, 0, 0))],
        out_shape=[jax.ShapeDtypeStruct((t, d), F32),
                   jax.ShapeDtypeStruct((nseq, kw - 1, 2 * dff), F32)],
        scratch_shapes=[pltpu.VMEM((tm, d), F32), pltpu.VMEM((tm, d), F32),
                        pltpu.VMEM((tm + 8, 2 * tc), F32), pltpu.VMEM((nff, 8, 2 * tc), F32)],
        compiler_params=_cp(("arbitrary", "arbitrary", "arbitrary")),
        name="ffn_prompt",
    )(x, g, wup, wup, cw, cw, cb, cb, wd)


def _ffn_sample_kernel(x_ref, g_ref, wg_ref, wu_ref, stg_ref, stu_ref, cwg_ref, cwu_ref, cbg_ref,
                       cbu_ref, wd_ref, o_ref, ug_ref, uu_ref, h_ref, acc_ref, *, kw):
    c = pl.program_id(0)

    @pl.when(c == 0)
    def _():
        h_ref[...] = _rms(x_ref[...]) * g_ref[...]
        acc_ref[...] = jnp.zeros(acc_ref.shape, F32)

    h = h_ref[...]
    ug = _dot(h, wg_ref[...])
    uu = _dot(h, wu_ref[...])
    ug_ref[...] = ug
    uu_ref[...] = uu
    fg = cbg_ref[...] + cwg_ref[kw - 1:kw, :] * ug
    fu = cbu_ref[...] + cwu_ref[kw - 1:kw, :] * uu
    for j in range(kw - 1):
        fg = fg + cwg_ref[j:j + 1, :] * stg_ref[j]
        fu = fu + cwu_ref[j:j + 1, :] * stu_ref[j]
    acc_ref[...] += _dot(_silu(fg) * fu, wd_ref[...])

    @pl.when(c == pl.num_programs(0) - 1)
    def _():
        o_ref[...] = x_ref[...] + acc_ref[...]


def _ffn_sample(x, g, wup, st_t, cw, cb, wd, tc):
    t, d = x.shape
    dff = wd.shape[0]
    kw = cw.shape[0]
    nff = dff // tc
    kern = functools.partial(_ffn_sample_kernel, kw=kw)
    return pl.pallas_call(
        kern,
        grid=(nff,),
        in_specs=[pl.BlockSpec((t, d), lambda c: (0, 0)),
                  pl.BlockSpec((1, d), lambda c: (0, 0)),
                  pl.BlockSpec((d, tc), lambda c: (0, c)),
                  pl.BlockSpec((d, tc), lambda c: (0, nff + c)),
                  pl.BlockSpec((kw - 1, t, tc), lambda c: (0, 0, c)),
                  pl.BlockSpec((kw - 1, t, tc), lambda c: (0, 0, nff + c)),
                  pl.BlockSpec((kw, tc), lambda c: (0, c)),
                  pl.BlockSpec((kw, tc), lambda c: (0, nff + c)),
                  pl.BlockSpec((1, tc), lambda c: (0, c)),
                  pl.BlockSpec((1, tc), lambda c: (0, nff + c)),
                  pl.BlockSpec((tc, d), lambda c: (c, 0))],
        out_specs=[pl.BlockSpec((t, d), lambda c: (0, 0)),
                   pl.BlockSpec((t, tc), lambda c: (0, c)),
                   pl.BlockSpec((t, tc), lambda c: (0, c))],
        out_shape=[jax.ShapeDtypeStruct((t, d), F32), jax.ShapeDtypeStruct((t, dff), F32),
                   jax.ShapeDtypeStruct((t, dff), F32)],
        scratch_shapes=[pltpu.VMEM((t, d), F32), pltpu.VMEM((t, d), F32)],
        compiler_params=_cp(("arbitrary",)),
        name="ffn_sample",
    )(x, g, wup, wup, st_t, st_t, cw, cw, cb, cb, wd)


def _constants():
    lane = np.arange(LANES)
    r = lane % HEAD_DIM
    rot = HEAD_DIM // 4
    half = rot // 2
    inv_freq = ROPE_THETA ** (-jnp.arange(half, dtype=F32) * (2.0 / rot))
    in_rot = r < rot
    invf = jnp.where(jnp.asarray(in_rot), inv_freq[jnp.asarray(r % half)], 0.0).astype(F32)[None, :]
    sgn = np.where(r < half, -1.0, np.where(in_rot, 1.0, 0.0)).astype(np.float32)[None, :]
    first_head = (lane < HEAD_DIM)[None, :]
    dq = N_HEADS * HEAD_DIM
    bd = (np.arange(dq)[:, None] // HEAD_DIM == np.arange(dq)[None, :] // HEAD_DIM)
    di = SSM_HEADS * SSM_HEAD_DIM
    e64 = np.zeros((LANES, di), np.float32)
    e128 = np.zeros((LANES, SSM_HEADS * LANES), np.float32)
    for h in range(SSM_HEADS):
        e64[SM_DT + h, h * SSM_HEAD_DIM:(h + 1) * SSM_HEAD_DIM] = 1.0
        e128[SM_DT + h, h * LANES:(h + 1) * LANES] = 1.0
    tril = np.tril(np.ones((SSM_CHUNK, SSM_CHUNK), np.float32))
    hpg = SSM_HEADS // SSM_GROUPS
    bmask = (np.arange(SSM_GROUPS * D_STATE)[:, None] // D_STATE
             == np.arange(di)[None, :] // (hpg * SSM_HEAD_DIM)).astype(np.float32)
    return dict(
        invf=invf, sgn=jnp.asarray(sgn),
        invf_s=jnp.where(jnp.asarray(first_head), invf, 0.0),
        sgn_s=jnp.asarray(np.where(first_head, sgn, 0.0).astype(np.float32)),
        bd=jnp.asarray(bd.astype(np.float32), dtype=BF16),
        e64=jnp.asarray(e64, dtype=BF16), e128=jnp.asarray(e128, dtype=BF16),
        tril=jnp.asarray(tril, dtype=BF16), bmask=jnp.asarray(bmask))


def _reorder_w_in(w_in):
    d = w_in.shape[0]
    sizes = dict(glu=1024, q=512, k=128, v=128, qi=512, ki=64, wi=8, z=512, xbc=768, dt=8, mq=512,
                 gates=4096)
    order_in = ["glu", "q", "k", "v", "qi", "ki", "wi", "z", "xbc", "dt", "mq", "gates"]
    parts, off = {}, 0
    for name in order_in:
        parts[name] = w_in[:, off:off + sizes[name]]
        off += sizes[name]
    assert off == w_in.shape[1]
    pad = jnp.zeros((d, LANES - sizes["ki"] - sizes["wi"] - sizes["dt"]), w_in.dtype)
    out = jnp.concatenate([parts[n] for n in ("glu", "q", "qi", "z", "mq", "xbc", "k", "v", "gates",
                                              "ki", "wi", "dt")] + [pad], axis=1)
    assert out.shape[1] == PROJ_COLS
    return out


def _pad_lanes(v, start):
    return jnp.zeros((1, LANES), F32).at[0, start:start + v.shape[0]].set(v)


def _tile(n, cap):
    return min(n, cap)


def _layer_params(l, prm, consts):
    p = dict(
        w_in=_reorder_w_in(prm["w_in"][l]),
        norm_mix_g=prm["norm_mix_g"][l][None, :],
        conv_a_w=prm["conv_a_w"][l], conv_a_b=prm["conv_a_b"][l][None, :],
        ln_a_g=prm["ln_a_g"][l][None, :], ln_a_b=prm["ln_a_b"][l][None, :],
        qg=jnp.tile(prm["q_norm_g"][l], N_HEADS)[None, :],
        kg=jnp.tile(prm["k_norm_g"][l], N_KV_HEADS)[None, :],
        ssm_conv_w=prm["ssm_conv_w"][l], ssm_conv_b=prm["ssm_conv_b"][l][None, :],
        dtb_row=_pad_lanes(prm["dt_bias"][l], SM_DT), alog_row=_pad_lanes(prm["a_log"][l], SM_DT),
        dskip_f=jnp.repeat(prm["d_skip"][l], SSM_HEAD_DIM)[None, :],
        ssm_norm_g=prm["ssm_norm_g"][l][None, :],
        mem_norm_g=prm["mem_norm_g"][l][None, :], w_mem_kv=prm["w_mem_kv"][l],
        mq_norm_g=prm["mq_norm_g"][l][None, :], mk_norm_g=prm["mk_norm_g"][l][None, :],
        w_branch=prm["w_branch"][l], w_out=prm["w_out"][l],
        norm_ffn_g=prm["norm_ffn_g"][l][None, :], w_ffn_up=prm["w_ffn_up"][l],
        ffn_conv_w=prm["ffn_conv_w"][l], ffn_conv_b=prm["ffn_conv_b"][l][None, :],
        w_ffn_down=prm["w_ffn_down"][l])
    return p


def _ssm_state_from_slab(slab):
    nseq = slab.shape[0]
    hpg = SSM_HEADS // SSM_GROUPS
    s = slab.reshape(nseq, SSM_GROUPS, D_STATE, SSM_HEADS, SSM_HEAD_DIM)
    per_head = [s[:, h // hpg, :, h, :] for h in range(SSM_HEADS)]
    return jnp.swapaxes(jnp.stack(per_head, axis=1), 2, 3)


def _prompt_layer(x, p, consts, mem2d, nseq, seq):
    t, d = x.shape
    nm = mem2d.shape[0] // nseq
    proj = _norm_proj(x, p["norm_mix_g"], p["w_in"], _tile(t, 1024), PROJ_COLS // 5)
    br_a, conf_new = _conf_prompt(proj, p["conv_a_w"], p["conv_a_b"], p["ln_a_g"], p["ln_a_b"],
                                  nseq, seq, _tile(seq, 512))
    qn, kn, v, qir, smr, kir = _dsa_prep(proj, p["qg"], p["kg"], consts, _tile(seq, 512), seq, None)
    br_b = _dsa_prompt(qn, qir, smr, kn, v, nseq, seq, _tile(seq, 256), _tile(seq, 512))
    br_c, sconv_new, sslab = _ssd_prompt(proj, p["ssm_conv_w"], p["ssm_conv_b"], p["dtb_row"],
                                         p["alog_row"], p["dskip_f"], p["ssm_norm_g"], consts,
                                         nseq, seq, _tile(seq, 512))
    mk, mv = _mem_kv(mem2d, p["mem_norm_g"], p["w_mem_kv"], p["mk_norm_g"], nm)
    dm = MEM_HEADS * MEM_HEAD_DIM
    br_m = _mem_attend(proj.reshape(nseq, seq, PROJ_COLS), mk.reshape(nseq, nm, dm),
                       mv.reshape(nseq, nm, dm), p["mq_norm_g"], _tile(seq, 512)).reshape(t, dm)
    x = _merge(x, proj, (br_a, br_b, br_c, br_m), p["w_branch"], p["w_out"], _tile(t, 256))
    x, ffn_new = _ffn_prompt(x, p["norm_ffn_g"], p["w_ffn_up"], p["ffn_conv_w"], p["ffn_conv_b"],
                             p["w_ffn_down"], nseq, seq, _tile(seq, 512), 256)
    state = (kn.reshape(nseq, seq, N_KV_HEADS, HEAD_DIM), v.reshape(nseq, seq, N_KV_HEADS, HEAD_DIM),
             kir.reshape(nseq, seq, D_IDX),
             mk.reshape(nseq, nm, MEM_HEADS, MEM_HEAD_DIM), mv.reshape(nseq, nm, MEM_HEADS, MEM_HEAD_DIM),
             conf_new, sconv_new, _ssm_state_from_slab(sslab), ffn_new)
    return x, state


def _sample_layer(x, p, consts, page_table, ck, cv, cki, cmk, cmv, st_conf, st_sconv, st_ssm, st_ffn):
    nb, d = x.shape
    n_pages = page_table.shape[1]
    page = ck.shape[1]
    past = n_pages * page
    dq = N_HEADS * HEAD_DIM
    dk = N_KV_HEADS * HEAD_DIM
    proj = _norm_proj(x, p["norm_mix_g"], p["w_in"], nb, PROJ_COLS // 5)
    br_a, a_new = _conf_sample(proj, jnp.swapaxes(st_conf, 0, 1), p["conv_a_w"], p["conv_a_b"],
                               p["ln_a_g"], p["ln_a_b"])
    qn, kn, v, qir, smr, kir = _dsa_prep(proj, p["qg"], p["kg"], consts, nb, 1, float(past))
    hpk = N_HEADS // N_KV_HEADS
    qh = qn.reshape(nb, N_HEADS, HEAD_DIM)
    grp = (np.arange(N_HEADS)[:, None] // hpk == np.arange(dk)[None, :] // HEAD_DIM)
    q8 = jnp.where(jnp.asarray(grp)[None], jnp.tile(qh, (1, 1, N_KV_HEADS)), 0.0)
    o8 = _dsa_sample(page_table, q8, qir.reshape(nb, N_IDX_HEADS, D_IDX),
                     smr[:, SM_WI:SM_WI + N_IDX_HEADS].reshape(nb, N_IDX_HEADS, 1),
                     kir.reshape(nb, 1, D_IDX), kn.reshape(nb, 1, dk), v.reshape(nb, 1, dk),
                     cki, ck.reshape(ck.shape[0], page, dk), cv.reshape(cv.shape[0], page, dk))
    o8 = o8.reshape(nb, N_HEADS, N_KV_HEADS, HEAD_DIM)
    br_b = jnp.stack([o8[:, h, h // hpk, :] for h in range(N_HEADS)], axis=1).reshape(nb, dq)
    h0 = st_ssm.reshape(nb, SSM_HEADS * SSM_HEAD_DIM, D_STATE)
    br_c, h_new = _ssd_sample(proj, jnp.swapaxes(st_sconv, 0, 1), h0, p["ssm_conv_w"], p["ssm_conv_b"],
                              p["dtb_row"], p["alog_row"], p["dskip_f"], p["ssm_norm_g"], consts)
    nm = cmk.shape[1]
    dm = MEM_HEADS * MEM_HEAD_DIM
    br_m = _mem_attend(proj.reshape(nb, 1, PROJ_COLS), cmk.reshape(nb, nm, dm), cmv.reshape(nb, nm, dm),
                       p["mq_norm_g"], 1).reshape(nb, dm)
    x = _merge(x, proj, (br_a, br_b, br_c, br_m), p["w_branch"], p["w_out"], nb)
    x, ug, uu = _ffn_sample(x, p["norm_ffn_g"], p["w_ffn_up"], jnp.swapaxes(st_ffn, 0, 1),
                            p["ffn_conv_w"], p["ffn_conv_b"], p["w_ffn_down"], 256)
    xbc_raw = proj[:, OFF["xbc"]:OFF["xbc"] + st_sconv.shape[-1]]
    state = (kn.reshape(nb, 1, N_KV_HEADS, HEAD_DIM), v.reshape(nb, 1, N_KV_HEADS, HEAD_DIM),
             kir.reshape(nb, 1, D_IDX),
             jnp.concatenate([st_conf[:, 1:], a_new[:, None]], axis=1),
             jnp.concatenate([st_sconv[:, 1:], xbc_raw[:, None]], axis=1),
             h_new.reshape(st_ssm.shape),
             jnp.concatenate([st_ffn[:, 1:], jnp.concatenate([ug, uu], axis=-1)[:, None]], axis=1))
    return x, state


def kernel(x_prompt, x_sample, cache_k, cache_v, cache_kidx, cache_mem_k, cache_mem_v, state_conformer, state_ssm_conv, state_ssm, state_ffn_conv, page_table, mem_prompt, norm_mix_g, w_in, conv_a_w, conv_a_b, ln_a_g, ln_a_b, q_norm_g, k_norm_g, ssm_conv_w, ssm_conv_b, dt_bias, a_log, d_skip, ssm_norm_g, mem_norm_g, w_mem_kv, mq_norm_g, mk_norm_g, w_branch, w_out, norm_ffn_g, w_ffn_up, ffn_conv_w, ffn_conv_b, w_ffn_down):
    prm = dict(norm_mix_g=norm_mix_g, w_in=w_in, conv_a_w=conv_a_w, conv_a_b=conv_a_b, ln_a_g=ln_a_g,
               ln_a_b=ln_a_b, q_norm_g=q_norm_g, k_norm_g=k_norm_g, ssm_conv_w=ssm_conv_w,
               ssm_conv_b=ssm_conv_b, dt_bias=dt_bias, a_log=a_log, d_skip=d_skip, ssm_norm_g=ssm_norm_g,
               mem_norm_g=mem_norm_g, w_mem_kv=w_mem_kv, mq_norm_g=mq_norm_g, mk_norm_g=mk_norm_g,
               w_branch=w_branch, w_out=w_out, norm_ffn_g=norm_ffn_g, w_ffn_up=w_ffn_up,
               ffn_conv_w=ffn_conv_w, ffn_conv_b=ffn_conv_b, w_ffn_down=w_ffn_down)
    depth = w_in.shape[0]
    nseq, seq, d = x_prompt.shape
    nb, dseq, _ = x_sample.shape
    assert dseq == 1
    consts = _constants()
    xp = x_prompt.reshape(nseq * seq, d)
    xs = x_sample.reshape(nb, d)
    mem2d = mem_prompt.reshape(nseq * mem_prompt.shape[1], d)
    p_states, s_states = [], []
    for l in range(depth):
        p = _layer_params(l, prm, consts)
        xp, st = _prompt_layer(xp, p, consts, mem2d, nseq, seq)
        p_states.append(st)
        xs, st = _sample_layer(xs, p, consts, page_table, cache_k[l], cache_v[l], cache_kidx[l],
                               cache_mem_k[l], cache_mem_v[l], state_conformer[l], state_ssm_conv[l],
                               state_ssm[l], state_ffn_conv[l])
        s_states.append(st)
    stack = lambda states, k: jnp.stack([s[k] for s in states])
    return ((xp.reshape(nseq, seq, d), xs.reshape(nb, 1, d))
            + tuple(stack(p_states, k) for k in range(9))
            + tuple(stack(s_states, k) for k in range(7)))
```

```python
import functools
import math

import numpy as np
import jax
import jax.numpy as jnp
from jax import lax
from jax.experimental import pallas as pl
from jax.experimental.pallas import tpu as pltpu

F32 = jnp.float32
BF16 = jnp.bfloat16
I32 = jnp.int32
MXU_DT = BF16

EPS = 1e-6
ROPE_THETA = 500000.0
LANES = 128
V7X_VMEM_LIMIT = 56 * 1024 * 1024

N_HEADS = 8
HEAD_DIM = 64
N_KV_HEADS = 2
N_IDX_HEADS = 8
D_IDX = 64
TOP_K = 256
SSM_HEADS = 8
SSM_HEAD_DIM = 64
SSM_GROUPS = 2
D_STATE = 64
SSM_CHUNK = 128
MEM_HEADS = 4
MEM_HEAD_DIM = 128
IDX_SCALE = (D_IDX ** -0.5) * (N_IDX_HEADS ** -0.5)
NEG_BIG = -1e30
INT_MIN = -2 ** 31

SM_KI = 0
SM_WI = 64
SM_DT = 72

OFF = dict(glu=0, q=1024, qi=1536, z=2048, mq=2560, xbc=3072, k=3840, v=3968, gates=4096, small=8192)
PROJ_COLS = 8320


def _cp(sem):
    return pltpu.CompilerParams(dimension_semantics=sem, vmem_limit_bytes=V7X_VMEM_LIMIT)


def _nt_dot(a, b):
    return lax.dot_general(a, b, (((1,), (1,)), ((), ())), preferred_element_type=F32)


def _dot(a, b):
    return jnp.dot(a, b, preferred_element_type=F32)


def _split3(x):
    hi = x.astype(BF16)
    r = x - hi.astype(F32)
    mid = r.astype(BF16)
    lo = (r - mid.astype(F32)).astype(BF16)
    return hi, mid, lo


def _dot_sel(x, e):
    hi, mid, lo = _split3(x)
    return _dot(hi, e) + _dot(mid, e) + _dot(lo, e)


def _sel_dot(e, x):
    hi, mid, lo = _split3(x)
    return _dot(e, hi) + _dot(e, mid) + _dot(e, lo)


def _silu(x):
    return x * jax.nn.sigmoid(x)


def _softplus(x):
    return jnp.maximum(x, 0.0) + jnp.log1p(jnp.exp(-jnp.abs(x)))


def _rms(x):
    return x * lax.rsqrt(jnp.mean(x * x, axis=-1, keepdims=True) + EPS)


def _sortable(x):
    b = lax.bitcast_convert_type(x, I32)
    b = jnp.where(x == 0.0, 0, b)
    return jnp.where(b < 0, b ^ 0x7FFFFFFF, b)


def _norm_proj_kernel(x_ref, g_ref, w_ref, o_ref, h_ref):
    @pl.when(pl.program_id(1) == 0)
    def _():
        h_ref[...] = (_rms(x_ref[...]) * g_ref[...]).astype(h_ref.dtype)

    o_ref[...] = _dot(h_ref[...], w_ref[...])


def _norm_proj(x, g, w, tm, tn):
    t, d = x.shape
    n = w.shape[1]
    return pl.pallas_call(
        _norm_proj_kernel,
        grid=(t // tm, n // tn),
        in_specs=[pl.BlockSpec((tm, d), lambda i, j: (i, 0)),
                  pl.BlockSpec((1, d), lambda i, j: (0, 0)),
                  pl.BlockSpec((d, tn), lambda i, j: (0, j))],
        out_specs=pl.BlockSpec((tm, tn), lambda i, j: (i, j)),
        out_shape=jax.ShapeDtypeStruct((t, n), F32),
        scratch_shapes=[pltpu.VMEM((tm, d), MXU_DT)],
        compiler_params=_cp(("parallel", "arbitrary")),
        name="norm_proj",
    )(x, g, w)


def _layernorm_silu(u, g, b):
    xc = u - jnp.mean(u, axis=-1, keepdims=True)
    var = jnp.mean(xc * xc, axis=-1, keepdims=True)
    return _silu(xc * lax.rsqrt(var + EPS) * g + b)


def _conf_prompt_kernel(glu_ref, w_ref, b_ref, lg_ref, lb_ref, o_ref, cnew_ref, abuf, *, tm, cw, dc):
    halo = 32

    @pl.when(pl.program_id(1) == 0)
    def _():
        abuf[0:halo, :] = jnp.zeros((halo, dc), F32)

    glu = glu_ref[...]
    abuf[halo:halo + tm, :] = glu[:, :dc] * jax.nn.sigmoid(glu[:, dc:])
    first = halo - (cw - 1)
    u = b_ref[...] + w_ref[0:1, :] * abuf[first:first + tm, :]
    for j in range(1, cw):
        u = u + w_ref[j:j + 1, :] * abuf[first + j:first + j + tm, :]
    o_ref[...] = _layernorm_silu(u, lg_ref[...], lb_ref[...])
    cnew_ref[0] = abuf[tm + first:tm + halo, :]
    abuf[0:halo, :] = abuf[tm:tm + halo, :]


def _conf_prompt(proj, w, b, lg, lb, nseq, seq, tm):
    cw, dc = w.shape
    nt = seq // tm
    kern = functools.partial(_conf_prompt_kernel, tm=tm, cw=cw, dc=dc)
    return pl.pallas_call(
        kern,
        grid=(nseq, nt),
        in_specs=[pl.BlockSpec((tm, 2 * dc), lambda s, i: (s * nt + i, OFF["glu"] // (2 * dc))),
                  pl.BlockSpec((cw, dc), lambda s, i: (0, 0)),
                  pl.BlockSpec((1, dc), lambda s, i: (0, 0)),
                  pl.BlockSpec((1, dc), lambda s, i: (0, 0)),
                  pl.BlockSpec((1, dc), lambda s, i: (0, 0))],
        out_specs=[pl.BlockSpec((tm, dc), lambda s, i: (s * nt + i, 0)),
                   pl.BlockSpec((1, cw - 1, dc), lambda s, i: (s, 0, 0))],
        out_shape=[jax.ShapeDtypeStruct((nseq * seq, dc), F32),
                   jax.ShapeDtypeStruct((nseq, cw - 1, dc), F32)],
        scratch_shapes=[pltpu.VMEM((tm + 32, dc), F32)],
        compiler_params=_cp(("arbitrary", "arbitrary")),
        name="conformer_prompt",
    )(proj, w, b, lg, lb)


def _conf_sample_kernel(glu_ref, st_ref, w_ref, b_ref, lg_ref, lb_ref, o_ref, a_ref, *, cw, dc):
    glu = glu_ref[...]
    a = glu[:, :dc] * jax.nn.sigmoid(glu[:, dc:])
    u = b_ref[...] + w_ref[cw - 1:cw, :] * a
    for j in range(cw - 1):
        u = u + w_ref[j:j + 1, :] * st_ref[j]
    o_ref[...] = _layernorm_silu(u, lg_ref[...], lb_ref[...])
    a_ref[...] = a


def _conf_sample(proj, st_t, w, b, lg, lb):
    cw, dc = w.shape
    n = proj.shape[0]
    kern = functools.partial(_conf_sample_kernel, cw=cw, dc=dc)
    return pl.pallas_call(
        kern,
        grid=(1,),
        in_specs=[pl.BlockSpec((n, 2 * dc), lambda i: (0, OFF["glu"] // (2 * dc))),
                  pl.BlockSpec((cw - 1, n, dc), lambda i: (0, 0, 0)),
                  pl.BlockSpec((cw, dc), lambda i: (0, 0)),
                  pl.BlockSpec((1, dc), lambda i: (0, 0)),
                  pl.BlockSpec((1, dc), lambda i: (0, 0)),
                  pl.BlockSpec((1, dc), lambda i: (0, 0))],
        out_specs=[pl.BlockSpec((n, dc), lambda i: (0, 0)),
                   pl.BlockSpec((n, dc), lambda i: (0, 0))],
        out_shape=[jax.ShapeDtypeStruct((n, dc), F32), jax.ShapeDtypeStruct((n, dc), F32)],
        compiler_params=_cp(("arbitrary",)),
        name="conformer_sample",
    )(proj, st_t, w, b, lg, lb)


def _rope128(x, c, s):
    lane = lax.broadcasted_iota(I32, x.shape, 1) % HEAD_DIM
    partner = jnp.where(lane < 8, pltpu.roll(x, LANES - 8, 1), pltpu.roll(x, 8, 1))
    return x * c + partner * s


def _dsa_prep_kernel(q_ref, qi_ref, k_ref, v_ref, sm_ref, qg_ref, kg_ref, invf_ref, sgn_ref,
                     invf_s_ref, sgn_s_ref, bd_ref,
                     qn_ref, kn_ref, vo_ref, qir_ref, smr_ref, kir_ref, *, tm, seq, const_pos):
    if const_pos is None:
        base = (pl.program_id(0) * tm) % seq
        pos = (base + lax.broadcasted_iota(I32, (tm, LANES), 0)).astype(F32)
    else:
        pos = jnp.full((tm, LANES), const_pos, F32)
    ang = pos * invf_ref[...]
    c = jnp.cos(ang)
    s = jnp.sin(ang) * sgn_ref[...]
    ang_s = pos * invf_s_ref[...]
    c_s = jnp.cos(ang_s)
    s_s = jnp.sin(ang_s) * sgn_s_ref[...]

    def seg_rms(x, bd):
        x2 = x * x
        hi = x2.astype(BF16)
        lo = (x2 - hi.astype(F32)).astype(BF16)
        ms = (_dot(hi, bd) + _dot(lo, bd)) * (1.0 / HEAD_DIM)
        return x * lax.rsqrt(ms + EPS)

    qn = seg_rms(q_ref[...], bd_ref[...]) * qg_ref[...]
    qi = qi_ref[...]
    for j in range(q_ref.shape[1] // LANES):
        sl = slice(j * LANES, (j + 1) * LANES)
        qn_ref[:, sl] = _rope128(qn[:, sl], c, s)
        qir_ref[:, sl] = _rope128(qi[:, sl], c, s)
    kn = seg_rms(k_ref[...], bd_ref[0:LANES, 0:LANES]) * kg_ref[...]
    kn_ref[...] = _rope128(kn, c, s)
    vo_ref[...] = v_ref[...]
    smr = _rope128(sm_ref[...], c_s, s_s)
    smr_ref[...] = smr
    kir_ref[...] = smr[:, 0:D_IDX]


def _dsa_prep(proj, qg, kg, consts, tm, seq, const_pos):
    t = proj.shape[0]
    dq = N_HEADS * HEAD_DIM
    dk = N_KV_HEADS * HEAD_DIM
    kern = functools.partial(_dsa_prep_kernel, tm=tm, seq=seq, const_pos=const_pos)
    row = lambda w: pl.BlockSpec((1, w), lambda i: (0, 0))
    return pl.pallas_call(
        kern,
        grid=(t // tm,),
        in_specs=[pl.BlockSpec((tm, dq), lambda i: (i, OFF["q"] // dq)),
                  pl.BlockSpec((tm, dq), lambda i: (i, OFF["qi"] // dq)),
                  pl.BlockSpec((tm, dk), lambda i: (i, OFF["k"] // dk)),
                  pl.BlockSpec((tm, dk), lambda i: (i, OFF["v"] // dk)),
                  pl.BlockSpec((tm, LANES), lambda i: (i, OFF["small"] // LANES)),
                  row(dq), row(dk), row(LANES), row(LANES), row(LANES), row(LANES),
                  pl.BlockSpec((dq, dq), lambda i: (0, 0))],
        out_specs=[pl.BlockSpec((tm, dq), lambda i: (i, 0)),
                   pl.BlockSpec((tm, dk), lambda i: (i, 0)),
                   pl.BlockSpec((tm, dk), lambda i: (i, 0)),
                   pl.BlockSpec((tm, dq), lambda i: (i, 0)),
                   pl.BlockSpec((tm, LANES), lambda i: (i, 0)),
                   pl.BlockSpec((tm, D_IDX), lambda i: (i, 0))],
        out_shape=[jax.ShapeDtypeStruct((t, dq), F32), jax.ShapeDtypeStruct((t, dk), F32),
                   jax.ShapeDtypeStruct((t, dk), F32), jax.ShapeDtypeStruct((t, dq), F32),
                   jax.ShapeDtypeStruct((t, LANES), F32), jax.ShapeDtypeStruct((t, D_IDX), F32)],
        compiler_params=_cp(("parallel",)),
        name="dsa_prep",
    )(proj, proj, proj, proj, proj, qg, kg, consts["invf"], consts["sgn"], consts["invf_s"],
      consts["sgn_s"], consts["bd"])


def _dsa_prep_t_kernel(q_ref, qi_ref, k_ref, v_ref, sm_ref, qg_ref, kg_ref, invf_ref, sgn_ref,
                       invf_s_ref, sgn_s_ref, bd_ref,
                       kn_ref, vo_ref, kir_ref, knb_ref, smb_ref, qt_ref, qit_ref, vt_ref, smt_ref, *, tm):
    pos = (pl.program_id(1) * tm + lax.broadcasted_iota(I32, (tm, LANES), 0)).astype(F32)
    ang = pos * invf_ref[...]
    c = jnp.cos(ang)
    s = jnp.sin(ang) * sgn_ref[...]
    ang_s = pos * invf_s_ref[...]
    c_s = jnp.cos(ang_s)
    s_s = jnp.sin(ang_s) * sgn_s_ref[...]

    def seg_rms(x, bd):
        x2 = x * x
        hi = x2.astype(BF16)
        lo = (x2 - hi.astype(F32)).astype(BF16)
        ms = (_dot(hi, bd) + _dot(lo, bd)) * (1.0 / HEAD_DIM)
        return x * lax.rsqrt(ms + EPS)

    lo_half = lax.broadcasted_iota(I32, (tm, LANES), 1) < HEAD_DIM
    qn = seg_rms(q_ref[...], bd_ref[...]) * qg_ref[...]
    qi = qi_ref[...]
    for j in range(N_HEADS // 2):
        sl = slice(j * LANES, (j + 1) * LANES)
        q2 = _rope128(qn[:, sl], c, s) * (HEAD_DIM ** -0.5)
        q2r = pltpu.roll(q2, HEAD_DIM, 1)
        if (2 * j) // (N_HEADS // N_KV_HEADS) == 0:
            qa, qb = jnp.where(lo_half, q2, 0.0), jnp.where(lo_half, q2r, 0.0)
        else:
            qa, qb = jnp.where(lo_half, 0.0, q2r), jnp.where(lo_half, 0.0, q2)
        qt_ref[0, 2 * j] = qa.T.astype(MXU_DT)
        qt_ref[0, 2 * j + 1] = qb.T.astype(MXU_DT)
        qi2 = _rope128(qi[:, sl], c, s)
        qit_ref[0, 2 * j] = jnp.where(lo_half, qi2, 0.0).T.astype(MXU_DT)
        qit_ref[0, 2 * j + 1] = jnp.where(lo_half, pltpu.roll(qi2, HEAD_DIM, 1), 0.0).T.astype(MXU_DT)
    kn = _rope128(seg_rms(k_ref[...], bd_ref[0:LANES, 0:LANES]) * kg_ref[...], c, s)
    kn_ref[...] = kn
    knb_ref[...] = kn.astype(MXU_DT)
    v = v_ref[...]
    vo_ref[...] = v
    vt_ref[0, 0] = v.T.astype(MXU_DT)
    smr = _rope128(sm_ref[...], c_s, s_s)
    smb_ref[...] = smr.astype(MXU_DT)
    kir_ref[...] = smr[:, 0:D_IDX]
    smt_ref[0] = smr.T


def _dsa_prep_t(proj, qg, kg, consts, nseq, seq, tm):
    t = proj.shape[0]
    dq = N_HEADS * HEAD_DIM
    dk = N_KV_HEADS * HEAD_DIM
    nt = seq // tm
    kern = functools.partial(_dsa_prep_t_kernel, tm=tm)
    row = lambda w: pl.BlockSpec((1, w), lambda s, i: (0, 0))
    tok = lambda w, col: pl.BlockSpec((tm, w), lambda s, i: (s * nt + i, col))
    return pl.pallas_call(
        kern,
        grid=(nseq, nt),
        in_specs=[tok(dq, OFF["q"] // dq), tok(dq, OFF["qi"] // dq), tok(dk, OFF["k"] // dk),
                  tok(dk, OFF["v"] // dk), tok(LANES, OFF["small"] // LANES),
                  row(dq), row(dk), row(LANES), row(LANES), row(LANES), row(LANES),
                  pl.BlockSpec((dq, dq), lambda s, i: (0, 0))],
        out_specs=[tok(dk, 0), tok(dk, 0), tok(D_IDX, 0), tok(dk, 0), tok(LANES, 0),
                   pl.BlockSpec((1, N_HEADS, LANES, tm), lambda s, i: (s, 0, 0, i)),
                   pl.BlockSpec((1, N_IDX_HEADS, LANES, tm), lambda s, i: (s, 0, 0, i)),
                   pl.BlockSpec((1, 1, dk, tm), lambda s, i: (s, i, 0, 0)),
                   pl.BlockSpec((1, LANES, tm), lambda s, i: (s, 0, i))],
        out_shape=[jax.ShapeDtypeStruct((t, dk), F32), jax.ShapeDtypeStruct((t, dk), F32),
                   jax.ShapeDtypeStruct((t, D_IDX), F32),
                   jax.ShapeDtypeStruct((t, dk), MXU_DT), jax.ShapeDtypeStruct((t, LANES), MXU_DT),
                   jax.ShapeDtypeStruct((nseq, N_HEADS, LANES, seq), MXU_DT),
                   jax.ShapeDtypeStruct((nseq, N_IDX_HEADS, LANES, seq), MXU_DT),
                   jax.ShapeDtypeStruct((nseq, nt, dk, tm), MXU_DT),
                   jax.ShapeDtypeStruct((nseq, LANES, seq), F32)],
        compiler_params=_cp(("parallel", "parallel")),
        name="dsa_prep_prompt",
    )(proj, proj, proj, proj, proj, qg, kg, consts["invf"], consts["sgn"], consts["invf_s"],
      consts["sgn_s"], consts["bd"])


def _dsa_prompt_t_kernel(qt_ref, qit_ref, smt_ref, smk_ref, k_ref, vt_ref, o_ref,
                         keys_ref, acc_ref, y_ref, *, tq, ck, ksel, nbits):
    i = pl.program_id(1)
    nc = (i * tq + tq + ck - 1) // ck
    qidx = i * tq + lax.broadcasted_iota(I32, (ck, tq), 1)
    krow = lax.broadcasted_iota(I32, (ck, tq), 0)

    def p1(c, carry):
        kc = smk_ref[pl.ds(pl.multiple_of(c * ck, ck), ck), :]
        acc = jnp.zeros((ck, tq), F32)
        for h in range(N_IDX_HEADS):
            s = _dot(kc, qit_ref[0, h])
            acc = acc + jnp.maximum(s, 0.0) * smt_ref[0, SM_WI + h:SM_WI + h + 1, :]
        sc = jnp.where(c * ck + krow <= qidx, acc * IDX_SCALE, -jnp.inf)
        keys_ref[c] = _sortable(sc)
        return carry

    lax.fori_loop(0, nc, p1, 0)

    def count(pred):
        def body(c, part):
            hit = pred(keys_ref[c], c * ck + krow)
            return part + jnp.sum(hit.reshape(ck // 8, 8, tq), axis=0)

        part = lax.fori_loop(0, nc, body, jnp.zeros((8, tq), I32))
        return jnp.sum(part, axis=0, keepdims=True)

    def bit_step(t, thr):
        cand = thr + lax.shift_left(jnp.int32(1), 31 - t)
        cnt = count(lambda kv, kidx: jnp.where(kv >= cand, 1, 0))
        return jnp.where(cnt >= ksel, cand, thr)

    thr = lax.fori_loop(0, 32, bit_step, jnp.full((1, tq), INT_MIN, I32))

    n_gt = count(lambda kv, kidx: jnp.where(kv > thr, 1, 0))
    n_eq = count(lambda kv, kidx: jnp.where(kv == thr, 1, 0))
    need = ksel - n_gt
    y_ref[...] = jnp.full((1, tq), 2 ** 30, I32)
    excess = jnp.max(jnp.where(n_eq > need, 1, 0))

    @pl.when(excess > 0)
    def _():
        def y_step(t, y):
            cand = y + lax.shift_left(jnp.int32(1), nbits - 1 - t)
            g = count(lambda kv, kidx: jnp.where(kv == thr, jnp.where(kidx < cand, 1, 0), 0))
            return jnp.where(g < need, cand, y)

        y_ref[...] = lax.fori_loop(0, nbits, y_step, jnp.zeros((1, tq), I32))

    y_v = y_ref[...]

    def pa(c, macc):
        kv = keys_ref[c]
        kidx = c * ck + krow
        sel = jnp.where(kv > thr, 1, jnp.where(kv == thr, jnp.where(kidx <= y_v, 1, 0), 0))
        sel = jnp.where(kidx <= qidx, sel, 0)
        bias = jnp.where(sel > 0, 0.0, NEG_BIG)
        keys_ref[c] = lax.bitcast_convert_type(bias, I32)
        kk = k_ref[pl.ds(pl.multiple_of(c * ck, ck), ck), :]
        out = []
        for h in range(N_HEADS):
            s = _dot(kk, qt_ref[0, h]) + bias
            out.append(jnp.maximum(macc[h], jnp.max(s.reshape(ck // 8, 8, tq), axis=0)))
        return tuple(out)

    m8 = lax.fori_loop(0, nc, pa, tuple(jnp.full((8, tq), NEG_BIG, F32) for _ in range(N_HEADS)))
    mrow = [jnp.max(x, axis=0, keepdims=True) for x in m8]

    acc_ref[...] = jnp.zeros(acc_ref.shape, F32)

    def pb(c, lacc):
        bias = lax.bitcast_convert_type(keys_ref[c], F32)
        kk = k_ref[pl.ds(pl.multiple_of(c * ck, ck), ck), :]
        vt = vt_ref[0, c]
        out, ps = [], []
        for h in range(N_HEADS):
            p = jnp.exp(_dot(kk, qt_ref[0, h]) + bias - mrow[h])
            out.append(lacc[h] + jnp.sum(p.reshape(ck // 8, 8, tq), axis=0))
            ps.append(p.astype(MXU_DT))
        for h in range(N_HEADS):
            acc_ref[h] += _dot(vt, ps[h])
        return tuple(out)

    l8 = lax.fori_loop(0, nc, pb, tuple(jnp.zeros((8, tq), F32) for _ in range(N_HEADS)))
    lrow = [jnp.sum(x, axis=0, keepdims=True) for x in l8]

    lo_half = lax.broadcasted_iota(I32, (tq, LANES), 1) < HEAD_DIM
    for j in range(N_HEADS // 2):
        ea = (acc_ref[2 * j] / lrow[2 * j]).T
        eb = (acc_ref[2 * j + 1] / lrow[2 * j + 1]).T
        if (2 * j) // (N_HEADS // N_KV_HEADS) == 0:
            out2 = jnp.where(lo_half, ea, pltpu.roll(eb, HEAD_DIM, 1))
        else:
            out2 = jnp.where(lo_half, pltpu.roll(ea, HEAD_DIM, 1), eb)
        o_ref[:, j * LANES:(j + 1) * LANES] = out2


def _dsa_prompt_t(qt, qit, smt, smr, kn, vt, nseq, seq, tq, ck):
    dq = N_HEADS * HEAD_DIM
    dk = N_KV_HEADS * HEAD_DIM
    nq = seq // tq
    nck = seq // ck
    ksel = min(TOP_K, seq // 4)
    assert ck >= ksel and ck % LANES == 0 and seq % ck == 0 and seq % tq == 0 and vt.shape[3] == ck
    nbits = max(1, int(math.ceil(math.log2(seq))))
    kern = functools.partial(_dsa_prompt_t_kernel, tq=tq, ck=ck, ksel=ksel, nbits=nbits)
    return pl.pallas_call(
        kern,
        grid=(nseq, nq),
        in_specs=[pl.BlockSpec((1, N_HEADS, LANES, tq), lambda s, i: (s, 0, 0, i)),
                  pl.BlockSpec((1, N_IDX_HEADS, LANES, tq), lambda s, i: (s, 0, 0, i)),
                  pl.BlockSpec((1, LANES, tq), lambda s, i: (s, 0, i)),
                  pl.BlockSpec((seq, LANES), lambda s, i: (s, 0)),
                  pl.BlockSpec((seq, dk), lambda s, i: (s, 0)),
                  pl.BlockSpec((1, nck, dk, ck), lambda s, i: (s, 0, 0, 0))],
        out_specs=pl.BlockSpec((tq, dq), lambda s, i: (s * nq + i, 0)),
        out_shape=jax.ShapeDtypeStruct((nseq * seq, dq), F32),
        scratch_shapes=[pltpu.VMEM((nck, ck, tq), I32),
                        pltpu.VMEM((N_HEADS, dk, tq), F32),
                        pltpu.VMEM((1, tq), I32)],
        compiler_params=_cp(("arbitrary", "arbitrary")),
        name="dsa_prompt",
    )(qt, qit, smt, smr, kn, vt)


def _dsa_prompt_kernel(q_ref, qi_ref, smq_ref, smk_ref, k_ref, v_ref, o_ref,
                       keys_ref, qs_ref, qis_ref, m_ref, l_ref, acc_ref, thr_ref, y_ref,
                       *, tq, ck, ksel, nbits):
    i = pl.program_id(1)
    nc = (i * tq + tq + ck - 1) // ck
    ng = ck // LANES
    lane = lax.broadcasted_iota(I32, (tq, LANES), 1)
    lo = lane < HEAD_DIM
    rowg = i * tq + lax.broadcasted_iota(I32, (tq, LANES), 0)
    scale = HEAD_DIM ** -0.5

    for j in range(N_HEADS // 2):
        sl = slice(j * LANES, (j + 1) * LANES)
        qi2 = qi_ref[:, sl]
        qis_ref[2 * j] = jnp.where(lo, qi2, 0.0)
        qis_ref[2 * j + 1] = jnp.where(lo, pltpu.roll(qi2, HEAD_DIM, 1), 0.0)
        q2 = q_ref[:, sl] * scale
        q2r = pltpu.roll(q2, HEAD_DIM, 1)
        if (2 * j) // (N_HEADS // N_KV_HEADS) == 0:
            qs_ref[2 * j] = jnp.where(lo, q2, 0.0)
            qs_ref[2 * j + 1] = jnp.where(lo, q2r, 0.0)
        else:
            qs_ref[2 * j] = jnp.where(lo, 0.0, q2r)
            qs_ref[2 * j + 1] = jnp.where(lo, 0.0, q2)

    def p1(c, carry):
        kc = smk_ref[pl.ds(pl.multiple_of(c * ck, ck), ck), :]
        acc = jnp.zeros((tq, ck), F32)
        for h in range(N_IDX_HEADS):
            s = _nt_dot(qis_ref[h], kc)
            acc = acc + jnp.maximum(s, 0.0) * smq_ref[:, SM_WI + h:SM_WI + h + 1]
        sc = acc * IDX_SCALE
        for g in range(ng):
            colg = c * ck + g * LANES + lane
            scg = jnp.where(colg <= rowg, sc[:, g * LANES:(g + 1) * LANES], -jnp.inf)
            keys_ref[c, :, g * LANES:(g + 1) * LANES] = _sortable(scg)
        return carry

    lax.fori_loop(0, nc, p1, 0)

    def count(pred):
        def body(c, part):
            for g in range(ng):
                colg = c * ck + g * LANES + lane
                part = part + pred(keys_ref[c, :, g * LANES:(g + 1) * LANES], colg)
            return part

        part = lax.fori_loop(0, nc, body, jnp.zeros((tq, LANES), I32))
        return jnp.sum(part, axis=1, keepdims=True)

    def bit_step(t, thr):
        cand = thr + lax.shift_left(jnp.int32(1), 31 - t)
        cnt = count(lambda kv, colg: jnp.where(kv >= cand, 1, 0))
        return jnp.where(cnt >= ksel, cand, thr)

    thr = lax.fori_loop(0, 32, bit_step, jnp.full((tq, LANES), INT_MIN, I32))
    thr_ref[...] = thr

    n_gt = count(lambda kv, colg: jnp.where(kv > thr, 1, 0))
    n_eq = count(lambda kv, colg: jnp.where(kv == thr, 1, 0))
    need = ksel - n_gt
    y_ref[...] = jnp.full((tq, LANES), 2 ** 30, I32)
    excess = jnp.max(jnp.where(n_eq > need, 1, 0))

    @pl.when(excess > 0)
    def _():
        def y_step(t, y):
            cand = y + lax.shift_left(jnp.int32(1), nbits - 1 - t)
            g = count(lambda kv, colg: jnp.where(kv == thr, jnp.where(colg < cand, 1, 0), 0))
            return jnp.where(g < need, cand, y)

        y_ref[...] = lax.fori_loop(0, nbits, y_step, jnp.zeros((tq, LANES), I32))

    m_ref[...] = jnp.full(m_ref.shape, NEG_BIG, F32)
    l_ref[...] = jnp.zeros(l_ref.shape, F32)
    acc_ref[...] = jnp.zeros(acc_ref.shape, F32)

    def p3(c, carry):
        thr_v = thr_ref[...]
        y_v = y_ref[...]
        biases = []
        for g in range(ng):
            colg = c * ck + g * LANES + lane
            kv = keys_ref[c, :, g * LANES:(g + 1) * LANES]
            sel = jnp.where(kv > thr_v, 1, jnp.where(kv == thr_v, jnp.where(colg <= y_v, 1, 0), 0))
            sel = jnp.where(colg <= rowg, sel, 0)
            biases.append(jnp.where(sel > 0, 0.0, NEG_BIG))
        bias = jnp.concatenate(biases, axis=1)
        off = pl.multiple_of(c * ck, ck)
        kk = k_ref[pl.ds(off, ck), :]
        vv = v_ref[pl.ds(off, ck), :]
        for h in range(N_HEADS):
            s = _nt_dot(qs_ref[h], kk) + bias
            m_old = m_ref[h]
            m_new = jnp.maximum(m_old, jnp.max(s, axis=1, keepdims=True))
            alpha = jnp.exp(m_old - m_new)
            p = jnp.exp(s - m_new)
            l_ref[h] = alpha * l_ref[h] + jnp.sum(p, axis=1, keepdims=True)
            acc_ref[h] = alpha * acc_ref[h] + _dot(p, vv)
            m_ref[h] = m_new
        return carry

    lax.fori_loop(0, nc, p3, 0)

    for j in range(N_HEADS // 2):
        ea = acc_ref[2 * j] / l_ref[2 * j]
        eb = acc_ref[2 * j + 1] / l_ref[2 * j + 1]
        if (2 * j) // (N_HEADS // N_KV_HEADS) == 0:
            out2 = jnp.where(lo, ea, pltpu.roll(eb, HEAD_DIM, 1))
        else:
            out2 = jnp.where(lo, pltpu.roll(ea, HEAD_DIM, 1), eb)
        o_ref[:, j * LANES:(j + 1) * LANES] = out2


def _dsa_prompt(qn, qir, smr, kn, v, nseq, seq, tq, ck):
    dq = N_HEADS * HEAD_DIM
    dk = N_KV_HEADS * HEAD_DIM
    nq = seq // tq
    ksel = min(TOP_K, seq // 4)
    assert ck >= ksel and ck % LANES == 0 and seq % ck == 0 and seq % tq == 0
    nbits = max(1, int(math.ceil(math.log2(seq))))
    kern = functools.partial(_dsa_prompt_kernel, tq=tq, ck=ck, ksel=ksel, nbits=nbits)
    return pl.pallas_call(
        kern,
        grid=(nseq, nq),
        in_specs=[pl.BlockSpec((tq, dq), lambda s, i: (s * nq + i, 0)),
                  pl.BlockSpec((tq, dq), lambda s, i: (s * nq + i, 0)),
                  pl.BlockSpec((tq, LANES), lambda s, i: (s * nq + i, 0)),
                  pl.BlockSpec((seq, LANES), lambda s, i: (s, 0)),
                  pl.BlockSpec((seq, dk), lambda s, i: (s, 0)),
                  pl.BlockSpec((seq, dk), lambda s, i: (s, 0))],
        out_specs=pl.BlockSpec((tq, dq), lambda s, i: (s * nq + i, 0)),
        out_shape=jax.ShapeDtypeStruct((nseq * seq, dq), F32),
        scratch_shapes=[pltpu.VMEM((seq // ck, tq, ck), I32),
                        pltpu.VMEM((N_HEADS, tq, LANES), F32),
                        pltpu.VMEM((N_IDX_HEADS, tq, LANES), F32),
                        pltpu.VMEM((N_HEADS, tq, 1), F32),
                        pltpu.VMEM((N_HEADS, tq, 1), F32),
                        pltpu.VMEM((N_HEADS, tq, LANES), F32),
                        pltpu.VMEM((tq, LANES), I32),
                        pltpu.VMEM((tq, LANES), I32)],
        compiler_params=_cp(("arbitrary", "arbitrary")),
        name="dsa_prompt",
    )(qn, qir, smr, smr, kn, v)


def _dsa_sample_kernel(pt_ref, q8_ref, qi8_ref, wi_ref, kin_ref, kn_ref, vn_ref,
                       cki_hbm, ck_hbm, cv_hbm, o_ref, kib, kb, vb, sem,
                       *, n_pages, page, ksel, nbits):
    b = pl.program_id(0)
    past = n_pages * page

    def copies(p):
        pg = pt_ref[b, p]
        rows = pl.ds(pl.multiple_of(p * page, page), page)
        return (pltpu.make_async_copy(cki_hbm.at[pg], kib.at[rows, :], sem.at[0]),
                pltpu.make_async_copy(ck_hbm.at[pg], kb.at[rows, :], sem.at[1]),
                pltpu.make_async_copy(cv_hbm.at[pg], vb.at[rows, :], sem.at[2]))

    def start(p, carry):
        for cp in copies(p):
            cp.start()
        return carry

    def wait(p, carry):
        for cp in copies(p):
            cp.wait()
        return carry

    lax.fori_loop(0, n_pages, start, 0)
    lax.fori_loop(0, n_pages, wait, 0)

    wcol = wi_ref[0]
    qi8 = qi8_ref[0]
    s = _nt_dot(qi8, kib[...])
    sc = jnp.sum(jnp.maximum(s, 0.0) * wcol, axis=0, keepdims=True) * IDX_SCALE
    s_new = jnp.sum(qi8 * kin_ref[0], axis=1, keepdims=True)
    sc_new = jnp.sum(jnp.maximum(s_new, 0.0) * wcol, axis=0, keepdims=True) * IDX_SCALE
    keys = _sortable(sc)
    key_new = _sortable(sc_new)
    col = lax.broadcasted_iota(I32, (1, past), 1)

    def cnt(main, new):
        return jnp.sum(main, axis=1, keepdims=True) + new

    def bit_step(t, thr):
        cand = thr + lax.shift_left(jnp.int32(1), 31 - t)
        c = cnt(jnp.where(keys >= cand, 1, 0), jnp.where(key_new >= cand, 1, 0))
        return jnp.where(c >= ksel, cand, thr)

    thr = lax.fori_loop(0, 32, bit_step, jnp.full((1, 1), INT_MIN, I32))
    need = ksel - cnt(jnp.where(keys > thr, 1, 0), jnp.where(key_new > thr, 1, 0))

    def y_step(t, y):
        cand = y + lax.shift_left(jnp.int32(1), nbits - 1 - t)
        g = cnt(jnp.where(keys == thr, jnp.where(col < cand, 1, 0), 0),
                jnp.where(key_new == thr, jnp.where(past < cand, 1, 0), 0))
        return jnp.where(g < need, cand, y)

    y = lax.fori_loop(0, nbits, y_step, jnp.zeros((1, 1), I32))
    sel = jnp.where(keys > thr, 1, jnp.where(keys == thr, jnp.where(col <= y, 1, 0), 0))
    sel_new = jnp.where(key_new > thr, 1, jnp.where(key_new == thr, jnp.where(past <= y, 1, 0), 0))
    bias = jnp.where(sel > 0, 0.0, NEG_BIG)
    bias_new = jnp.where(sel_new > 0, 0.0, NEG_BIG)

    q8 = q8_ref[0] * (HEAD_DIM ** -0.5)
    sa = _nt_dot(q8, kb[...]) + bias
    sa_new = jnp.sum(q8 * kn_ref[0], axis=1, keepdims=True) + bias_new
    m = jnp.maximum(jnp.max(sa, axis=1, keepdims=True), sa_new)
    p = jnp.exp(sa - m)
    p_new = jnp.exp(sa_new - m)
    l = jnp.sum(p, axis=1, keepdims=True) + p_new
    o_ref[0] = (_dot(p, vb[...]) + p_new * vn_ref[0]) / l


def _dsa_sample(page_table, q8, qi8, wi, ki_new, k_new, v_new, cki, ck, cv):
    nb, n_pages = page_table.shape
    page = cki.shape[1]
    past = n_pages * page
    dk = N_KV_HEADS * HEAD_DIM
    ksel = min(TOP_K, (past + 1) // 4)
    nbits = int(math.floor(math.log2(past))) + 1
    kern = functools.partial(_dsa_sample_kernel, n_pages=n_pages, page=page, ksel=ksel, nbits=nbits)
    grid_spec = pltpu.PrefetchScalarGridSpec(
        num_scalar_prefetch=1,
        grid=(nb,),
        in_specs=[pl.BlockSpec((1, N_HEADS, dk), lambda b, pt: (b, 0, 0)),
                  pl.BlockSpec((1, N_IDX_HEADS, D_IDX), lambda b, pt: (b, 0, 0)),
                  pl.BlockSpec((1, N_IDX_HEADS, 1), lambda b, pt: (b, 0, 0)),
                  pl.BlockSpec((1, 1, D_IDX), lambda b, pt: (b, 0, 0)),
                  pl.BlockSpec((1, 1, dk), lambda b, pt: (b, 0, 0)),
                  pl.BlockSpec((1, 1, dk), lambda b, pt: (b, 0, 0)),
                  pl.BlockSpec(memory_space=pl.ANY),
                  pl.BlockSpec(memory_space=pl.ANY),
                  pl.BlockSpec(memory_space=pl.ANY)],
        out_specs=pl.BlockSpec((1, N_HEADS, dk), lambda b, pt: (b, 0, 0)),
        scratch_shapes=[pltpu.VMEM((past, D_IDX), F32),
                        pltpu.VMEM((past, dk), F32),
                        pltpu.VMEM((past, dk), F32),
                        pltpu.SemaphoreType.DMA((3,))],
    )
    return pl.pallas_call(
        kern,
        grid_spec=grid_spec,
        out_shape=jax.ShapeDtypeStruct((nb, N_HEADS, dk), F32),
        compiler_params=_cp(("arbitrary",)),
        name="dsa_sample",
    )(page_table, q8, qi8, wi, ki_new, k_new, v_new, cki, ck, cv)


def _ssd_prompt_kernel(xbc_ref, z_ref, sm_ref, cw_ref, cb_ref, dtb_ref, alog_ref, dskip_ref, ng_ref,
                       tril_ref, e64_ref, e128_ref, bmask_ref,
                       o_ref, cnew_ref, sst_ref, xbuf, st_ref, *, ts, kw, dxbc):
    halo = 8
    di = SSM_HEADS * SSM_HEAD_DIM
    dbc = SSM_GROUPS * D_STATE

    @pl.when(pl.program_id(1) == 0)
    def _():
        xbuf[0:halo, :] = jnp.zeros((halo, dxbc), F32)
        st_ref[...] = jnp.zeros(st_ref.shape, F32)

    xbuf[halo:halo + ts, :] = xbc_ref[...]
    first = halo - (kw - 1)
    conv = cb_ref[...] + cw_ref[0:1, :] * xbuf[first:first + ts, :]
    for j in range(1, kw):
        conv = conv + cw_ref[j:j + 1, :] * xbuf[first + j:first + j + ts, :]
    xc = _silu(conv)
    cnew_ref[0] = xbuf[ts + first:ts + halo, :]
    xbuf[0:halo, :] = xbuf[ts:ts + halo, :]

    lane = lax.broadcasted_iota(I32, (SSM_CHUNK, LANES), 1)
    head_lane = (lane[0:1, :] >= SM_DT) & (lane[0:1, :] < SM_DT + SSM_HEADS)
    a_row = jnp.where(head_lane, -jnp.exp(alog_ref[...]), 0.0)
    tri = lax.broadcasted_iota(I32, (SSM_CHUNK, SSM_CHUNK), 0) >= lax.broadcasted_iota(
        I32, (SSM_CHUNK, SSM_CHUNK), 1)
    glo = lane < D_STATE

    for k in range(ts // SSM_CHUNK):
        rows = slice(k * SSM_CHUNK, (k + 1) * SSM_CHUNK)
        dtf = _softplus(sm_ref[rows, :] + dtb_ref[...])
        adt = dtf * a_row
        a_cs = _sel_dot(tril_ref[...], adt)
        a_cs_t = a_cs.T
        acs_b = _dot_sel(a_cs, e128_ref[...])
        acs_f = _dot_sel(a_cs, e64_ref[...])
        dt_f = _dot_sel(dtf, e64_ref[...])
        alast_f = acs_f[SSM_CHUNK - 1:SSM_CHUNK, :]
        xs = xc[rows, 0:di]
        bm = xc[rows, di:di + dbc]
        cm = xc[rows, di + dbc:di + 2 * dbc]
        xdt = xs * dt_f
        xd = xdt * jnp.exp(alast_f - acs_f)
        bt = bm.T
        cb = (_dot(jnp.where(glo, cm, 0.0), bt), _dot(jnp.where(glo, 0.0, cm), bt))
        pairs = []
        for j in range(SSM_HEADS // 2):
            x2 = xdt[:, j * LANES:(j + 1) * LANES]
            acc = None
            for hh in range(2):
                h = 2 * j + hh
                seg = acs_b[:, h * LANES:(h + 1) * LANES] - a_cs_t[SM_DT + h:SM_DT + h + 1, :]
                lm = jnp.exp(jnp.where(tri, seg, -jnp.inf))
                sc = cb[h // (SSM_HEADS // SSM_GROUPS)] * lm
                xm = jnp.where(glo, x2, 0.0) if hh == 0 else jnp.where(glo, 0.0, x2)
                part = _dot(sc, xm)
                acc = part if acc is None else acc + part
            pairs.append(acc)
        y = jnp.concatenate(pairs, axis=1)
        y = y + _dot(cm, st_ref[...]) * jnp.exp(acs_f) + dskip_ref[...] * xs
        st_ref[...] = st_ref[...] * jnp.exp(alast_f) + bmask_ref[...] * _dot(bt, xd)
        yg = y * _silu(z_ref[rows, :])
        o_ref[rows, :] = _rms(yg) * ng_ref[...]
    sst_ref[0] = st_ref[...]


def _ssd_prompt(proj, cw, cb, dtb_row, alog_row, dskip_f, ng, consts, nseq, seq, ts):
    kw, dxbc = cw.shape
    di = SSM_HEADS * SSM_HEAD_DIM
    nt = seq // ts
    kern = functools.partial(_ssd_prompt_kernel, ts=ts, kw=kw, dxbc=dxbc)
    full = lambda a: pl.BlockSpec(a.shape, lambda s, i: (0,) * a.ndim)
    cs = (consts["tril"], consts["e64"], consts["e128"], consts["bmask"])
    return pl.pallas_call(
        kern,
        grid=(nseq, nt),
        in_specs=[pl.BlockSpec((ts, dxbc), lambda s, i: (s * nt + i, OFF["xbc"] // dxbc)),
                  pl.BlockSpec((ts, di), lambda s, i: (s * nt + i, OFF["z"] // di)),
                  pl.BlockSpec((ts, LANES), lambda s, i: (s * nt + i, OFF["small"] // LANES)),
                  full(cw), full(cb), full(dtb_row), full(alog_row), full(dskip_f), full(ng)]
                 + [full(c) for c in cs],
        out_specs=[pl.BlockSpec((ts, di), lambda s, i: (s * nt + i, 0)),
                   pl.BlockSpec((1, kw - 1, dxbc), lambda s, i: (s, 0, 0)),
                   pl.BlockSpec((1, SSM_GROUPS * D_STATE, di), lambda s, i: (s, 0, 0))],
        out_shape=[jax.ShapeDtypeStruct((nseq * seq, di), F32),
                   jax.ShapeDtypeStruct((nseq, kw - 1, dxbc), F32),
                   jax.ShapeDtypeStruct((nseq, SSM_GROUPS * D_STATE, di), F32)],
        scratch_shapes=[pltpu.VMEM((ts + 8, dxbc), F32),
                        pltpu.VMEM((SSM_GROUPS * D_STATE, di), F32)],
        compiler_params=_cp(("arbitrary", "arbitrary")),
        name="ssd_prompt",
    )(proj, proj, proj, cw, cb, dtb_row, alog_row, dskip_f, ng, *cs)


def _ssd_sample_kernel(xbc_ref, z_ref, sm_ref, st_ref, h0_ref, cw_ref, cb_ref, dtb_ref, alog_ref,
                       dskip_ref, ng_ref, e64_ref, o_ref, hn_ref, y_ref, *, nb, kw):
    di = SSM_HEADS * SSM_HEAD_DIM
    dbc = SSM_GROUPS * D_STATE
    conv = cb_ref[...] + cw_ref[kw - 1:kw, :] * xbc_ref[...]
    for j in range(kw - 1):
        conv = conv + cw_ref[j:j + 1, :] * st_ref[j]
    xc = _silu(conv)
    lane = lax.broadcasted_iota(I32, (1, LANES), 1)
    head_lane = (lane >= SM_DT) & (lane < SM_DT + SSM_HEADS)
    a_row = jnp.where(head_lane, -jnp.exp(alog_ref[...]), 0.0)
    dtf = _softplus(sm_ref[...] + dtb_ref[...])
    dec = jnp.exp(dtf * a_row)
    dt_f = _dot_sel(dtf, e64_ref[...])
    dec_f = _dot_sel(dec, e64_ref[...])
    xs = xc[:, 0:di]
    bm = xc[:, di:di + dbc]
    cm = xc[:, di + dbc:di + 2 * dbc]
    pad = jnp.zeros((LANES - nb, di), F32)
    xdt_t = jnp.concatenate([xs * dt_f, pad], axis=0).T
    dec_t = jnp.concatenate([dec_f, pad], axis=0).T
    bm_r = pltpu.roll(bm, D_STATE, 1)
    cm_r = pltpu.roll(cm, D_STATE, 1)
    rowi = lax.broadcasted_iota(I32, (di, D_STATE), 0)
    g0 = rowi < (SSM_HEADS // SSM_GROUPS) * SSM_HEAD_DIM
    lane_y = lax.broadcasted_iota(I32, (1, di), 1) < (SSM_HEADS // SSM_GROUPS) * SSM_HEAD_DIM
    row8 = lax.broadcasted_iota(I32, (8, D_STATE), 0)
    for b in range(nb):
        bsel = jnp.where(g0, bm[b:b + 1, 0:D_STATE], bm_r[b:b + 1, 0:D_STATE])
        hn = h0_ref[b] * dec_t[:, b:b + 1] + xdt_t[:, b:b + 1] * bsel
        hn_ref[b] = hn
        c2 = jnp.where(row8 == 0, cm[b:b + 1, 0:D_STATE],
                       jnp.where(row8 == 1, cm_r[b:b + 1, 0:D_STATE], 0.0))
        yr = _nt_dot(c2, hn)
        y_ref[b:b + 1, :] = jnp.where(lane_y, yr[0:1, :], yr[1:2, :])
    y = y_ref[...] + dskip_ref[...] * xs
    yg = y * _silu(z_ref[...])
    o_ref[...] = _rms(yg) * ng_ref[...]


def _ssd_sample(proj, st_t, h0, cw, cb, dtb_row, alog_row, dskip_f, ng, consts):
    kw, dxbc = cw.shape
    nb = proj.shape[0]
    di = SSM_HEADS * SSM_HEAD_DIM
    kern = functools.partial(_ssd_sample_kernel, nb=nb, kw=kw)
    full = lambda a: pl.BlockSpec(a.shape, lambda i: (0,) * a.ndim)
    return pl.pallas_call(
        kern,
        grid=(1,),
        in_specs=[pl.BlockSpec((nb, dxbc), lambda i: (0, OFF["xbc"] // dxbc)),
                  pl.BlockSpec((nb, di), lambda i: (0, OFF["z"] // di)),
                  pl.BlockSpec((nb, LANES), lambda i: (0, OFF["small"] // LANES)),
                  full(st_t), full(h0), full(cw), full(cb), full(dtb_row), full(alog_row),
                  full(dskip_f), full(ng), full(consts["e64"])],
        out_specs=[pl.BlockSpec((nb, di), lambda i: (0, 0)),
                   pl.BlockSpec(h0.shape, lambda i: (0, 0, 0))],
        out_shape=[jax.ShapeDtypeStruct((nb, di), F32), jax.ShapeDtypeStruct(h0.shape, F32)],
        scratch_shapes=[pltpu.VMEM((nb, di), F32)],
        compiler_params=_cp(("arbitrary",)),
        name="ssd_sample",
    )(proj, proj, proj, st_t, h0, cw, cb, dtb_row, alog_row, dskip_f, ng, consts["e64"])


def _mem_kv_kernel(x_ref, g_ref, w_ref, kg_ref, mk_ref, mv_ref):
    dm = mk_ref.shape[1]
    m = _dot(_rms(x_ref[...]) * g_ref[...], w_ref[...])
    for h in range(MEM_HEADS):
        sl = slice(h * MEM_HEAD_DIM, (h + 1) * MEM_HEAD_DIM)
        mk_ref[:, sl] = _rms(m[:, sl]) * kg_ref[...]
    mv_ref[...] = m[:, dm:]


def _mem_kv(mem2d, g, w, kg, rows):
    t, d = mem2d.shape
    dm = MEM_HEADS * MEM_HEAD_DIM
    return pl.pallas_call(
        _mem_kv_kernel,
        grid=(t // rows,),
        in_specs=[pl.BlockSpec((rows, d), lambda i: (i, 0)),
                  pl.BlockSpec((1, d), lambda i: (0, 0)),
                  pl.BlockSpec((d, 2 * dm), lambda i: (0, 0)),
                  pl.BlockSpec((1, MEM_HEAD_DIM), lambda i: (0, 0))],
        out_specs=[pl.BlockSpec((rows, dm), lambda i: (i, 0)),
                   pl.BlockSpec((rows, dm), lambda i: (i, 0))],
        out_shape=[jax.ShapeDtypeStruct((t, dm), F32), jax.ShapeDtypeStruct((t, dm), F32)],
        compiler_params=_cp(("parallel",)),
        name="mem_kv",
    )(mem2d, g, w, kg)


def _mem_attend_kernel(mq_ref, mk_ref, mv_ref, g_ref, o_ref, *, tm):
    mq = mq_ref[0]
    rows = max(tm, 8)
    if tm < rows:
        mq = jnp.broadcast_to(mq, (rows, mq.shape[1]))
    for h in range(MEM_HEADS):
        sl = slice(h * MEM_HEAD_DIM, (h + 1) * MEM_HEAD_DIM)
        qn = _rms(mq[:, sl]) * g_ref[...]
        s = _nt_dot(qn, mk_ref[0, :, sl]) * (MEM_HEAD_DIM ** -0.5)
        p = jnp.exp(s - jnp.max(s, axis=1, keepdims=True))
        o = _dot(p, mv_ref[0, :, sl]) / jnp.sum(p, axis=1, keepdims=True)
        o_ref[0, :, sl] = o[0:tm, :]


def _mem_attend(proj3, mk3, mv3, g, tm):
    nseq, seq, _ = proj3.shape
    nm = mk3.shape[1]
    dm = MEM_HEADS * MEM_HEAD_DIM
    kern = functools.partial(_mem_attend_kernel, tm=tm)
    return pl.pallas_call(
        kern,
        grid=(nseq, seq // tm),
        in_specs=[pl.BlockSpec((1, tm, dm), lambda s, i: (s, i, OFF["mq"] // dm)),
                  pl.BlockSpec((1, nm, dm), lambda s, i: (s, 0, 0)),
                  pl.BlockSpec((1, nm, dm), lambda s, i: (s, 0, 0)),
                  pl.BlockSpec((1, MEM_HEAD_DIM), lambda s, i: (0, 0))],
        out_specs=pl.BlockSpec((1, tm, dm), lambda s, i: (s, i, 0)),
        out_shape=jax.ShapeDtypeStruct((nseq, seq, dm), F32),
        compiler_params=_cp(("parallel", "arbitrary")),
        name="mem_attend",
    )(proj3, mk3, mv3, g)


def _merge_kernel(x_ref, ba_ref, bb_ref, bc_ref, bm_ref, gt_ref, wb_ref, wo_ref, o_ref):
    d = x_ref.shape[1]
    acc = None
    for n, br in enumerate((ba_ref, bb_ref, bc_ref, bm_ref)):
        term = jax.nn.sigmoid(gt_ref[:, n * d:(n + 1) * d]) * _dot(br[...].astype(MXU_DT), wb_ref[n])
        acc = term if acc is None else acc + term
    o_ref[...] = x_ref[...] + _dot(acc.astype(MXU_DT), wo_ref[...])


def _merge(x, proj, brs, wb, wo, tm):
    t, d = x.shape
    nbr, bw, _ = wb.shape
    return pl.pallas_call(
        _merge_kernel,
        grid=(t // tm,),
        in_specs=[pl.BlockSpec((tm, d), lambda i: (i, 0))]
                 + [pl.BlockSpec((tm, bw), lambda i: (i, 0))] * nbr
                 + [pl.BlockSpec((tm, nbr * d), lambda i: (i, OFF["gates"] // (nbr * d))),
                    pl.BlockSpec((nbr, bw, d), lambda i: (0, 0, 0)),
                    pl.BlockSpec((d, d), lambda i: (0, 0))],
        out_specs=pl.BlockSpec((tm, d), lambda i: (i, 0)),
        out_shape=jax.ShapeDtypeStruct((t, d), F32),
        compiler_params=_cp(("parallel",)),
        name="merge",
    )(x, *brs, proj, wb, wo)


def _ffn_prompt_kernel(x_ref, g_ref, wg_ref, wu_ref, cwg_ref, cwu_ref, cbg_ref, cbu_ref, wd_ref,
                       o_ref, unew_ref, h_ref, acc_ref, ubuf, carry, *, tm, tc, kw, nff):
    i = pl.program_id(1)
    c = pl.program_id(2)
    halo = 8
    first = halo - (kw - 1)

    @pl.when(c == 0)
    def _():
        h_ref[...] = (_rms(x_ref[...]) * g_ref[...]).astype(h_ref.dtype)
        acc_ref[...] = jnp.zeros(acc_ref.shape, F32)

    @pl.when(i == 0)
    def _():
        carry[c] = jnp.zeros((halo, 2 * tc), F32)

    h = h_ref[...]
    ubuf[0:halo, :] = carry[c]
    ubuf[halo:halo + tm, 0:tc] = _dot(h, wg_ref[...])
    ubuf[halo:halo + tm, tc:2 * tc] = _dot(h, wu_ref[...])
    carry[c] = ubuf[tm:tm + halo, :]
    fg =cbg_ref[...] + cwg_ref[0:1, :] * ubuf[first:first + tm, 0:tc]
    fu = cbu_ref[...] + cwu_ref[0:1, :] * ubuf[first:first + tm, tc:2 * tc]
    for j in range(1, kw):
        fg = fg + cwg_ref[j:j + 1, :] * ubuf[first + j:first + j + tm, 0:tc]
        fu = fu + cwu_ref[j:j + 1, :] * ubuf[first + j:first + j + tm, tc:2 * tc]
    acc_ref[...] += _dot((_silu(fg) * fu).astype(MXU_DT), wd_ref[...])

    @pl.when(c == nff - 1)
    def _():
        o_ref[...] = x_ref[...] + acc_ref[...]

    @pl.when((c == nff - 1) & (i == pl.num_programs(1) - 1))
    def _():
        for cc in range(nff):
            unew_ref[0, :, cc * tc:(cc + 1) * tc] = carry[cc, first:halo, 0:tc]
            unew_ref[0, :, (nff + cc) * tc:(nff + cc + 1) * tc] = carry[cc, first:halo, tc:2 * tc]


def _ffn_prompt(x, g, wup, cw, cb, wd, nseq, seq, tm, tc):
    t, d = x.shape
    dff = wd.shape[0]
    kw = cw.shape[0]
    nt = seq // tm
    nff = dff // tc
    kern = functools.partial(_ffn_prompt_kernel, tm=tm, tc=tc, kw=kw, nff=nff)
    return pl.pallas_call(
        kern,
        grid=(nseq, nt, nff),
        in_specs=[pl.BlockSpec((tm, d), lambda s, i, c: (s * nt + i, 0)),
                  pl.BlockSpec((1, d), lambda s, i, c: (0, 0)),
                  pl.BlockSpec((d, tc), lambda s, i, c: (0, c)),
                  pl.BlockSpec((d, tc), lambda s, i, c: (0, nff + c)),
                  pl.BlockSpec((kw, tc), lambda s, i, c: (0, c)),
                  pl.BlockSpec((kw, tc), lambda s, i, c: (0, nff + c)),
                  pl.BlockSpec((1, tc), lambda s, i, c: (0, c)),
                  pl.BlockSpec((1, tc), lambda s, i, c: (0, nff + c)),
                  pl.BlockSpec((tc, d), lambda s, i, c: (c, 0))],
        out_specs=[pl.BlockSpec((tm, d), lambda s, i, c: (s * nt + i, 0)),
                   pl.BlockSpec((1, kw - 1, 2 * dff), lambda s, i, c: (s, 0, 0))],
        out_shape=[jax.ShapeDtypeStruct((t, d), F32),
                   jax.ShapeDtypeStruct((nseq, kw - 1, 2 * dff), F32)],
        scratch_shapes=[pltpu.VMEM((tm, d), MXU_DT), pltpu.VMEM((tm, d), F32),
                        pltpu.VMEM((tm + 8, 2 * tc), F32), pltpu.VMEM((nff, 8, 2 * tc), F32)],
        compiler_params=_cp(("arbitrary", "arbitrary", "arbitrary")),
        name="ffn_prompt",
    )(x, g, wup, wup, cw, cw, cb, cb, wd)


def _ffn_sample_kernel(x_ref, g_ref, wg_ref, wu_ref, stg_ref, stu_ref, cwg_ref, cwu_ref, cbg_ref,
                       cbu_ref, wd_ref, o_ref, ug_ref, uu_ref, h_ref, acc_ref, *, kw):
    c = pl.program_id(0)

    @pl.when(c == 0)
    def _():
        h_ref[...] = (_rms(x_ref[...]) * g_ref[...]).astype(h_ref.dtype)
        acc_ref[...] = jnp.zeros(acc_ref.shape, F32)

    h = h_ref[...]
    ug = _dot(h, wg_ref[...])
    uu = _dot(h, wu_ref[...])
    ug_ref[...] = ug
    uu_ref[...] = uu
    fg = cbg_ref[...] + cwg_ref[kw - 1:kw, :] * ug
    fu = cbu_ref[...] + cwu_ref[kw - 1:kw, :] * uu
    for j in range(kw - 1):
        fg = fg + cwg_ref[j:j + 1, :] * stg_ref[j]
        fu = fu + cwu_ref[j:j + 1, :] * stu_ref[j]
    acc_ref[...] += _dot((_silu(fg) * fu).astype(MXU_DT), wd_ref[...])

    @pl.when(c == pl.num_programs(0) - 1)
    def _():
        o_ref[...] = x_ref[...] + acc_ref[...]


def _ffn_sample(x, g, wup, st_t, cw, cb, wd, tc):
    t, d = x.shape
    dff = wd.shape[0]
    kw = cw.shape[0]
    nff = dff // tc
    kern = functools.partial(_ffn_sample_kernel, kw=kw)
    return pl.pallas_call(
        kern,
        grid=(nff,),
        in_specs=[pl.BlockSpec((t, d), lambda c: (0, 0)),
                  pl.BlockSpec((1, d), lambda c: (0, 0)),
                  pl.BlockSpec((d, tc), lambda c: (0, c)),
                  pl.BlockSpec((d, tc), lambda c: (0, nff + c)),
                  pl.BlockSpec((kw - 1, t, tc), lambda c: (0, 0, c)),
                  pl.BlockSpec((kw - 1, t, tc), lambda c: (0, 0, nff + c)),
                  pl.BlockSpec((kw, tc), lambda c: (0, c)),
                  pl.BlockSpec((kw, tc), lambda c: (0, nff + c)),
                  pl.BlockSpec((1, tc), lambda c: (0, c)),
                  pl.BlockSpec((1, tc), lambda c: (0, nff + c)),
                  pl.BlockSpec((tc, d), lambda c: (c, 0))],
        out_specs=[pl.BlockSpec((t, d), lambda c: (0, 0)),
                   pl.BlockSpec((t, tc), lambda c: (0, c)),
                   pl.BlockSpec((t, tc), lambda c: (0, c))],
        out_shape=[jax.ShapeDtypeStruct((t, d), F32), jax.ShapeDtypeStruct((t, dff), F32),
                   jax.ShapeDtypeStruct((t, dff), F32)],
        scratch_shapes=[pltpu.VMEM((t, d), MXU_DT), pltpu.VMEM((t, d), F32)],
        compiler_params=_cp(("arbitrary",)),
        name="ffn_sample",
    )(x, g, wup, wup, st_t, st_t, cw, cw, cb, cb, wd)


def _constants():
    lane = np.arange(LANES)
    r = lane % HEAD_DIM
    rot = HEAD_DIM // 4
    half = rot // 2
    inv_freq = ROPE_THETA ** (-jnp.arange(half, dtype=F32) * (2.0 / rot))
    in_rot = r < rot
    invf = jnp.where(jnp.asarray(in_rot), inv_freq[jnp.asarray(r % half)], 0.0).astype(F32)[None, :]
    sgn = np.where(r < half, -1.0, np.where(in_rot, 1.0, 0.0)).astype(np.float32)[None, :]
    first_head = (lane < HEAD_DIM)[None, :]
    dq = N_HEADS * HEAD_DIM
    bd = (np.arange(dq)[:, None] // HEAD_DIM == np.arange(dq)[None, :] // HEAD_DIM)
    di = SSM_HEADS * SSM_HEAD_DIM
    e64 = np.zeros((LANES, di), np.float32)
    e128 = np.zeros((LANES, SSM_HEADS * LANES), np.float32)
    for h in range(SSM_HEADS):
        e64[SM_DT + h, h * SSM_HEAD_DIM:(h + 1) * SSM_HEAD_DIM] = 1.0
        e128[SM_DT + h, h * LANES:(h + 1) * LANES] = 1.0
    tril = np.tril(np.ones((SSM_CHUNK, SSM_CHUNK), np.float32))
    hpg = SSM_HEADS // SSM_GROUPS
    bmask = (np.arange(SSM_GROUPS * D_STATE)[:, None] // D_STATE
             == np.arange(di)[None, :] // (hpg * SSM_HEAD_DIM)).astype(np.float32)
    return dict(
        invf=invf, sgn=jnp.asarray(sgn),
        invf_s=jnp.where(jnp.asarray(first_head), invf, 0.0),
        sgn_s=jnp.asarray(np.where(first_head, sgn, 0.0).astype(np.float32)),
        bd=jnp.asarray(bd.astype(np.float32), dtype=BF16),
        e64=jnp.asarray(e64, dtype=BF16), e128=jnp.asarray(e128, dtype=BF16),
        tril=jnp.asarray(tril, dtype=BF16), bmask=jnp.asarray(bmask))


def _reorder_w_in(w_in):
    d = w_in.shape[0]
    sizes = dict(glu=1024, q=512, k=128, v=128, qi=512, ki=64, wi=8, z=512, xbc=768, dt=8, mq=512,
                 gates=4096)
    order_in = ["glu", "q", "k", "v", "qi", "ki", "wi", "z", "xbc", "dt", "mq", "gates"]
    parts, off = {}, 0
    for name in order_in:
        parts[name] = w_in[:, off:off + sizes[name]]
        off += sizes[name]
    assert off == w_in.shape[1]
    pad = jnp.zeros((d, LANES - sizes["ki"] - sizes["wi"] - sizes["dt"]), w_in.dtype)
    out = jnp.concatenate([parts[n] for n in ("glu", "q", "qi", "z", "mq", "xbc", "k", "v", "gates",
                                              "ki", "wi", "dt")] + [pad], axis=1)
    assert out.shape[1] == PROJ_COLS
    return out


def _pad_lanes(v, start):
    return jnp.zeros((1, LANES), F32).at[0, start:start + v.shape[0]].set(v)


def _tile(n, cap):
    return min(n, cap)


def _layer_params(l, prm, consts):
    p = dict(
        w_in=_reorder_w_in(prm["w_in"][l]).astype(MXU_DT),
        norm_mix_g=prm["norm_mix_g"][l][None, :],
        conv_a_w=prm["conv_a_w"][l], conv_a_b=prm["conv_a_b"][l][None, :],
        ln_a_g=prm["ln_a_g"][l][None, :], ln_a_b=prm["ln_a_b"][l][None, :],
        qg=jnp.tile(prm["q_norm_g"][l], N_HEADS)[None, :],
        kg=jnp.tile(prm["k_norm_g"][l], N_KV_HEADS)[None, :],
        ssm_conv_w=prm["ssm_conv_w"][l], ssm_conv_b=prm["ssm_conv_b"][l][None, :],
        dtb_row=_pad_lanes(prm["dt_bias"][l], SM_DT), alog_row=_pad_lanes(prm["a_log"][l], SM_DT),
        dskip_f=jnp.repeat(prm["d_skip"][l], SSM_HEAD_DIM)[None, :],
        ssm_norm_g=prm["ssm_norm_g"][l][None, :],
        mem_norm_g=prm["mem_norm_g"][l][None, :], w_mem_kv=prm["w_mem_kv"][l],
        mq_norm_g=prm["mq_norm_g"][l][None, :], mk_norm_g=prm["mk_norm_g"][l][None, :],
        w_branch=prm["w_branch"][l].astype(MXU_DT), w_out=prm["w_out"][l].astype(MXU_DT),
        norm_ffn_g=prm["norm_ffn_g"][l][None, :], w_ffn_up=prm["w_ffn_up"][l].astype(MXU_DT),
        ffn_conv_w=prm["ffn_conv_w"][l], ffn_conv_b=prm["ffn_conv_b"][l][None, :],
        w_ffn_down=prm["w_ffn_down"][l].astype(MXU_DT))
    return p


def _ssm_state_from_slab(slab):
    nseq = slab.shape[0]
    hpg = SSM_HEADS // SSM_GROUPS
    s = slab.reshape(nseq, SSM_GROUPS, D_STATE, SSM_HEADS, SSM_HEAD_DIM)
    per_head = [s[:, h // hpg, :, h, :] for h in range(SSM_HEADS)]
    return jnp.swapaxes(jnp.stack(per_head, axis=1), 2, 3)


def _prompt_layer(x, p, consts, mem2d, nseq, seq):
    t, d = x.shape
    nm = mem2d.shape[0] // nseq
    proj = _norm_proj(x, p["norm_mix_g"], p["w_in"], _tile(t, 1024), PROJ_COLS // 5)
    br_a, conf_new = _conf_prompt(proj, p["conv_a_w"], p["conv_a_b"], p["ln_a_g"], p["ln_a_b"],
                                  nseq, seq, _tile(seq, 512))
    ck = _tile(seq, 512)
    kn, v, kir, knb, smb, qt, qit, vt, smt = _dsa_prep_t(proj, p["qg"], p["kg"], consts, nseq, seq, ck)
    br_b = _dsa_prompt_t(qt, qit, smt, smb, knb, vt, nseq, seq, _tile(seq, 256), ck)
    br_c, sconv_new, sslab = _ssd_prompt(proj, p["ssm_conv_w"], p["ssm_conv_b"], p["dtb_row"],
                                         p["alog_row"], p["dskip_f"], p["ssm_norm_g"], consts,
                                         nseq, seq, _tile(seq, 512))
    mk, mv = _mem_kv(mem2d, p["mem_norm_g"], p["w_mem_kv"], p["mk_norm_g"], nm)
    dm = MEM_HEADS * MEM_HEAD_DIM
    br_m = _mem_attend(proj.reshape(nseq, seq, PROJ_COLS), mk.reshape(nseq, nm, dm),
                       mv.reshape(nseq, nm, dm), p["mq_norm_g"], _tile(seq, 512)).reshape(t, dm)
    x = _merge(x, proj, (br_a, br_b, br_c, br_m), p["w_branch"], p["w_out"], _tile(t, 256))
    x, ffn_new = _ffn_prompt(x, p["norm_ffn_g"], p["w_ffn_up"], p["ffn_conv_w"], p["ffn_conv_b"],
                             p["w_ffn_down"], nseq, seq, _tile(seq, 512), 256)
    state = (kn.reshape(nseq, seq, N_KV_HEADS, HEAD_DIM), v.reshape(nseq, seq, N_KV_HEADS, HEAD_DIM),
             kir.reshape(nseq, seq, D_IDX),
             mk.reshape(nseq, nm, MEM_HEADS, MEM_HEAD_DIM), mv.reshape(nseq, nm, MEM_HEADS, MEM_HEAD_DIM),
             conf_new, sconv_new, _ssm_state_from_slab(sslab), ffn_new)
    return x, state


def _sample_layer(x, p, consts, page_table, ck, cv, cki, cmk, cmv, st_conf, st_sconv, st_ssm, st_ffn):
    nb, d = x.shape
    n_pages = page_table.shape[1]
    page = ck.shape[1]
    past = n_pages * page
    dq = N_HEADS * HEAD_DIM
    dk = N_KV_HEADS * HEAD_DIM
    proj = _norm_proj(x, p["norm_mix_g"], p["w_in"], nb, PROJ_COLS // 5)
    br_a, a_new = _conf_sample(proj, jnp.swapaxes(st_conf, 0, 1), p["conv_a_w"], p["conv_a_b"],
                               p["ln_a_g"], p["ln_a_b"])
    qn, kn, v, qir, smr, kir = _dsa_prep(proj, p["qg"], p["kg"], consts, nb, 1, float(past))
    hpk = N_HEADS // N_KV_HEADS
    qh = qn.reshape(nb, N_HEADS, HEAD_DIM)
    grp = (np.arange(N_HEADS)[:, None] // hpk == np.arange(dk)[None, :] // HEAD_DIM)
    q8 = jnp.where(jnp.asarray(grp)[None], jnp.tile(qh, (1, 1, N_KV_HEADS)), 0.0)
    o8 = _dsa_sample(page_table, q8, qir.reshape(nb, N_IDX_HEADS, D_IDX),
                     smr[:, SM_WI:SM_WI + N_IDX_HEADS].reshape(nb, N_IDX_HEADS, 1),
                     kir.reshape(nb, 1, D_IDX), kn.reshape(nb, 1, dk), v.reshape(nb, 1, dk),
                     cki, ck.reshape(ck.shape[0], page, dk), cv.reshape(cv.shape[0], page, dk))
    o8 = o8.reshape(nb, N_HEADS, N_KV_HEADS, HEAD_DIM)
    br_b = jnp.stack([o8[:, h, h // hpk, :] for h in range(N_HEADS)], axis=1).reshape(nb, dq)
    h0 = st_ssm.reshape(nb, SSM_HEADS * SSM_HEAD_DIM, D_STATE)
    br_c, h_new = _ssd_sample(proj, jnp.swapaxes(st_sconv, 0, 1), h0, p["ssm_conv_w"], p["ssm_conv_b"],
                              p["dtb_row"], p["alog_row"], p["dskip_f"], p["ssm_norm_g"], consts)
    nm = cmk.shape[1]
    dm = MEM_HEADS * MEM_HEAD_DIM
    br_m = _mem_attend(proj.reshape(nb, 1, PROJ_COLS), cmk.reshape(nb, nm, dm), cmv.reshape(nb, nm, dm),
                       p["mq_norm_g"], 1).reshape(nb, dm)
    x = _merge(x, proj, (br_a, br_b, br_c, br_m), p["w_branch"], p["w_out"], nb)
    x, ug, uu = _ffn_sample(x, p["norm_ffn_g"], p["w_ffn_up"], jnp.swapaxes(st_ffn, 0, 1),
                            p["ffn_conv_w"], p["ffn_conv_b"], p["w_ffn_down"], 256)
    xbc_raw = proj[:, OFF["xbc"]:OFF["xbc"] + st_sconv.shape[-1]]
    state = (kn.reshape(nb, 1, N_KV_HEADS, HEAD_DIM), v.reshape(nb, 1, N_KV_HEADS, HEAD_DIM),
             kir.reshape(nb, 1, D_IDX),
             jnp.concatenate([st_conf[:, 1:], a_new[:, None]], axis=1),
             jnp.concatenate([st_sconv[:, 1:], xbc_raw[:, None]], axis=1),
             h_new.reshape(st_ssm.shape),
             jnp.concatenate([st_ffn[:, 1:], jnp.concatenate([ug, uu], axis=-1)[:, None]], axis=1))
    return x, state


def kernel(x_prompt, x_sample, cache_k, cache_v, cache_kidx, cache_mem_k, cache_mem_v, state_conformer, state_ssm_conv, state_ssm, state_ffn_conv, page_table, mem_prompt, norm_mix_g, w_in, conv_a_w, conv_a_b, ln_a_g, ln_a_b, q_norm_g, k_norm_g, ssm_conv_w, ssm_conv_b, dt_bias, a_log, d_skip, ssm_norm_g, mem_norm_g, w_mem_kv, mq_norm_g, mk_norm_g, w_branch, w_out, norm_ffn_g, w_ffn_up, ffn_conv_w, ffn_conv_b, w_ffn_down):
    prm = dict(norm_mix_g=norm_mix_g, w_in=w_in, conv_a_w=conv_a_w, conv_a_b=conv_a_b, ln_a_g=ln_a_g,
               ln_a_b=ln_a_b, q_norm_g=q_norm_g, k_norm_g=k_norm_g, ssm_conv_w=ssm_conv_w,
               ssm_conv_b=ssm_conv_b, dt_bias=dt_bias, a_log=a_log, d_skip=d_skip, ssm_norm_g=ssm_norm_g,
               mem_norm_g=mem_norm_g, w_mem_kv=w_mem_kv, mq_norm_g=mq_norm_g, mk_norm_g=mk_norm_g,
               w_branch=w_branch, w_out=w_out, norm_ffn_g=norm_ffn_g, w_ffn_up=w_ffn_up,
               ffn_conv_w=ffn_conv_w, ffn_conv_b=ffn_conv_b, w_ffn_down=w_ffn_down)
    depth = w_in.shape[0]
    nseq, seq, d = x_prompt.shape
    nb, dseq, _ = x_sample.shape
    assert dseq == 1
    consts = _constants()
    xp = x_prompt.reshape(nseq * seq, d)
    xs = x_sample.reshape(nb, d)
    mem2d = mem_prompt.reshape(nseq * mem_prompt.shape[1], d)
    p_states, s_states = [], []
    for l in range(depth):
        p = _layer_params(l, prm, consts)
        xp, st = _prompt_layer(xp, p, consts, mem2d, nseq, seq)
        p_states.append(st)
        xs, st = _sample_layer(xs, p, consts, page_table, cache_k[l], cache_v[l], cache_kidx[l],
                               cache_mem_k[l], cache_mem_v[l], state_conformer[l], state_ssm_conv[l],
                               state_ssm[l], state_ffn_conv[l])
        s_states.append(st)
    stack = lambda states, k: jnp.stack([s[k] for s in states])
    return ((xp.reshape(nseq, seq, d), xs.reshape(nb, 1, d))
            + tuple(stack(p_states, k) for k in range(9))
            + tuple(stack(s_states, k) for k in range(7)))
```

```python
import functools
import math

import numpy as np
import jax
import jax.numpy as jnp
from jax import lax
from jax.experimental import pallas as pl
from jax.experimental.pallas import tpu as pltpu

F32 = jnp.float32
BF16 = jnp.bfloat16
I32 = jnp.int32
MXU_DT = BF16

EPS = 1e-6
ROPE_THETA = 500000.0
LANES = 128
V7X_VMEM_LIMIT = 56 * 1024 * 1024

N_HEADS = 8
HEAD_DIM = 64
N_KV_HEADS = 2
N_IDX_HEADS = 8
D_IDX = 64
TOP_K = 256
SSM_HEADS = 8
SSM_HEAD_DIM = 64
SSM_GROUPS = 2
D_STATE = 64
SSM_CHUNK = 128
MEM_HEADS = 4
MEM_HEAD_DIM = 128
IDX_SCALE = (D_IDX ** -0.5) * (N_IDX_HEADS ** -0.5)
NEG_BIG = -1e30
LOG2E = math.log2(math.e)
INT_MIN = -2 ** 31

SM_KI = 0
SM_WI = 64
SM_DT = 72

OFF = dict(glu=0, q=1024, qi=1536, z=2048, mq=2560, xbc=3072, k=3840, v=3968, gates=4096, small=8192)
PROJ_COLS = 8320


def _cp(sem):
    return pltpu.CompilerParams(dimension_semantics=sem, vmem_limit_bytes=V7X_VMEM_LIMIT)


def _nt_dot(a, b):
    return lax.dot_general(a, b, (((1,), (1,)), ((), ())), preferred_element_type=F32)


def _dot(a, b):
    return jnp.dot(a, b, preferred_element_type=F32)


def _split3(x):
    hi = x.astype(BF16)
    r = x - hi.astype(F32)
    mid = r.astype(BF16)
    lo = (r - mid.astype(F32)).astype(BF16)
    return hi, mid, lo


def _dot_sel(x, e):
    hi, mid, lo = _split3(x)
    return _dot(hi, e) + _dot(mid, e) + _dot(lo, e)


def _sel_dot(e, x):
    hi, mid, lo = _split3(x)
    return _dot(e, hi) + _dot(e, mid) + _dot(e, lo)


def _silu(x):
    return x * jax.nn.sigmoid(x)


def _softplus(x):
    return jnp.maximum(x, 0.0) + jnp.log1p(jnp.exp(-jnp.abs(x)))


def _rms(x):
    return x * lax.rsqrt(jnp.mean(x * x, axis=-1, keepdims=True) + EPS)


def _sortable(x):
    b = lax.bitcast_convert_type(x, I32)
    b = jnp.where(x == 0.0, 0, b)
    return jnp.where(b < 0, b ^ 0x7FFFFFFF, b)


def _norm_proj_kernel(x_ref, g_ref, w_ref, o_ref, h_ref):
    @pl.when(pl.program_id(1) == 0)
    def _():
        h_ref[...] = (_rms(x_ref[...]) * g_ref[...]).astype(h_ref.dtype)

    o_ref[...] = _dot(h_ref[...], w_ref[...])


def _norm_proj(x, g, w, tm, tn):
    t, d = x.shape
    n = w.shape[1]
    return pl.pallas_call(
        _norm_proj_kernel,
        grid=(t // tm, n // tn),
        in_specs=[pl.BlockSpec((tm, d), lambda i, j: (i, 0)),
                  pl.BlockSpec((1, d), lambda i, j: (0, 0)),
                  pl.BlockSpec((d, tn), lambda i, j: (0, j))],
        out_specs=pl.BlockSpec((tm, tn), lambda i, j: (i, j)),
        out_shape=jax.ShapeDtypeStruct((t, n), F32),
        scratch_shapes=[pltpu.VMEM((tm, d), MXU_DT)],
        compiler_params=_cp(("parallel", "arbitrary")),
        name="norm_proj",
    )(x, g, w)


def _layernorm_silu(u, g, b):
    xc = u - jnp.mean(u, axis=-1, keepdims=True)
    var = jnp.mean(xc * xc, axis=-1, keepdims=True)
    return _silu(xc * lax.rsqrt(var + EPS) * g + b)


def _conf_prompt_kernel(glu_ref, w_ref, b_ref, lg_ref, lb_ref, o_ref, cnew_ref, abuf, *, tm, cw, dc):
    halo = 32

    @pl.when(pl.program_id(1) == 0)
    def _():
        abuf[0:halo, :] = jnp.zeros((halo, dc), F32)

    glu = glu_ref[...]
    abuf[halo:halo + tm, :] = glu[:, :dc] * jax.nn.sigmoid(glu[:, dc:])
    first = halo - (cw - 1)
    u = b_ref[...] + w_ref[0:1, :] * abuf[first:first + tm, :]
    for j in range(1, cw):
        u = u + w_ref[j:j + 1, :] * abuf[first + j:first + j + tm, :]
    o_ref[...] = _layernorm_silu(u, lg_ref[...], lb_ref[...])
    cnew_ref[0] = abuf[tm + first:tm + halo, :]
    abuf[0:halo, :] = abuf[tm:tm + halo, :]


def _conf_prompt(proj, w, b, lg, lb, nseq, seq, tm):
    cw, dc = w.shape
    nt = seq // tm
    kern = functools.partial(_conf_prompt_kernel, tm=tm, cw=cw, dc=dc)
    return pl.pallas_call(
        kern,
        grid=(nseq, nt),
        in_specs=[pl.BlockSpec((tm, 2 * dc), lambda s, i: (s * nt + i, OFF["glu"] // (2 * dc))),
                  pl.BlockSpec((cw, dc), lambda s, i: (0, 0)),
                  pl.BlockSpec((1, dc), lambda s, i: (0, 0)),
                  pl.BlockSpec((1, dc), lambda s, i: (0, 0)),
                  pl.BlockSpec((1, dc), lambda s, i: (0, 0))],
        out_specs=[pl.BlockSpec((tm, dc), lambda s, i: (s * nt + i, 0)),
                   pl.BlockSpec((1, cw - 1, dc), lambda s, i: (s, 0, 0))],
        out_shape=[jax.ShapeDtypeStruct((nseq * seq, dc), F32),
                   jax.ShapeDtypeStruct((nseq, cw - 1, dc), F32)],
        scratch_shapes=[pltpu.VMEM((tm + 32, dc), F32)],
        compiler_params=_cp(("arbitrary", "arbitrary")),
        name="conformer_prompt",
    )(proj, w, b, lg, lb)


def _conf_sample_kernel(glu_ref, st_ref, w_ref, b_ref, lg_ref, lb_ref, o_ref, a_ref, *, cw, dc):
    glu = glu_ref[...]
    a = glu[:, :dc] * jax.nn.sigmoid(glu[:, dc:])
    u = b_ref[...] + w_ref[cw - 1:cw, :] * a
    for j in range(cw - 1):
        u = u + w_ref[j:j + 1, :] * st_ref[j]
    o_ref[...] = _layernorm_silu(u, lg_ref[...], lb_ref[...])
    a_ref[...] = a


def _conf_sample(proj, st_t, w, b, lg, lb):
    cw, dc = w.shape
    n = proj.shape[0]
    kern = functools.partial(_conf_sample_kernel, cw=cw, dc=dc)
    return pl.pallas_call(
        kern,
        grid=(1,),
        in_specs=[pl.BlockSpec((n, 2 * dc), lambda i: (0, OFF["glu"] // (2 * dc))),
                  pl.BlockSpec((cw - 1, n, dc), lambda i: (0, 0, 0)),
                  pl.BlockSpec((cw, dc), lambda i: (0, 0)),
                  pl.BlockSpec((1, dc), lambda i: (0, 0)),
                  pl.BlockSpec((1, dc), lambda i: (0, 0)),
                  pl.BlockSpec((1, dc), lambda i: (0, 0))],
        out_specs=[pl.BlockSpec((n, dc), lambda i: (0, 0)),
                   pl.BlockSpec((n, dc), lambda i: (0, 0))],
        out_shape=[jax.ShapeDtypeStruct((n, dc), F32), jax.ShapeDtypeStruct((n, dc), F32)],
        compiler_params=_cp(("arbitrary",)),
        name="conformer_sample",
    )(proj, st_t, w, b, lg, lb)


def _rope128(x, c, s):
    lane = lax.broadcasted_iota(I32, x.shape, 1) % HEAD_DIM
    partner = jnp.where(lane < 8, pltpu.roll(x, LANES - 8, 1), pltpu.roll(x, 8, 1))
    return x * c + partner * s


def _dsa_prep_kernel(q_ref, qi_ref, k_ref, v_ref, sm_ref, qg_ref, kg_ref, invf_ref, sgn_ref,
                     invf_s_ref, sgn_s_ref, bd_ref,
                     qn_ref, kn_ref, vo_ref, qir_ref, smr_ref, kir_ref, *, tm, seq, const_pos):
    if const_pos is None:
        base = (pl.program_id(0) * tm) % seq
        pos = (base + lax.broadcasted_iota(I32, (tm, LANES), 0)).astype(F32)
    else:
        pos = jnp.full((tm, LANES), const_pos, F32)
    ang = pos * invf_ref[...]
    c = jnp.cos(ang)
    s = jnp.sin(ang) * sgn_ref[...]
    ang_s = pos * invf_s_ref[...]
    c_s = jnp.cos(ang_s)
    s_s = jnp.sin(ang_s) * sgn_s_ref[...]

    def seg_rms(x, bd):
        x2 = x * x
        hi = x2.astype(BF16)
        lo = (x2 - hi.astype(F32)).astype(BF16)
        ms = (_dot(hi, bd) + _dot(lo, bd)) * (1.0 / HEAD_DIM)
        return x * lax.rsqrt(ms + EPS)

    qn = seg_rms(q_ref[...], bd_ref[...]) * qg_ref[...]
    qi = qi_ref[...]
    for j in range(q_ref.shape[1] // LANES):
        sl = slice(j * LANES, (j + 1) * LANES)
        qn_ref[:, sl] = _rope128(qn[:, sl], c, s)
        qir_ref[:, sl] = _rope128(qi[:, sl], c, s)
    kn = seg_rms(k_ref[...], bd_ref[0:LANES, 0:LANES]) * kg_ref[...]
    kn_ref[...] = _rope128(kn, c, s)
    vo_ref[...] = v_ref[...]
    smr = _rope128(sm_ref[...], c_s, s_s)
    smr_ref[...] = smr
    kir_ref[...] = smr[:, 0:D_IDX]


def _dsa_prep(proj, qg, kg, consts, tm, seq, const_pos):
    t = proj.shape[0]
    dq = N_HEADS * HEAD_DIM
    dk = N_KV_HEADS * HEAD_DIM
    kern = functools.partial(_dsa_prep_kernel, tm=tm, seq=seq, const_pos=const_pos)
    row = lambda w: pl.BlockSpec((1, w), lambda i: (0, 0))
    return pl.pallas_call(
        kern,
        grid=(t // tm,),
        in_specs=[pl.BlockSpec((tm, dq), lambda i: (i, OFF["q"] // dq)),
                  pl.BlockSpec((tm, dq), lambda i: (i, OFF["qi"] // dq)),
                  pl.BlockSpec((tm, dk), lambda i: (i, OFF["k"] // dk)),
                  pl.BlockSpec((tm, dk), lambda i: (i, OFF["v"] // dk)),
                  pl.BlockSpec((tm, LANES), lambda i: (i, OFF["small"] // LANES)),
                  row(dq), row(dk), row(LANES), row(LANES), row(LANES), row(LANES),
                  pl.BlockSpec((dq, dq), lambda i: (0, 0))],
        out_specs=[pl.BlockSpec((tm, dq), lambda i: (i, 0)),
                   pl.BlockSpec((tm, dk), lambda i: (i, 0)),
                   pl.BlockSpec((tm, dk), lambda i: (i, 0)),
                   pl.BlockSpec((tm, dq), lambda i: (i, 0)),
                   pl.BlockSpec((tm, LANES), lambda i: (i, 0)),
                   pl.BlockSpec((tm, D_IDX), lambda i: (i, 0))],
        out_shape=[jax.ShapeDtypeStruct((t, dq), F32), jax.ShapeDtypeStruct((t, dk), F32),
                   jax.ShapeDtypeStruct((t, dk), F32), jax.ShapeDtypeStruct((t, dq), F32),
                   jax.ShapeDtypeStruct((t, LANES), F32), jax.ShapeDtypeStruct((t, D_IDX), F32)],
        compiler_params=_cp(("parallel",)),
        name="dsa_prep",
    )(proj, proj, proj, proj, proj, qg, kg, consts["invf"], consts["sgn"], consts["invf_s"],
      consts["sgn_s"], consts["bd"])


def _dsa_prep_t_kernel(q_ref, qi_ref, k_ref, v_ref, sm_ref, qg_ref, kg_ref, invf_ref, sgn_ref,
                       invf_s_ref, sgn_s_ref, bd_ref,
                       kn_ref, vo_ref, kir_ref, knb_ref, smb_ref, qt_ref, qit_ref, vt_ref, smt_ref, *, tm):
    pos = (pl.program_id(1) * tm + lax.broadcasted_iota(I32, (tm, LANES), 0)).astype(F32)
    ang = pos * invf_ref[...]
    c = jnp.cos(ang)
    s = jnp.sin(ang) * sgn_ref[...]
    ang_s = pos * invf_s_ref[...]
    c_s = jnp.cos(ang_s)
    s_s = jnp.sin(ang_s) * sgn_s_ref[...]

    def seg_rms(x, bd):
        x2 = x * x
        hi = x2.astype(BF16)
        lo = (x2 - hi.astype(F32)).astype(BF16)
        ms = (_dot(hi, bd) + _dot(lo, bd)) * (1.0 / HEAD_DIM)
        return x * lax.rsqrt(ms + EPS)

    lo_half = lax.broadcasted_iota(I32, (tm, LANES), 1) < HEAD_DIM
    qn = seg_rms(q_ref[...], bd_ref[...]) * qg_ref[...]
    qi = qi_ref[...]
    for j in range(N_HEADS // 2):
        sl = slice(j * LANES, (j + 1) * LANES)
        q2 = _rope128(qn[:, sl], c, s) * (HEAD_DIM ** -0.5 * LOG2E)
        q2r = pltpu.roll(q2, HEAD_DIM, 1)
        if (2 * j) // (N_HEADS // N_KV_HEADS) == 0:
            qa, qb = jnp.where(lo_half, q2, 0.0), jnp.where(lo_half, q2r, 0.0)
        else:
            qa, qb = jnp.where(lo_half, 0.0, q2r), jnp.where(lo_half, 0.0, q2)
        qt_ref[0, 2 * j] = qa.T.astype(MXU_DT)
        qt_ref[0, 2 * j + 1] = qb.T.astype(MXU_DT)
        qi2 = _rope128(qi[:, sl], c, s)
        qit_ref[0, 2 * j] = jnp.where(lo_half, qi2, 0.0).T.astype(MXU_DT)
        qit_ref[0, 2 * j + 1] = jnp.where(lo_half, pltpu.roll(qi2, HEAD_DIM, 1), 0.0).T.astype(MXU_DT)
    kn = _rope128(seg_rms(k_ref[...], bd_ref[0:LANES, 0:LANES]) * kg_ref[...], c, s)
    kn_ref[...] = kn
    knb_ref[...] = kn.astype(MXU_DT)
    v = v_ref[...]
    vo_ref[...] = v
    vt_ref[0, 0] = v.T.astype(MXU_DT)
    smr = _rope128(sm_ref[...], c_s, s_s)
    smb_ref[...] = smr.astype(MXU_DT)
    kir_ref[...] = smr[:, 0:D_IDX]
    smt_ref[0] = smr.T


def _dsa_prep_t(proj, qg, kg, consts, nseq, seq, tm):
    t = proj.shape[0]
    dq = N_HEADS * HEAD_DIM
    dk = N_KV_HEADS * HEAD_DIM
    nt = seq // tm
    kern = functools.partial(_dsa_prep_t_kernel, tm=tm)
    row = lambda w: pl.BlockSpec((1, w), lambda s, i: (0, 0))
    tok = lambda w, col: pl.BlockSpec((tm, w), lambda s, i: (s * nt + i, col))
    return pl.pallas_call(
        kern,
        grid=(nseq, nt),
        in_specs=[tok(dq, OFF["q"] // dq), tok(dq, OFF["qi"] // dq), tok(dk, OFF["k"] // dk),
                  tok(dk, OFF["v"] // dk), tok(LANES, OFF["small"] // LANES),
                  row(dq), row(dk), row(LANES), row(LANES), row(LANES), row(LANES),
                  pl.BlockSpec((dq, dq), lambda s, i: (0, 0))],
        out_specs=[tok(dk, 0), tok(dk, 0), tok(D_IDX, 0), tok(dk, 0), tok(LANES, 0),
                   pl.BlockSpec((1, N_HEADS, LANES, tm), lambda s, i: (s, 0, 0, i)),
                   pl.BlockSpec((1, N_IDX_HEADS, LANES, tm), lambda s, i: (s, 0, 0, i)),
                   pl.BlockSpec((1, 1, dk, tm), lambda s, i: (s, i, 0, 0)),
                   pl.BlockSpec((1, LANES, tm), lambda s, i: (s, 0, i))],
        out_shape=[jax.ShapeDtypeStruct((t, dk), F32), jax.ShapeDtypeStruct((t, dk), F32),
                   jax.ShapeDtypeStruct((t, D_IDX), F32),
                   jax.ShapeDtypeStruct((t, dk), MXU_DT), jax.ShapeDtypeStruct((t, LANES), MXU_DT),
                   jax.ShapeDtypeStruct((nseq, N_HEADS, LANES, seq), MXU_DT),
                   jax.ShapeDtypeStruct((nseq, N_IDX_HEADS, LANES, seq), MXU_DT),
                   jax.ShapeDtypeStruct((nseq, nt, dk, tm), MXU_DT),
                   jax.ShapeDtypeStruct((nseq, LANES, seq), F32)],
        compiler_params=_cp(("parallel", "parallel")),
        name="dsa_prep_prompt",
    )(proj, proj, proj, proj, proj, qg, kg, consts["invf"], consts["sgn"], consts["invf_s"],
      consts["sgn_s"], consts["bd"])


def _dsa_prompt_t_kernel(qt_ref, qit_ref, smt_ref, smk_ref, k_ref, vt_ref, o_ref,
                         keys_ref, k16_ref, acc_ref, thr_ref, ngt_ref, neq_ref, *, tq, ck, ksel, nbits):
    i = pl.program_id(1)
    nc = (i * tq + tq + ck - 1) // ck
    qidx = i * tq + lax.broadcasted_iota(I32, (ck, tq), 1)
    krow = lax.broadcasted_iota(I32, (ck, tq), 0)

    def p1(c, carry):
        kc = smk_ref[pl.ds(pl.multiple_of(c * ck, ck), ck), :]
        acc = jnp.zeros((ck, tq), F32)
        for h in range(N_IDX_HEADS):
            s = _dot(kc, qit_ref[0, h])
            acc = acc + jnp.maximum(s, 0.0) * smt_ref[0, SM_WI + h:SM_WI + h + 1, :]
        causal = c * ck + krow <= qidx
        sc = acc * IDX_SCALE
        bits = jnp.where(sc == 0.0, 0, lax.bitcast_convert_type(sc, I32))
        keys_ref[c] = jnp.where(causal, jnp.where(bits < 0, bits ^ 0x7FFFFFFF, bits), INT_MIN)
        half = lax.bitcast_convert_type(jnp.where(causal, bits & -65536, -1), F32)
        k16_ref[c] = half.astype(BF16)
        return carry

    lax.fori_loop(0, nc, p1, 0)

    def count(pred):
        def body(c, part):
            hit = pred(keys_ref[c], c * ck + krow)
            return part + jnp.sum(hit.reshape(ck // 8, 8, tq), axis=0)

        part = lax.fori_loop(0, nc, body, jnp.zeros((8, tq), I32))
        return jnp.sum(part, axis=0, keepdims=True)

    def bisect(thr0, nsteps):
        def bit_step(t, thr):
            cand = thr + lax.shift_left(jnp.int32(1), nsteps - 1 - t)
            cnt = count(lambda kv, kidx: jnp.where(kv >= cand, 1, 0))
            return jnp.where(cnt >= ksel, cand, thr)

        return lax.fori_loop(0, nsteps, bit_step, thr0)

    one16 = jnp.ones((ck, tq), BF16)
    zero16 = jnp.zeros((ck, tq), BF16)

    def bit16_step(t, thr16):
        cand = thr16 + lax.shift_left(jnp.int32(1), 15 - t)
        raw = jnp.where(cand < 0, cand ^ 0x7FFF, cand)
        cand_f = lax.bitcast_convert_type(lax.shift_left(raw, 16), F32).astype(BF16)

        def body(c, part):
            hit = jnp.where(k16_ref[c] >= cand_f, one16, zero16).reshape(ck // 16, 16, tq)
            tot = hit[0]
            for r in range(1, ck // 16):
                tot = tot + hit[r]
            return part + tot.astype(F32)

        part = lax.fori_loop(0, nc, body, jnp.zeros((16, tq), F32))
        cnt = jnp.sum(part, axis=0, keepdims=True)
        return jnp.where(cnt >= ksel, cand, thr16)

    thr16 = lax.fori_loop(0, 16, bit16_step, jnp.full((1, tq), -(2 ** 15), I32))
    thr = bisect(lax.shift_left(thr16, 16), 16)

    def tallies(thr):
        return (count(lambda kv, kidx: jnp.where(kv > thr, 1, 0)),
                count(lambda kv, kidx: jnp.where(kv == thr, 1, 0)))

    n_gt, n_eq = tallies(thr)
    thr_ref[...] = thr
    ngt_ref[...] = n_gt
    neq_ref[...] = n_eq
    missed = jnp.max(jnp.where((n_gt >= ksel) | (n_gt + n_eq < ksel), 1, 0))

    @pl.when(missed > 0)
    def _():
        thr_full = bisect(jnp.full((1, tq), INT_MIN, I32), 32)
        n_gt_full, n_eq_full = tallies(thr_full)
        thr_ref[...] = thr_full
        ngt_ref[...] = n_gt_full
        neq_ref[...] = n_eq_full

    thr = thr_ref[...]
    need = ksel - ngt_ref[...]
    excess = jnp.max(jnp.where((neq_ref[...] > need) & (thr > INT_MIN), 1, 0))

    @pl.when(excess > 0)
    def _():
        def y_step(t, y):
            cand = y + lax.shift_left(jnp.int32(1), nbits - 1 - t)
            g = count(lambda kv, kidx: jnp.where(kv == thr, jnp.where(kidx < cand, 1, 0), 0))
            return jnp.where(g < need, cand, y)

        y = lax.fori_loop(0, nbits, y_step, jnp.zeros((1, tq), I32))
        y = jnp.where(thr > INT_MIN, y, 2 ** 30)

        def demote(c, carry):
            kv = keys_ref[c]
            surplus = jnp.where(kv == thr, jnp.where(c * ck + krow > y, 1, 0), 0)
            keys_ref[c] = jnp.where(surplus > 0, thr - 1, kv)
            return carry

        lax.fori_loop(0, nc, demote, 0)

    thr_sel = jnp.maximum(thr, INT_MIN + 1)

    acc_ref[...] = jnp.zeros(acc_ref.shape, F32)

    def p3(c, carry):
        ms, ls = carry
        bias = jnp.where(keys_ref[c] >= thr_sel, 0.0, NEG_BIG)
        kk = k_ref[pl.ds(pl.multiple_of(c * ck, ck), ck), :]
        vt = vt_ref[0, c]
        m_out, l_out, ps, alphas = [], [], [], []
        for h in range(N_HEADS):
            s = _dot(kk, qt_ref[0, h]) + bias
            m_new = jnp.maximum(ms[h], jnp.max(s, axis=0, keepdims=True))
            alpha = jnp.exp2(ms[h] - m_new)
            p = jnp.exp2(s - m_new)
            m_out.append(m_new)
            l_out.append(alpha * ls[h] + jnp.sum(p, axis=0, keepdims=True))
            alphas.append(alpha)
            ps.append(p.astype(MXU_DT))
        for h in range(N_HEADS):
            acc_ref[h] = alphas[h] * acc_ref[h] + _dot(vt, ps[h])
        return tuple(m_out), tuple(l_out)

    init = (tuple(jnp.full((1, tq), NEG_BIG, F32) for _ in range(N_HEADS)),
            tuple(jnp.zeros((1, tq), F32) for _ in range(N_HEADS)))
    _, lrow = lax.fori_loop(0, nc, p3, init)

    lo_half = lax.broadcasted_iota(I32, (tq, LANES), 1) < HEAD_DIM
    for j in range(N_HEADS // 2):
        ea = (acc_ref[2 * j] / lrow[2 * j]).T
        eb = (acc_ref[2 * j + 1] / lrow[2 * j + 1]).T
        if (2 * j) // (N_HEADS // N_KV_HEADS) == 0:
            out2 = jnp.where(lo_half, ea, pltpu.roll(eb, HEAD_DIM, 1))
        else:
            out2 = jnp.where(lo_half, pltpu.roll(ea, HEAD_DIM, 1), eb)
        o_ref[:, j * LANES:(j + 1) * LANES] = out2


def _dsa_prompt_t(qt, qit, smt, smr, kn, vt, nseq, seq, tq, ck):
    dq = N_HEADS * HEAD_DIM
    dk = N_KV_HEADS * HEAD_DIM
    nq = seq // tq
    nck = seq // ck
    ksel = min(TOP_K, seq // 4)
    assert ck >= ksel and ck % LANES == 0 and seq % ck == 0 and seq % tq == 0 and vt.shape[3] == ck
    nbits = max(1, int(math.ceil(math.log2(seq))))
    kern = functools.partial(_dsa_prompt_t_kernel, tq=tq, ck=ck, ksel=ksel, nbits=nbits)
    return pl.pallas_call(
        kern,
        grid=(nseq, nq),
        in_specs=[pl.BlockSpec((1, N_HEADS, LANES, tq), lambda s, i: (s, 0, 0, i)),
                  pl.BlockSpec((1, N_IDX_HEADS, LANES, tq), lambda s, i: (s, 0, 0, i)),
                  pl.BlockSpec((1, LANES, tq), lambda s, i: (s, 0, i)),
                  pl.BlockSpec((seq, LANES), lambda s, i: (s, 0)),
                  pl.BlockSpec((seq, dk), lambda s, i: (s, 0)),
                  pl.BlockSpec((1, nck, dk, ck), lambda s, i: (s, 0, 0, 0))],
        out_specs=pl.BlockSpec((tq, dq), lambda s, i: (s * nq + i, 0)),
        out_shape=jax.ShapeDtypeStruct((nseq * seq, dq), F32),
        scratch_shapes=[pltpu.VMEM((nck, ck, tq), I32),
                        pltpu.VMEM((nck, ck, tq), BF16),
                        pltpu.VMEM((N_HEADS, dk, tq), F32),
                        pltpu.VMEM((1, tq), I32), pltpu.VMEM((1, tq), I32), pltpu.VMEM((1, tq), I32)],
        compiler_params=_cp(("arbitrary", "arbitrary")),
        name="dsa_prompt",
    )(qt, qit, smt, smr, kn, vt)


def _dsa_prompt_kernel(q_ref, qi_ref, smq_ref, smk_ref, k_ref, v_ref, o_ref,
                       keys_ref, qs_ref, qis_ref, m_ref, l_ref, acc_ref, thr_ref, y_ref,
                       *, tq, ck, ksel, nbits):
    i = pl.program_id(1)
    nc = (i * tq + tq + ck - 1) // ck
    ng = ck // LANES
    lane = lax.broadcasted_iota(I32, (tq, LANES), 1)
    lo = lane < HEAD_DIM
    rowg = i * tq + lax.broadcasted_iota(I32, (tq, LANES), 0)
    scale = HEAD_DIM ** -0.5

    for j in range(N_HEADS // 2):
        sl = slice(j * LANES, (j + 1) * LANES)
        qi2 = qi_ref[:, sl]
        qis_ref[2 * j] = jnp.where(lo, qi2, 0.0)
        qis_ref[2 * j + 1] = jnp.where(lo, pltpu.roll(qi2, HEAD_DIM, 1), 0.0)
        q2 = q_ref[:, sl] * scale
        q2r = pltpu.roll(q2, HEAD_DIM, 1)
        if (2 * j) // (N_HEADS // N_KV_HEADS) == 0:
            qs_ref[2 * j] = jnp.where(lo, q2, 0.0)
            qs_ref[2 * j + 1] = jnp.where(lo, q2r, 0.0)
        else:
            qs_ref[2 * j] = jnp.where(lo, 0.0, q2r)
            qs_ref[2 * j + 1] = jnp.where(lo, 0.0, q2)

    def p1(c, carry):
        kc = smk_ref[pl.ds(pl.multiple_of(c * ck, ck), ck), :]
        acc = jnp.zeros((tq, ck), F32)
        for h in range(N_IDX_HEADS):
            s = _nt_dot(qis_ref[h], kc)
            acc = acc + jnp.maximum(s, 0.0) * smq_ref[:, SM_WI + h:SM_WI + h + 1]
        sc = acc * IDX_SCALE
        for g in range(ng):
            colg = c * ck + g * LANES + lane
            scg = jnp.where(colg <= rowg, sc[:, g * LANES:(g + 1) * LANES], -jnp.inf)
            keys_ref[c, :, g * LANES:(g + 1) * LANES] = _sortable(scg)
        return carry

    lax.fori_loop(0, nc, p1, 0)

    def count(pred):
        def body(c, part):
            for g in range(ng):
                colg = c * ck + g * LANES + lane
                part = part + pred(keys_ref[c, :, g * LANES:(g + 1) * LANES], colg)
            return part

        part = lax.fori_loop(0, nc, body, jnp.zeros((tq, LANES), I32))
        return jnp.sum(part, axis=1, keepdims=True)

    def bit_step(t, thr):
        cand = thr + lax.shift_left(jnp.int32(1), 31 - t)
        cnt = count(lambda kv, colg: jnp.where(kv >= cand, 1, 0))
        return jnp.where(cnt >= ksel, cand, thr)

    thr = lax.fori_loop(0, 32, bit_step, jnp.full((tq, LANES), INT_MIN, I32))
    thr_ref[...] = thr

    n_gt = count(lambda kv, colg: jnp.where(kv > thr, 1, 0))
    n_eq = count(lambda kv, colg: jnp.where(kv == thr, 1, 0))
    need = ksel - n_gt
    y_ref[...] = jnp.full((tq, LANES), 2 ** 30, I32)
    excess = jnp.max(jnp.where(n_eq > need, 1, 0))

    @pl.when(excess > 0)
    def _():
        def y_step(t, y):
            cand = y + lax.shift_left(jnp.int32(1), nbits - 1 - t)
            g = count(lambda kv, colg: jnp.where(kv == thr, jnp.where(colg < cand, 1, 0), 0))
            return jnp.where(g < need, cand, y)

        y_ref[...] = lax.fori_loop(0, nbits, y_step, jnp.zeros((tq, LANES), I32))

    m_ref[...] = jnp.full(m_ref.shape, NEG_BIG, F32)
    l_ref[...] = jnp.zeros(l_ref.shape, F32)
    acc_ref[...] = jnp.zeros(acc_ref.shape, F32)

    def p3(c, carry):
        thr_v = thr_ref[...]
        y_v = y_ref[...]
        biases = []
        for g in range(ng):
            colg = c * ck + g * LANES + lane
            kv = keys_ref[c, :, g * LANES:(g + 1) * LANES]
            sel = jnp.where(kv > thr_v, 1, jnp.where(kv == thr_v, jnp.where(colg <= y_v, 1, 0), 0))
            sel = jnp.where(colg <= rowg, sel, 0)
            biases.append(jnp.where(sel > 0, 0.0, NEG_BIG))
        bias = jnp.concatenate(biases, axis=1)
        off = pl.multiple_of(c * ck, ck)
        kk = k_ref[pl.ds(off, ck), :]
        vv = v_ref[pl.ds(off, ck), :]
        for h in range(N_HEADS):
            s = _nt_dot(qs_ref[h], kk) + bias
            m_old = m_ref[h]
            m_new = jnp.maximum(m_old, jnp.max(s, axis=1, keepdims=True))
            alpha = jnp.exp(m_old - m_new)
            p = jnp.exp(s - m_new)
            l_ref[h] = alpha * l_ref[h] + jnp.sum(p, axis=1, keepdims=True)
            acc_ref[h] = alpha * acc_ref[h] + _dot(p, vv)
            m_ref[h] = m_new
        return carry

    lax.fori_loop(0, nc, p3, 0)

    for j in range(N_HEADS // 2):
        ea = acc_ref[2 * j] / l_ref[2 * j]
        eb = acc_ref[2 * j + 1] / l_ref[2 * j + 1]
        if (2 * j) // (N_HEADS // N_KV_HEADS) == 0:
            out2 = jnp.where(lo, ea, pltpu.roll(eb, HEAD_DIM, 1))
        else:
            out2 = jnp.where(lo, pltpu.roll(ea, HEAD_DIM, 1), eb)
        o_ref[:, j * LANES:(j + 1) * LANES] = out2


def _dsa_prompt(qn, qir, smr, kn, v, nseq, seq, tq, ck):
    dq = N_HEADS * HEAD_DIM
    dk = N_KV_HEADS * HEAD_DIM
    nq = seq // tq
    ksel = min(TOP_K, seq // 4)
    assert ck >= ksel and ck % LANES == 0 and seq % ck == 0 and seq % tq == 0
    nbits = max(1, int(math.ceil(math.log2(seq))))
    kern = functools.partial(_dsa_prompt_kernel, tq=tq, ck=ck, ksel=ksel, nbits=nbits)
    return pl.pallas_call(
        kern,
        grid=(nseq, nq),
        in_specs=[pl.BlockSpec((tq, dq), lambda s, i: (s * nq + i, 0)),
                  pl.BlockSpec((tq, dq), lambda s, i: (s * nq + i, 0)),
                  pl.BlockSpec((tq, LANES), lambda s, i: (s * nq + i, 0)),
                  pl.BlockSpec((seq, LANES), lambda s, i: (s, 0)),
                  pl.BlockSpec((seq, dk), lambda s, i: (s, 0)),
                  pl.BlockSpec((seq, dk), lambda s, i: (s, 0))],
        out_specs=pl.BlockSpec((tq, dq), lambda s, i: (s * nq + i, 0)),
        out_shape=jax.ShapeDtypeStruct((nseq * seq, dq), F32),
        scratch_shapes=[pltpu.VMEM((seq // ck, tq, ck), I32),
                        pltpu.VMEM((N_HEADS, tq, LANES), F32),
                        pltpu.VMEM((N_IDX_HEADS, tq, LANES), F32),
                        pltpu.VMEM((N_HEADS, tq, 1), F32),
                        pltpu.VMEM((N_HEADS, tq, 1), F32),
                        pltpu.VMEM((N_HEADS, tq, LANES), F32),
                        pltpu.VMEM((tq, LANES), I32),
                        pltpu.VMEM((tq, LANES), I32)],
        compiler_params=_cp(("arbitrary", "arbitrary")),
        name="dsa_prompt",
    )(qn, qir, smr, smr, kn, v)


def _dsa_sample_kernel(pt_ref, q8_ref, qi8_ref, wi_ref, kin_ref, kn_ref, vn_ref,
                       cki_hbm, ck_hbm, cv_hbm, o_ref, kib, kb, vb, sem,
                       *, n_pages, page, ksel, nbits):
    b = pl.program_id(0)
    past = n_pages * page

    def copies(p):
        pg = pt_ref[b, p]
        rows = pl.ds(pl.multiple_of(p * page, page), page)
        return (pltpu.make_async_copy(cki_hbm.at[pg], kib.at[rows, :], sem.at[0]),
                pltpu.make_async_copy(ck_hbm.at[pg], kb.at[rows, :], sem.at[1]),
                pltpu.make_async_copy(cv_hbm.at[pg], vb.at[rows, :], sem.at[2]))

    def start(p, carry):
        for cp in copies(p):
            cp.start()
        return carry

    def wait(p, carry):
        for cp in copies(p):
            cp.wait()
        return carry

    lax.fori_loop(0, n_pages, start, 0)
    lax.fori_loop(0, n_pages, wait, 0)

    wcol = wi_ref[0]
    qi8 = qi8_ref[0]
    s = _nt_dot(qi8, kib[...])
    sc = jnp.sum(jnp.maximum(s, 0.0) * wcol, axis=0, keepdims=True) * IDX_SCALE
    s_new = jnp.sum(qi8 * kin_ref[0], axis=1, keepdims=True)
    sc_new = jnp.sum(jnp.maximum(s_new, 0.0) * wcol, axis=0, keepdims=True) * IDX_SCALE
    keys = _sortable(sc)
    key_new = _sortable(sc_new)
    col = lax.broadcasted_iota(I32, (1, past), 1)

    def cnt(main, new):
        return jnp.sum(main, axis=1, keepdims=True) + new

    def bit_step(t, thr):
        cand = thr + lax.shift_left(jnp.int32(1), 31 - t)
        c = cnt(jnp.where(keys >= cand, 1, 0), jnp.where(key_new >= cand, 1, 0))
        return jnp.where(c >= ksel, cand, thr)

    thr = lax.fori_loop(0, 32, bit_step, jnp.full((1, 1), INT_MIN, I32))
    need = ksel - cnt(jnp.where(keys > thr, 1, 0), jnp.where(key_new > thr, 1, 0))

    def y_step(t, y):
        cand = y + lax.shift_left(jnp.int32(1), nbits - 1 - t)
        g = cnt(jnp.where(keys == thr, jnp.where(col < cand, 1, 0), 0),
                jnp.where(key_new == thr, jnp.where(past < cand, 1, 0), 0))
        return jnp.where(g < need, cand, y)

    y = lax.fori_loop(0, nbits, y_step, jnp.zeros((1, 1), I32))
    sel = jnp.where(keys > thr, 1, jnp.where(keys == thr, jnp.where(col <= y, 1, 0), 0))
    sel_new = jnp.where(key_new > thr, 1, jnp.where(key_new == thr, jnp.where(past <= y, 1, 0), 0))
    bias = jnp.where(sel > 0, 0.0, NEG_BIG)
    bias_new = jnp.where(sel_new > 0, 0.0, NEG_BIG)

    q8 = q8_ref[0] * (HEAD_DIM ** -0.5)
    sa = _nt_dot(q8, kb[...]) + bias
    sa_new = jnp.sum(q8 * kn_ref[0], axis=1, keepdims=True) + bias_new
    m = jnp.maximum(jnp.max(sa, axis=1, keepdims=True), sa_new)
    p = jnp.exp(sa - m)
    p_new = jnp.exp(sa_new - m)
    l = jnp.sum(p, axis=1, keepdims=True) + p_new
    o_ref[0] = (_dot(p, vb[...]) + p_new * vn_ref[0]) / l


def _dsa_sample(page_table, q8, qi8, wi, ki_new, k_new, v_new, cki, ck, cv):
    nb, n_pages = page_table.shape
    page = cki.shape[1]
    past = n_pages * page
    dk = N_KV_HEADS * HEAD_DIM
    ksel = min(TOP_K, (past + 1) // 4)
    nbits = int(math.floor(math.log2(past))) + 1
    kern = functools.partial(_dsa_sample_kernel, n_pages=n_pages, page=page, ksel=ksel, nbits=nbits)
    grid_spec = pltpu.PrefetchScalarGridSpec(
        num_scalar_prefetch=1,
        grid=(nb,),
        in_specs=[pl.BlockSpec((1, N_HEADS, dk), lambda b, pt: (b, 0, 0)),
                  pl.BlockSpec((1, N_IDX_HEADS, D_IDX), lambda b, pt: (b, 0, 0)),
                  pl.BlockSpec((1, N_IDX_HEADS, 1), lambda b, pt: (b, 0, 0)),
                  pl.BlockSpec((1, 1, D_IDX), lambda b, pt: (b, 0, 0)),
                  pl.BlockSpec((1, 1, dk), lambda b, pt: (b, 0, 0)),
                  pl.BlockSpec((1, 1, dk), lambda b, pt: (b, 0, 0)),
                  pl.BlockSpec(memory_space=pl.ANY),
                  pl.BlockSpec(memory_space=pl.ANY),
                  pl.BlockSpec(memory_space=pl.ANY)],
        out_specs=pl.BlockSpec((1, N_HEADS, dk), lambda b, pt: (b, 0, 0)),
        scratch_shapes=[pltpu.VMEM((past, D_IDX), F32),
                        pltpu.VMEM((past, dk), F32),
                        pltpu.VMEM((past, dk), F32),
                        pltpu.SemaphoreType.DMA((3,))],
    )
    return pl.pallas_call(
        kern,
        grid_spec=grid_spec,
        out_shape=jax.ShapeDtypeStruct((nb, N_HEADS, dk), F32),
        compiler_params=_cp(("arbitrary",)),
        name="dsa_sample",
    )(page_table, q8, qi8, wi, ki_new, k_new, v_new, cki, ck, cv)


def _page_copies(hbm, layer, pt_ref, b, buf, slot, sem, n_pages, page):
    return [pltpu.make_async_copy(hbm.at[layer, pt_ref[b, p]],
                                  buf.at[slot, :, p * page:(p + 1) * page], sem.at[slot])
            for p in range(n_pages)]


def _prefetch_pages(hbms, bufs, sems, layer, pt_ref, n_pages, page):
    b = pl.program_id(0)
    nb = pl.num_programs(0)
    slot = b % 2

    def start(bb, sl):
        for hbm, buf, sem in zip(hbms, bufs, sems):
            for cp in _page_copies(hbm, layer, pt_ref, bb, buf, sl, sem, n_pages, page):
                cp.start()

    @pl.when(b == 0)
    def _():
        start(0, 0)

    @pl.when(b + 1 < nb)
    def _():
        start(b + 1, 1 - slot)

    for hbm, buf, sem in zip(hbms, bufs, sems):
        for cp in _page_copies(hbm, layer, pt_ref, b, buf, slot, sem, n_pages, page):
            cp.wait()
    return slot


def _idx_sample_kernel(pt_ref, qi8_ref, wi_ref, kin_ref, cki_hbm, sc_ref, kibuf, sem,
                       *, layer, n_pages, page):
    past = n_pages * page
    slot = _prefetch_pages((cki_hbm,), (kibuf,), (sem,), layer, pt_ref, n_pages, page)
    wcol = wi_ref[0]
    qi8 = qi8_ref[0]
    s = _dot(qi8, kibuf[slot])
    sc_ref[0, :, 0:past] = jnp.sum(jnp.maximum(s, 0.0) * wcol, axis=0, keepdims=True) * IDX_SCALE
    s_new = jnp.sum(qi8 * kin_ref[0], axis=1, keepdims=True)
    sc_new = jnp.sum(jnp.maximum(s_new, 0.0) * wcol, axis=0, keepdims=True) * IDX_SCALE
    lane = lax.broadcasted_iota(I32, (1, LANES), 1)
    sc_ref[0, :, past:past + LANES] = jnp.where(lane == 0, sc_new, -jnp.inf)


def _idx_sample(page_table, qi8, wi, ki_new, cki_t, layer):
    nb, n_pages = page_table.shape
    page = cki_t.shape[3]
    past = n_pages * page
    kern = functools.partial(_idx_sample_kernel, layer=layer, n_pages=n_pages, page=page)
    grid_spec = pltpu.PrefetchScalarGridSpec(
        num_scalar_prefetch=1,
        grid=(nb,),
        in_specs=[pl.BlockSpec((1, N_IDX_HEADS, D_IDX), lambda b, pt: (b, 0, 0)),
                  pl.BlockSpec((1, N_IDX_HEADS, 1), lambda b, pt: (b, 0, 0)),
                  pl.BlockSpec((1, 1, D_IDX), lambda b, pt: (b, 0, 0)),
                  pl.BlockSpec(memory_space=pl.ANY)],
        out_specs=pl.BlockSpec((1, 1, past + LANES), lambda b, pt: (b, 0, 0)),
        scratch_shapes=[pltpu.VMEM((2, D_IDX, past), F32), pltpu.SemaphoreType.DMA((2,))],
    )
    return pl.pallas_call(
        kern, grid_spec=grid_spec,
        out_shape=jax.ShapeDtypeStruct((nb, 1, past + LANES), F32),
        compiler_params=_cp(("arbitrary",)),
        name="idx_sample",
    )(page_table, qi8, wi, ki_new, cki_t)


def _topk_bias_kernel(sc_ref, bias_ref, *, ksel, nbits):
    keys = _sortable(sc_ref[...])
    col = lax.broadcasted_iota(I32, keys.shape, 1)

    def cnt(hit):
        return jnp.sum(hit, axis=1, keepdims=True)

    def bit_step(t, thr):
        cand = thr + lax.shift_left(jnp.int32(1), 31 - t)
        return jnp.where(cnt(jnp.where(keys >= cand, 1, 0)) >= ksel, cand, thr)

    thr = lax.fori_loop(0, 32, bit_step, jnp.full((keys.shape[0], 1), INT_MIN, I32))
    need = ksel - cnt(jnp.where(keys > thr, 1, 0))

    def y_step(t, y):
        cand = y + lax.shift_left(jnp.int32(1), nbits - 1 - t)
        g = cnt(jnp.where(keys == thr, jnp.where(col < cand, 1, 0), 0))
        return jnp.where(g < need, cand, y)

    y = lax.fori_loop(0, nbits, y_step, jnp.zeros((keys.shape[0], 1), I32))
    sel = jnp.where(keys > thr, 1, jnp.where(keys == thr, jnp.where(col <= y, 1, 0), 0))
    bias_ref[...] = jnp.where(sel > 0, 0.0, NEG_BIG)


def _topk_bias(sc, ksel):
    nb, width = sc.shape
    nbits = int(math.floor(math.log2(width))) + 1
    kern = functools.partial(_topk_bias_kernel, ksel=ksel, nbits=nbits)
    return pl.pallas_call(
        kern, grid=(1,),
        in_specs=[pl.BlockSpec((nb, width), lambda i: (0, 0))],
        out_specs=pl.BlockSpec((nb, width), lambda i: (0, 0)),
        out_shape=jax.ShapeDtypeStruct((nb, width), F32),
        compiler_params=_cp(("arbitrary",)),
        name="topk_bias_sample",
    )(sc)


def _attn_sample_kernel(pt_ref, q8_ref, bias_ref, kn_ref, vn_ref, ck_hbm, cv_hbm, o_ref,
                        kbuf, vbuf, ksem, vsem, *, layer, n_pages, page):
    past = n_pages * page
    slot = _prefetch_pages((ck_hbm, cv_hbm), (kbuf, vbuf), (ksem, vsem), layer, pt_ref, n_pages, page)
    q8 = q8_ref[0] * (HEAD_DIM ** -0.5)
    sa = _dot(q8, kbuf[slot]) + bias_ref[0, :, 0:past]
    sa_new = jnp.sum(q8 * kn_ref[0], axis=1, keepdims=True) + bias_ref[0, :, past:past + 1]
    m = jnp.maximum(jnp.max(sa, axis=1, keepdims=True), sa_new)
    p = jnp.exp(sa - m)
    p_new = jnp.exp(sa_new - m)
    l = jnp.sum(p, axis=1, keepdims=True) + p_new
    o_ref[0] = (_nt_dot(p, vbuf[slot]) + p_new * vn_ref[0]) / l


def _attn_sample(page_table, q8, bias, k_new, v_new, ck_t, cv_t, layer):
    nb, n_pages = page_table.shape
    page = ck_t.shape[3]
    past = n_pages * page
    dk = N_KV_HEADS * HEAD_DIM
    kern = functools.partial(_attn_sample_kernel, layer=layer, n_pages=n_pages, page=page)
    grid_spec = pltpu.PrefetchScalarGridSpec(
        num_scalar_prefetch=1,
        grid=(nb,),
        in_specs=[pl.BlockSpec((1, N_HEADS, dk), lambda b, pt: (b, 0, 0)),
                  pl.BlockSpec((1, 1, past + LANES), lambda b, pt: (b, 0, 0)),
                  pl.BlockSpec((1, 1, dk), lambda b, pt: (b, 0, 0)),
                  pl.BlockSpec((1, 1, dk), lambda b, pt: (b, 0, 0)),
                  pl.BlockSpec(memory_space=pl.ANY),
                  pl.BlockSpec(memory_space=pl.ANY)],
        out_specs=pl.BlockSpec((1, N_HEADS, dk), lambda b, pt: (b, 0, 0)),
        scratch_shapes=[pltpu.VMEM((2, dk, past), F32), pltpu.VMEM((2, dk, past), F32),
                        pltpu.SemaphoreType.DMA((2,)), pltpu.SemaphoreType.DMA((2,))],
    )
    return pl.pallas_call(
        kern, grid_spec=grid_spec,
        out_shape=jax.ShapeDtypeStruct((nb, N_HEADS, dk), F32),
        compiler_params=_cp(("arbitrary",)),
        name="attn_sample",
    )(page_table, q8, bias, k_new, v_new, ck_t, cv_t)


def _ssd_prompt_kernel(xbc_ref, z_ref, sm_ref, cw_ref, cb_ref, dtb_ref, alog_ref, dskip_ref, ng_ref,
                       tril_ref, e64_ref, e128_ref, bmask_ref,
                       o_ref, cnew_ref, sst_ref, xbuf, st_ref, *, ts, kw, dxbc):
    halo = 8
    di = SSM_HEADS * SSM_HEAD_DIM
    dbc = SSM_GROUPS * D_STATE

    @pl.when(pl.program_id(1) == 0)
    def _():
        xbuf[0:halo, :] = jnp.zeros((halo, dxbc), F32)
        st_ref[...] = jnp.zeros(st_ref.shape, F32)

    xbuf[halo:halo + ts, :] = xbc_ref[...]
    first = halo - (kw - 1)
    conv = cb_ref[...] + cw_ref[0:1, :] * xbuf[first:first + ts, :]
    for j in range(1, kw):
        conv = conv + cw_ref[j:j + 1, :] * xbuf[first + j:first + j + ts, :]
    xc = _silu(conv)
    cnew_ref[0] = xbuf[ts + first:ts + halo, :]
    xbuf[0:halo, :] = xbuf[ts:ts + halo, :]

    lane = lax.broadcasted_iota(I32, (SSM_CHUNK, LANES), 1)
    head_lane = (lane[0:1, :] >= SM_DT) & (lane[0:1, :] < SM_DT + SSM_HEADS)
    a_row = jnp.where(head_lane, -jnp.exp(alog_ref[...]), 0.0)
    tri = lax.broadcasted_iota(I32, (SSM_CHUNK, SSM_CHUNK), 0) >= lax.broadcasted_iota(
        I32, (SSM_CHUNK, SSM_CHUNK), 1)
    glo = lane < D_STATE

    for k in range(ts // SSM_CHUNK):
        rows = slice(k * SSM_CHUNK, (k + 1) * SSM_CHUNK)
        dtf = _softplus(sm_ref[rows, :] + dtb_ref[...])
        adt = dtf * a_row
        a_cs = _sel_dot(tril_ref[...], adt)
        a_cs_t = a_cs.T
        acs_b = _dot_sel(a_cs, e128_ref[...])
        acs_f = _dot_sel(a_cs, e64_ref[...])
        dt_f = _dot_sel(dtf, e64_ref[...])
        alast_f = acs_f[SSM_CHUNK - 1:SSM_CHUNK, :]
        xs = xc[rows, 0:di]
        bm = xc[rows, di:di + dbc]
        cm = xc[rows, di + dbc:di + 2 * dbc]
        xdt = xs * dt_f
        xd = xdt * jnp.exp(alast_f - acs_f)
        bt = bm.T
        cb = (_dot(jnp.where(glo, cm, 0.0), bt), _dot(jnp.where(glo, 0.0, cm), bt))
        pairs = []
        for j in range(SSM_HEADS // 2):
            x2 = xdt[:, j * LANES:(j + 1) * LANES]
            acc = None
            for hh in range(2):
                h = 2 * j + hh
                seg = acs_b[:, h * LANES:(h + 1) * LANES] - a_cs_t[SM_DT + h:SM_DT + h + 1, :]
                lm = jnp.exp(jnp.where(tri, seg, -jnp.inf))
                sc = cb[h // (SSM_HEADS // SSM_GROUPS)] * lm
                xm = jnp.where(glo, x2, 0.0) if hh == 0 else jnp.where(glo, 0.0, x2)
                part = _dot(sc, xm)
                acc = part if acc is None else acc + part
            pairs.append(acc)
        y = jnp.concatenate(pairs, axis=1)
        y = y + _dot(cm, st_ref[...]) * jnp.exp(acs_f) + dskip_ref[...] * xs
        st_ref[...] = st_ref[...] * jnp.exp(alast_f) + bmask_ref[...] * _dot(bt, xd)
        yg = y * _silu(z_ref[rows, :])
        o_ref[rows, :] = _rms(yg) * ng_ref[...]
    sst_ref[0] = st_ref[...]


def _ssd_prompt(proj, cw, cb, dtb_row, alog_row, dskip_f, ng, consts, nseq, seq, ts):
    kw, dxbc = cw.shape
    di = SSM_HEADS * SSM_HEAD_DIM
    nt = seq // ts
    kern = functools.partial(_ssd_prompt_kernel, ts=ts, kw=kw, dxbc=dxbc)
    full = lambda a: pl.BlockSpec(a.shape, lambda s, i: (0,) * a.ndim)
    cs = (consts["tril"], consts["e64"], consts["e128"], consts["bmask"])
    return pl.pallas_call(
        kern,
        grid=(nseq, nt),
        in_specs=[pl.BlockSpec((ts, dxbc), lambda s, i: (s * nt + i, OFF["xbc"] // dxbc)),
                  pl.BlockSpec((ts, di), lambda s, i: (s * nt + i, OFF["z"] // di)),
                  pl.BlockSpec((ts, LANES), lambda s, i: (s * nt + i, OFF["small"] // LANES)),
                  full(cw), full(cb), full(dtb_row), full(alog_row), full(dskip_f), full(ng)]
                 + [full(c) for c in cs],
        out_specs=[pl.BlockSpec((ts, di), lambda s, i: (s * nt + i, 0)),
                   pl.BlockSpec((1, kw - 1, dxbc), lambda s, i: (s, 0, 0)),
                   pl.BlockSpec((1, SSM_GROUPS * D_STATE, di), lambda s, i: (s, 0, 0))],
        out_shape=[jax.ShapeDtypeStruct((nseq * seq, di), F32),
                   jax.ShapeDtypeStruct((nseq, kw - 1, dxbc), F32),
                   jax.ShapeDtypeStruct((nseq, SSM_GROUPS * D_STATE, di), F32)],
        scratch_shapes=[pltpu.VMEM((ts + 8, dxbc), F32),
                        pltpu.VMEM((SSM_GROUPS * D_STATE, di), F32)],
        compiler_params=_cp(("arbitrary", "arbitrary")),
        name="ssd_prompt",
    )(proj, proj, proj, cw, cb, dtb_row, alog_row, dskip_f, ng, *cs)


def _ssd_sample_kernel(xbc_ref, z_ref, sm_ref, st_ref, h0_ref, cw_ref, cb_ref, dtb_ref, alog_ref,
                       dskip_ref, ng_ref, e64_ref, o_ref, hn_ref, y_ref, *, nb, kw):
    di = SSM_HEADS * SSM_HEAD_DIM
    dbc = SSM_GROUPS * D_STATE
    conv = cb_ref[...] + cw_ref[kw - 1:kw, :] * xbc_ref[...]
    for j in range(kw - 1):
        conv = conv + cw_ref[j:j + 1, :] * st_ref[j]
    xc = _silu(conv)
    lane = lax.broadcasted_iota(I32, (1, LANES), 1)
    head_lane = (lane >= SM_DT) & (lane < SM_DT + SSM_HEADS)
    a_row = jnp.where(head_lane, -jnp.exp(alog_ref[...]), 0.0)
    dtf = _softplus(sm_ref[...] + dtb_ref[...])
    dec = jnp.exp(dtf * a_row)
    dt_f = _dot_sel(dtf, e64_ref[...])
    dec_f = _dot_sel(dec, e64_ref[...])
    xs = xc[:, 0:di]
    bm = xc[:, di:di + dbc]
    cm = xc[:, di + dbc:di + 2 * dbc]
    pad = jnp.zeros((LANES - nb, di), F32)
    xdt_t = jnp.concatenate([xs * dt_f, pad], axis=0).T
    dec_t = jnp.concatenate([dec_f, pad], axis=0).T
    bm_r = pltpu.roll(bm, D_STATE, 1)
    cm_r = pltpu.roll(cm, D_STATE, 1)
    rowi = lax.broadcasted_iota(I32, (di, D_STATE), 0)
    g0 = rowi < (SSM_HEADS // SSM_GROUPS) * SSM_HEAD_DIM
    lane_y = lax.broadcasted_iota(I32, (1, di), 1) < (SSM_HEADS // SSM_GROUPS) * SSM_HEAD_DIM
    row8 = lax.broadcasted_iota(I32, (8, D_STATE), 0)
    for b in range(nb):
        bsel = jnp.where(g0, bm[b:b + 1, 0:D_STATE], bm_r[b:b + 1, 0:D_STATE])
        hn = h0_ref[b] * dec_t[:, b:b + 1] + xdt_t[:, b:b + 1] * bsel
        hn_ref[b] = hn
        c2 = jnp.where(row8 == 0, cm[b:b + 1, 0:D_STATE],
                       jnp.where(row8 == 1, cm_r[b:b + 1, 0:D_STATE], 0.0))
        yr = _nt_dot(c2, hn)
        y_ref[b:b + 1, :] = jnp.where(lane_y, yr[0:1, :], yr[1:2, :])
    y = y_ref[...] + dskip_ref[...] * xs
    yg = y * _silu(z_ref[...])
    o_ref[...] = _rms(yg) * ng_ref[...]


def _ssd_sample(proj, st_t, h0, cw, cb, dtb_row, alog_row, dskip_f, ng, consts):
    kw, dxbc = cw.shape
    nb = proj.shape[0]
    di = SSM_HEADS * SSM_HEAD_DIM
    kern = functools.partial(_ssd_sample_kernel, nb=nb, kw=kw)
    full = lambda a: pl.BlockSpec(a.shape, lambda i: (0,) * a.ndim)
    return pl.pallas_call(
        kern,
        grid=(1,),
        in_specs=[pl.BlockSpec((nb, dxbc), lambda i: (0, OFF["xbc"] // dxbc)),
                  pl.BlockSpec((nb, di), lambda i: (0, OFF["z"] // di)),
                  pl.BlockSpec((nb, LANES), lambda i: (0, OFF["small"] // LANES)),
                  full(st_t), full(h0), full(cw), full(cb), full(dtb_row), full(alog_row),
                  full(dskip_f), full(ng), full(consts["e64"])],
        out_specs=[pl.BlockSpec((nb, di), lambda i: (0, 0)),
                   pl.BlockSpec(h0.shape, lambda i: (0, 0, 0))],
        out_shape=[jax.ShapeDtypeStruct((nb, di), F32), jax.ShapeDtypeStruct(h0.shape, F32)],
        scratch_shapes=[pltpu.VMEM((nb, di), F32)],
        compiler_params=_cp(("arbitrary",)),
        name="ssd_sample",
    )(proj, proj, proj, st_t, h0, cw, cb, dtb_row, alog_row, dskip_f, ng, consts["e64"])


def _mem_kv_kernel(x_ref, g_ref, w_ref, kg_ref, mk_ref, mv_ref):
    dm = mk_ref.shape[1]
    m = _dot(_rms(x_ref[...]) * g_ref[...], w_ref[...])
    for h in range(MEM_HEADS):
        sl = slice(h * MEM_HEAD_DIM, (h + 1) * MEM_HEAD_DIM)
        mk_ref[:, sl] = _rms(m[:, sl]) * kg_ref[...]
    mv_ref[...] = m[:, dm:]


def _mem_kv(mem2d, g, w, kg, rows):
    t, d = mem2d.shape
    dm = MEM_HEADS * MEM_HEAD_DIM
    return pl.pallas_call(
        _mem_kv_kernel,
        grid=(t // rows,),
        in_specs=[pl.BlockSpec((rows, d), lambda i: (i, 0)),
                  pl.BlockSpec((1, d), lambda i: (0, 0)),
                  pl.BlockSpec((d, 2 * dm), lambda i: (0, 0)),
                  pl.BlockSpec((1, MEM_HEAD_DIM), lambda i: (0, 0))],
        out_specs=[pl.BlockSpec((rows, dm), lambda i: (i, 0)),
                   pl.BlockSpec((rows, dm), lambda i: (i, 0))],
        out_shape=[jax.ShapeDtypeStruct((t, dm), F32), jax.ShapeDtypeStruct((t, dm), F32)],
        compiler_params=_cp(("parallel",)),
        name="mem_kv",
    )(mem2d, g, w, kg)


def _mem_attend_kernel(mq_ref, mk_ref, mv_ref, g_ref, o_ref, *, tm):
    mq = mq_ref[0]
    rows = max(tm, 8)
    if tm < rows:
        mq = jnp.broadcast_to(mq, (rows, mq.shape[1]))
    for h in range(MEM_HEADS):
        sl = slice(h * MEM_HEAD_DIM, (h + 1) * MEM_HEAD_DIM)
        qn = _rms(mq[:, sl]) * g_ref[...]
        s = _nt_dot(qn, mk_ref[0, :, sl]) * (MEM_HEAD_DIM ** -0.5)
        p = jnp.exp(s - jnp.max(s, axis=1, keepdims=True))
        o = _dot(p, mv_ref[0, :, sl]) / jnp.sum(p, axis=1, keepdims=True)
        o_ref[0, :, sl] = o[0:tm, :]


def _mem_attend(proj3, mk3, mv3, g, tm):
    nseq, seq, _ = proj3.shape
    nm = mk3.shape[1]
    dm = MEM_HEADS * MEM_HEAD_DIM
    kern = functools.partial(_mem_attend_kernel, tm=tm)
    return pl.pallas_call(
        kern,
        grid=(nseq, seq // tm),
        in_specs=[pl.BlockSpec((1, tm, dm), lambda s, i: (s, i, OFF["mq"] // dm)),
                  pl.BlockSpec((1, nm, dm), lambda s, i: (s, 0, 0)),
                  pl.BlockSpec((1, nm, dm), lambda s, i: (s, 0, 0)),
                  pl.BlockSpec((1, MEM_HEAD_DIM), lambda s, i: (0, 0))],
        out_specs=pl.BlockSpec((1, tm, dm), lambda s, i: (s, i, 0)),
        out_shape=jax.ShapeDtypeStruct((nseq, seq, dm), F32),
        compiler_params=_cp(("parallel", "arbitrary")),
        name="mem_attend",
    )(proj3, mk3, mv3, g)


def _merge_kernel(x_ref, ba_ref, bb_ref, bc_ref, bm_ref, gt_ref, wb_ref, wo_ref, o_ref):
    d = x_ref.shape[1]
    acc = None
    for n, br in enumerate((ba_ref, bb_ref, bc_ref, bm_ref)):
        term = jax.nn.sigmoid(gt_ref[:, n * d:(n + 1) * d]) * _dot(br[...].astype(MXU_DT), wb_ref[n])
        acc = term if acc is None else acc + term
    o_ref[...] = x_ref[...] + _dot(acc.astype(MXU_DT), wo_ref[...])


def _merge(x, proj, brs, wb, wo, tm):
    t, d = x.shape
    nbr, bw, _ = wb.shape
    return pl.pallas_call(
        _merge_kernel,
        grid=(t // tm,),
        in_specs=[pl.BlockSpec((tm, d), lambda i: (i, 0))]
                 + [pl.BlockSpec((tm, bw), lambda i: (i, 0))] * nbr
                 + [pl.BlockSpec((tm, nbr * d), lambda i: (i, OFF["gates"] // (nbr * d))),
                    pl.BlockSpec((nbr, bw, d), lambda i: (0, 0, 0)),
                    pl.BlockSpec((d, d), lambda i: (0, 0))],
        out_specs=pl.BlockSpec((tm, d), lambda i: (i, 0)),
        out_shape=jax.ShapeDtypeStruct((t, d), F32),
        compiler_params=_cp(("parallel",)),
        name="merge",
    )(x, *brs, proj, wb, wo)


def _ffn_prompt_kernel(x_ref, g_ref, wg_ref, wu_ref, cwg_ref, cwu_ref, cbg_ref, cbu_ref, wd_ref,
                       o_ref, unew_ref, h_ref, acc_ref, ubuf, carry, *, tm, tc, kw, nff):
    i = pl.program_id(1)
    c = pl.program_id(2)
    halo = 8
    first = halo - (kw - 1)

    @pl.when(c == 0)
    def _():
        h_ref[...] = (_rms(x_ref[...]) * g_ref[...]).astype(h_ref.dtype)
        acc_ref[...] = jnp.zeros(acc_ref.shape, F32)

    @pl.when(i == 0)
    def _():
        carry[c] = jnp.zeros((halo, 2 * tc), F32)

    h = h_ref[...]
    ubuf[0:halo, :] = carry[c]
    ubuf[halo:halo + tm, 0:tc] = _dot(h, wg_ref[...])
    ubuf[halo:halo + tm, tc:2 * tc] = _dot(h, wu_ref[...])
    carry[c] = ubuf[tm:tm + halo, :]
    fg =cbg_ref[...] + cwg_ref[0:1, :] * ubuf[first:first + tm, 0:tc]
    fu = cbu_ref[...] + cwu_ref[0:1, :] * ubuf[first:first + tm, tc:2 * tc]
    for j in range(1, kw):
        fg = fg + cwg_ref[j:j + 1, :] * ubuf[first + j:first + j + tm, 0:tc]
        fu = fu + cwu_ref[j:j + 1, :] * ubuf[first + j:first + j + tm, tc:2 * tc]
    acc_ref[...] += _dot((_silu(fg) * fu).astype(MXU_DT), wd_ref[...])

    @pl.when(c == nff - 1)
    def _():
        o_ref[...] = x_ref[...] + acc_ref[...]

    @pl.when((c == nff - 1) & (i == pl.num_programs(1) - 1))
    def _():
        for cc in range(nff):
            unew_ref[0, :, cc * tc:(cc + 1) * tc] = carry[cc, first:halo, 0:tc]
            unew_ref[0, :, (nff + cc) * tc:(nff + cc + 1) * tc] = carry[cc, first:halo, tc:2 * tc]


def _ffn_prompt(x, g, wup, cw, cb, wd, nseq, seq, tm, tc):
    t, d = x.shape
    dff = wd.shape[0]
    kw = cw.shape[0]
    nt = seq // tm
    nff = dff // tc
    kern = functools.partial(_ffn_prompt_kernel, tm=tm, tc=tc, kw=kw, nff=nff)
    return pl.pallas_call(
        kern,
        grid=(nseq, nt, nff),
        in_specs=[pl.BlockSpec((tm, d), lambda s, i, c: (s * nt + i, 0)),
                  pl.BlockSpec((1, d), lambda s, i, c: (0, 0)),
                  pl.BlockSpec((d, tc), lambda s, i, c: (0, c)),
                  pl.BlockSpec((d, tc), lambda s, i, c: (0, nff + c)),
                  pl.BlockSpec((kw, tc), lambda s, i, c: (0, c)),
                  pl.BlockSpec((kw, tc), lambda s, i, c: (0, nff + c)),
                  pl.BlockSpec((1, tc), lambda s, i, c: (0, c)),
                  pl.BlockSpec((1, tc), lambda s, i, c: (0, nff + c)),
                  pl.BlockSpec((tc, d), lambda s, i, c: (c, 0))],
        out_specs=[pl.BlockSpec((tm, d), lambda s, i, c: (s * nt + i, 0)),
                   pl.BlockSpec((1, kw - 1, 2 * dff), lambda s, i, c: (s, 0, 0))],
        out_shape=[jax.ShapeDtypeStruct((t, d), F32),
                   jax.ShapeDtypeStruct((nseq, kw - 1, 2 * dff), F32)],
        scratch_shapes=[pltpu.VMEM((tm, d), MXU_DT), pltpu.VMEM((tm, d), F32),
                        pltpu.VMEM((tm + 8, 2 * tc), F32), pltpu.VMEM((nff, 8, 2 * tc), F32)],
        compiler_params=_cp(("arbitrary", "arbitrary", "arbitrary")),
        name="ffn_prompt",
    )(x, g, wup, wup, cw, cw, cb, cb, wd)


def _ffn_sample_kernel(x_ref, g_ref, wg_ref, wu_ref, stg_ref, stu_ref, cwg_ref, cwu_ref, cbg_ref,
                       cbu_ref, wd_ref, o_ref, ug_ref, uu_ref, h_ref, acc_ref, *, kw):
    c = pl.program_id(0)

    @pl.when(c == 0)
    def _():
        h_ref[...] = (_rms(x_ref[...]) * g_ref[...]).astype(h_ref.dtype)
        acc_ref[...] = jnp.zeros(acc_ref.shape, F32)

    h = h_ref[...]
    ug = _dot(h, wg_ref[...])
    uu = _dot(h, wu_ref[...])
    ug_ref[...] = ug
    uu_ref[...] = uu
    fg = cbg_ref[...] + cwg_ref[kw - 1:kw, :] * ug
    fu = cbu_ref[...] + cwu_ref[kw - 1:kw, :] * uu
    for j in range(kw - 1):
        fg = fg + cwg_ref[j:j + 1, :] * stg_ref[j]
        fu = fu + cwu_ref[j:j + 1, :] * stu_ref[j]
    acc_ref[...] += _dot((_silu(fg) * fu).astype(MXU_DT), wd_ref[...])

    @pl.when(c == pl.num_programs(0) - 1)
    def _():
        o_ref[...] = x_ref[...] + acc_ref[...]


def _ffn_sample(x, g, wup, st_t, cw, cb, wd, tc):
    t, d = x.shape
    dff = wd.shape[0]
    kw = cw.shape[0]
    nff = dff // tc
    kern = functools.partial(_ffn_sample_kernel, kw=kw)
    return pl.pallas_call(
        kern,
        grid=(nff,),
        in_specs=[pl.BlockSpec((t, d), lambda c: (0, 0)),
                  pl.BlockSpec((1, d), lambda c: (0, 0)),
                  pl.BlockSpec((d, tc), lambda c: (0, c)),
                  pl.BlockSpec((d, tc), lambda c: (0, nff + c)),
                  pl.BlockSpec((kw - 1, t, tc), lambda c: (0, 0, c)),
                  pl.BlockSpec((kw - 1, t, tc), lambda c: (0, 0, nff + c)),
                  pl.BlockSpec((kw, tc), lambda c: (0, c)),
                  pl.BlockSpec((kw, tc), lambda c: (0, nff + c)),
                  pl.BlockSpec((1, tc), lambda c: (0, c)),
                  pl.BlockSpec((1, tc), lambda c: (0, nff + c)),
                  pl.BlockSpec((tc, d), lambda c: (c, 0))],
        out_specs=[pl.BlockSpec((t, d), lambda c: (0, 0)),
                   pl.BlockSpec((t, tc), lambda c: (0, c)),
                   pl.BlockSpec((t, tc), lambda c: (0, c))],
        out_shape=[jax.ShapeDtypeStruct((t, d), F32), jax.ShapeDtypeStruct((t, dff), F32),
                   jax.ShapeDtypeStruct((t, dff), F32)],
        scratch_shapes=[pltpu.VMEM((t, d), MXU_DT), pltpu.VMEM((t, d), F32)],
        compiler_params=_cp(("arbitrary",)),
        name="ffn_sample",
    )(x, g, wup, wup, st_t, st_t, cw, cw, cb, cb, wd)


def _constants():
    lane = np.arange(LANES)
    r = lane % HEAD_DIM
    rot = HEAD_DIM // 4
    half = rot // 2
    inv_freq = ROPE_THETA ** (-jnp.arange(half, dtype=F32) * (2.0 / rot))
    in_rot = r < rot
    invf = jnp.where(jnp.asarray(in_rot), inv_freq[jnp.asarray(r % half)], 0.0).astype(F32)[None, :]
    sgn = np.where(r < half, -1.0, np.where(in_rot, 1.0, 0.0)).astype(np.float32)[None, :]
    first_head = (lane < HEAD_DIM)[None, :]
    dq = N_HEADS * HEAD_DIM
    bd = (np.arange(dq)[:, None] // HEAD_DIM == np.arange(dq)[None, :] // HEAD_DIM)
    di = SSM_HEADS * SSM_HEAD_DIM
    e64 = np.zeros((LANES, di), np.float32)
    e128 = np.zeros((LANES, SSM_HEADS * LANES), np.float32)
    for h in range(SSM_HEADS):
        e64[SM_DT + h, h * SSM_HEAD_DIM:(h + 1) * SSM_HEAD_DIM] = 1.0
        e128[SM_DT + h, h * LANES:(h + 1) * LANES] = 1.0
    tril = np.tril(np.ones((SSM_CHUNK, SSM_CHUNK), np.float32))
    hpg = SSM_HEADS // SSM_GROUPS
    bmask = (np.arange(SSM_GROUPS * D_STATE)[:, None] // D_STATE
             == np.arange(di)[None, :] // (hpg * SSM_HEAD_DIM)).astype(np.float32)
    return dict(
        invf=invf, sgn=jnp.asarray(sgn),
        invf_s=jnp.where(jnp.asarray(first_head), invf, 0.0),
        sgn_s=jnp.asarray(np.where(first_head, sgn, 0.0).astype(np.float32)),
        bd=jnp.asarray(bd.astype(np.float32), dtype=BF16),
        e64=jnp.asarray(e64, dtype=BF16), e128=jnp.asarray(e128, dtype=BF16),
        tril=jnp.asarray(tril, dtype=BF16), bmask=jnp.asarray(bmask))


def _reorder_w_in(w_in):
    d = w_in.shape[0]
    sizes = dict(glu=1024, q=512, k=128, v=128, qi=512, ki=64, wi=8, z=512, xbc=768, dt=8, mq=512,
                 gates=4096)
    order_in = ["glu", "q", "k", "v", "qi", "ki", "wi", "z", "xbc", "dt", "mq", "gates"]
    parts, off = {}, 0
    for name in order_in:
        parts[name] = w_in[:, off:off + sizes[name]]
        off += sizes[name]
    assert off == w_in.shape[1]
    pad = jnp.zeros((d, LANES - sizes["ki"] - sizes["wi"] - sizes["dt"]), w_in.dtype)
    out = jnp.concatenate([parts[n] for n in ("glu", "q", "qi", "z", "mq", "xbc", "k", "v", "gates",
                                              "ki", "wi", "dt")] + [pad], axis=1)
    assert out.shape[1] == PROJ_COLS
    return out


def _pad_lanes(v, start):
    return jnp.zeros((1, LANES), F32).at[0, start:start + v.shape[0]].set(v)


def _tile(n, cap):
    return min(n, cap)


def _layer_params(l, prm, consts):
    p = dict(
        w_in=_reorder_w_in(prm["w_in"][l]).astype(MXU_DT),
        norm_mix_g=prm["norm_mix_g"][l][None, :],
        conv_a_w=prm["conv_a_w"][l], conv_a_b=prm["conv_a_b"][l][None, :],
        ln_a_g=prm["ln_a_g"][l][None, :], ln_a_b=prm["ln_a_b"][l][None, :],
        qg=jnp.tile(prm["q_norm_g"][l], N_HEADS)[None, :],
        kg=jnp.tile(prm["k_norm_g"][l], N_KV_HEADS)[None, :],
        ssm_conv_w=prm["ssm_conv_w"][l], ssm_conv_b=prm["ssm_conv_b"][l][None, :],
        dtb_row=_pad_lanes(prm["dt_bias"][l], SM_DT), alog_row=_pad_lanes(prm["a_log"][l], SM_DT),
        dskip_f=jnp.repeat(prm["d_skip"][l], SSM_HEAD_DIM)[None, :],
        ssm_norm_g=prm["ssm_norm_g"][l][None, :],
        mem_norm_g=prm["mem_norm_g"][l][None, :], w_mem_kv=prm["w_mem_kv"][l],
        mq_norm_g=prm["mq_norm_g"][l][None, :], mk_norm_g=prm["mk_norm_g"][l][None, :],
        w_branch=prm["w_branch"][l].astype(MXU_DT), w_out=prm["w_out"][l].astype(MXU_DT),
        norm_ffn_g=prm["norm_ffn_g"][l][None, :], w_ffn_up=prm["w_ffn_up"][l].astype(MXU_DT),
        ffn_conv_w=prm["ffn_conv_w"][l], ffn_conv_b=prm["ffn_conv_b"][l][None, :],
        w_ffn_down=prm["w_ffn_down"][l].astype(MXU_DT))
    return p


def _ssm_state_from_slab(slab):
    nseq = slab.shape[0]
    hpg = SSM_HEADS // SSM_GROUPS
    s = slab.reshape(nseq, SSM_GROUPS, D_STATE, SSM_HEADS, SSM_HEAD_DIM)
    per_head = [s[:, h // hpg, :, h, :] for h in range(SSM_HEADS)]
    return jnp.swapaxes(jnp.stack(per_head, axis=1), 2, 3)


def _prompt_layer(x, p, consts, mem2d, nseq, seq):
    t, d = x.shape
    nm = mem2d.shape[0] // nseq
    proj = _norm_proj(x, p["norm_mix_g"], p["w_in"], _tile(t, 1024), PROJ_COLS // 5)
    br_a, conf_new = _conf_prompt(proj, p["conv_a_w"], p["conv_a_b"], p["ln_a_g"], p["ln_a_b"],
                                  nseq, seq, _tile(seq, 512))
    ck = _tile(seq, 512)
    kn, v, kir, knb, smb, qt, qit, vt, smt = _dsa_prep_t(proj, p["qg"], p["kg"], consts, nseq, seq, ck)
    br_b = _dsa_prompt_t(qt, qit, smt, smb, knb, vt, nseq, seq, _tile(seq, 256), ck)
    br_c, sconv_new, sslab = _ssd_prompt(proj, p["ssm_conv_w"], p["ssm_conv_b"], p["dtb_row"],
                                         p["alog_row"], p["dskip_f"], p["ssm_norm_g"], consts,
                                         nseq, seq, _tile(seq, 512))
    mk, mv = _mem_kv(mem2d, p["mem_norm_g"], p["w_mem_kv"], p["mk_norm_g"], nm)
    dm = MEM_HEADS * MEM_HEAD_DIM
    br_m = _mem_attend(proj.reshape(nseq, seq, PROJ_COLS), mk.reshape(nseq, nm, dm),
                       mv.reshape(nseq, nm, dm), p["mq_norm_g"], _tile(seq, 512)).reshape(t, dm)
    x = _merge(x, proj, (br_a, br_b, br_c, br_m), p["w_branch"], p["w_out"], _tile(t, 256))
    x, ffn_new = _ffn_prompt(x, p["norm_ffn_g"], p["w_ffn_up"], p["ffn_conv_w"], p["ffn_conv_b"],
                             p["w_ffn_down"], nseq, seq, _tile(seq, 512), p["w_ffn_down"].shape[0] // 2)
    state = (kn.reshape(nseq, seq, N_KV_HEADS, HEAD_DIM), v.reshape(nseq, seq, N_KV_HEADS, HEAD_DIM),
             kir.reshape(nseq, seq, D_IDX),
             mk.reshape(nseq, nm, MEM_HEADS, MEM_HEAD_DIM), mv.reshape(nseq, nm, MEM_HEADS, MEM_HEAD_DIM),
             conf_new, sconv_new, _ssm_state_from_slab(sslab), ffn_new)
    return x, state


def _sample_layer(x, p, consts, layer, page_table, ck_t, cv_t, cki_t, cmk, cmv, st_conf, st_sconv, st_ssm,
                  st_ffn):
    nb, d = x.shape
    n_pages = page_table.shape[1]
    page = ck_t.shape[3]
    past = n_pages * page
    dq = N_HEADS * HEAD_DIM
    dk = N_KV_HEADS * HEAD_DIM
    proj = _norm_proj(x, p["norm_mix_g"], p["w_in"], nb, PROJ_COLS // 5)
    br_a, a_new = _conf_sample(proj, jnp.swapaxes(st_conf, 0, 1), p["conv_a_w"], p["conv_a_b"],
                               p["ln_a_g"], p["ln_a_b"])
    qn, kn, v, qir, smr, kir = _dsa_prep(proj, p["qg"], p["kg"], consts, nb, 1, float(past))
    hpk = N_HEADS // N_KV_HEADS
    qh = qn.reshape(nb, N_HEADS, HEAD_DIM)
    grp = (np.arange(N_HEADS)[:, None] // hpk == np.arange(dk)[None, :] // HEAD_DIM)
    q8 = jnp.where(jnp.asarray(grp)[None], jnp.tile(qh, (1, 1, N_KV_HEADS)), 0.0)
    sc = _idx_sample(page_table, qir.reshape(nb, N_IDX_HEADS, D_IDX),
                     smr[:, SM_WI:SM_WI + N_IDX_HEADS].reshape(nb, N_IDX_HEADS, 1),
                     kir.reshape(nb, 1, D_IDX), cki_t, layer)
    bias = _topk_bias(sc.reshape(nb, past + LANES), min(TOP_K, (past + 1) // 4))
    o8 = _attn_sample(page_table, q8, bias.reshape(nb, 1, past + LANES), kn.reshape(nb, 1, dk),
                      v.reshape(nb, 1, dk), ck_t, cv_t, layer)
    o8 = o8.reshape(nb, N_HEADS, N_KV_HEADS, HEAD_DIM)
    br_b = jnp.stack([o8[:, h, h // hpk, :] for h in range(N_HEADS)], axis=1).reshape(nb, dq)
    h0 = st_ssm.reshape(nb, SSM_HEADS * SSM_HEAD_DIM, D_STATE)
    br_c, h_new = _ssd_sample(proj, jnp.swapaxes(st_sconv, 0, 1), h0, p["ssm_conv_w"], p["ssm_conv_b"],
                              p["dtb_row"], p["alog_row"], p["dskip_f"], p["ssm_norm_g"], consts)
    nm = cmk.shape[1]
    dm = MEM_HEADS * MEM_HEAD_DIM
    br_m = _mem_attend(proj.reshape(nb, 1, PROJ_COLS), cmk.reshape(nb, nm, dm), cmv.reshape(nb, nm, dm),
                       p["mq_norm_g"], 1).reshape(nb, dm)
    x = _merge(x, proj, (br_a, br_b, br_c, br_m), p["w_branch"], p["w_out"], nb)
    x, ug, uu = _ffn_sample(x, p["norm_ffn_g"], p["w_ffn_up"], jnp.swapaxes(st_ffn, 0, 1),
                            p["ffn_conv_w"], p["ffn_conv_b"], p["w_ffn_down"], 256)
    xbc_raw = proj[:, OFF["xbc"]:OFF["xbc"] + st_sconv.shape[-1]]
    state = (kn.reshape(nb, 1, N_KV_HEADS, HEAD_DIM), v.reshape(nb, 1, N_KV_HEADS, HEAD_DIM),
             kir.reshape(nb, 1, D_IDX),
             jnp.concatenate([st_conf[:, 1:], a_new[:, None]], axis=1),
             jnp.concatenate([st_sconv[:, 1:], xbc_raw[:, None]], axis=1),
             h_new.reshape(st_ssm.shape),
             jnp.concatenate([st_ffn[:, 1:], jnp.concatenate([ug, uu], axis=-1)[:, None]], axis=1))
    return x, state


def kernel(x_prompt, x_sample, cache_k, cache_v, cache_kidx, cache_mem_k, cache_mem_v, state_conformer, state_ssm_conv, state_ssm, state_ffn_conv, page_table, mem_prompt, norm_mix_g, w_in, conv_a_w, conv_a_b, ln_a_g, ln_a_b, q_norm_g, k_norm_g, ssm_conv_w, ssm_conv_b, dt_bias, a_log, d_skip, ssm_norm_g, mem_norm_g, w_mem_kv, mq_norm_g, mk_norm_g, w_branch, w_out, norm_ffn_g, w_ffn_up, ffn_conv_w, ffn_conv_b, w_ffn_down):
    prm = dict(norm_mix_g=norm_mix_g, w_in=w_in, conv_a_w=conv_a_w, conv_a_b=conv_a_b, ln_a_g=ln_a_g,
               ln_a_b=ln_a_b, q_norm_g=q_norm_g, k_norm_g=k_norm_g, ssm_conv_w=ssm_conv_w,
               ssm_conv_b=ssm_conv_b, dt_bias=dt_bias, a_log=a_log, d_skip=d_skip, ssm_norm_g=ssm_norm_g,
               mem_norm_g=mem_norm_g, w_mem_kv=w_mem_kv, mq_norm_g=mq_norm_g, mk_norm_g=mk_norm_g,
               w_branch=w_branch, w_out=w_out, norm_ffn_g=norm_ffn_g, w_ffn_up=w_ffn_up,
               ffn_conv_w=ffn_conv_w, ffn_conv_b=ffn_conv_b, w_ffn_down=w_ffn_down)
    depth = w_in.shape[0]
    nseq, seq, d = x_prompt.shape
    nb, dseq, _ = x_sample.shape
    assert dseq == 1
    consts = _constants()
    xp = x_prompt.reshape(nseq * seq, d)
    xs = x_sample.reshape(nb, d)
    mem2d = mem_prompt.reshape(nseq * mem_prompt.shape[1], d)
    n_phys, page = cache_k.shape[1], cache_k.shape[2]
    ck_t = jnp.transpose(cache_k, (0, 1, 3, 4, 2)).reshape(depth, n_phys, N_KV_HEADS * HEAD_DIM, page)
    cv_t = jnp.transpose(cache_v, (0, 1, 3, 4, 2)).reshape(depth, n_phys, N_KV_HEADS * HEAD_DIM, page)
    cki_t = jnp.transpose(cache_kidx, (0, 1, 3, 2))
    p_states, s_states = [], []
    for l in range(depth):
        p = _layer_params(l, prm, consts)
        xp, st = _prompt_layer(xp, p, consts, mem2d, nseq, seq)
        p_states.append(st)
        xs, st = _sample_layer(xs, p, consts, l, page_table, ck_t, cv_t, cki_t,
                               cache_mem_k[l], cache_mem_v[l], state_conformer[l], state_ssm_conv[l],
                               state_ssm[l], state_ffn_conv[l])
        s_states.append(st)
    stack = lambda states, k: jnp.stack([s[k] for s in states])
    return ((xp.reshape(nseq, seq, d), xs.reshape(nb, 1, d))
            + tuple(stack(p_states, k) for k in range(9))
            + tuple(stack(s_states, k) for k in range(7)))
```

```python
import functools
import math

import numpy as np
import jax
import jax.numpy as jnp
from jax import lax
from jax.experimental import pallas as pl
from jax.experimental.pallas import tpu as pltpu

F32 = jnp.float32
BF16 = jnp.bfloat16
I32 = jnp.int32
MXU_DT = BF16

EPS = 1e-6
ROPE_THETA = 500000.0
LANES = 128
V7X_VMEM_LIMIT = 56 * 1024 * 1024

N_HEADS = 8
HEAD_DIM = 64
N_KV_HEADS = 2
N_IDX_HEADS = 8
D_IDX = 64
TOP_K = 256
SSM_HEADS = 8
SSM_HEAD_DIM = 64
SSM_GROUPS = 2
D_STATE = 64
SSM_CHUNK = 128
MEM_HEADS = 4
MEM_HEAD_DIM = 128
IDX_SCALE = (D_IDX ** -0.5) * (N_IDX_HEADS ** -0.5)
NEG_BIG = -1e30
LOG2E = math.log2(math.e)
INT_MIN = -2 ** 31

SM_KI = 0
SM_WI = 64
SM_DT = 72

OFF = dict(glu=0, q=1024, qi=1536, z=2048, mq=2560, xbc=3072, k=3840, v=3968, gates=4096, small=8192)
PROJ_COLS = 8320


def _cp(sem):
    return pltpu.CompilerParams(dimension_semantics=sem, vmem_limit_bytes=V7X_VMEM_LIMIT)


def _nt_dot(a, b):
    return lax.dot_general(a, b, (((1,), (1,)), ((), ())), preferred_element_type=F32)


def _dot(a, b):
    return jnp.dot(a, b, preferred_element_type=F32)


def _split3(x):
    hi = x.astype(BF16)
    r = x - hi.astype(F32)
    mid = r.astype(BF16)
    lo = (r - mid.astype(F32)).astype(BF16)
    return hi, mid, lo


def _dot_sel(x, e):
    hi, mid, lo = _split3(x)
    return _dot(hi, e) + _dot(mid, e) + _dot(lo, e)


def _sel_dot(e, x):
    hi, mid, lo = _split3(x)
    return _dot(e, hi) + _dot(e, mid) + _dot(e, lo)


def _silu(x):
    return x * jax.nn.sigmoid(x)


def _softplus(x):
    return jnp.maximum(x, 0.0) + jnp.log1p(jnp.exp(-jnp.abs(x)))


def _rms(x):
    return x * lax.rsqrt(jnp.mean(x * x, axis=-1, keepdims=True) + EPS)


def _sortable(x):
    b = lax.bitcast_convert_type(x, I32)
    b = jnp.where(x == 0.0, 0, b)
    return jnp.where(b < 0, b ^ 0x7FFFFFFF, b)


def _norm_proj_kernel(x_ref, g_ref, w_ref, o_ref, h_ref):
    @pl.when(pl.program_id(1) == 0)
    def _():
        h_ref[...] = (_rms(x_ref[...]) * g_ref[...]).astype(h_ref.dtype)

    o_ref[...] = _dot(h_ref[...], w_ref[...])


def _norm_proj(x, g, w, tm, tn):
    t, d = x.shape
    n = w.shape[1]
    return pl.pallas_call(
        _norm_proj_kernel,
        grid=(t // tm, n // tn),
        in_specs=[pl.BlockSpec((tm, d), lambda i, j: (i, 0)),
                  pl.BlockSpec((1, d), lambda i, j: (0, 0)),
                  pl.BlockSpec((d, tn), lambda i, j: (0, j))],
        out_specs=pl.BlockSpec((tm, tn), lambda i, j: (i, j)),
        out_shape=jax.ShapeDtypeStruct((t, n), F32),
        scratch_shapes=[pltpu.VMEM((tm, d), MXU_DT)],
        compiler_params=_cp(("parallel", "arbitrary")),
        name="norm_proj",
    )(x, g, w)


def _layernorm_silu(u, g, b):
    xc = u - jnp.mean(u, axis=-1, keepdims=True)
    var = jnp.mean(xc * xc, axis=-1, keepdims=True)
    return _silu(xc * lax.rsqrt(var + EPS) * g + b)


def _conf_prompt_kernel(glu_ref, w_ref, b_ref, lg_ref, lb_ref, o_ref, cnew_ref, abuf, *, tm, cw, dc):
    halo = 32

    @pl.when(pl.program_id(1) == 0)
    def _():
        abuf[0:halo, :] = jnp.zeros((halo, dc), F32)

    glu = glu_ref[...]
    abuf[halo:halo + tm, :] = glu[:, :dc] * jax.nn.sigmoid(glu[:, dc:])
    first = halo - (cw - 1)
    u = b_ref[...] + w_ref[0:1, :] * abuf[first:first + tm, :]
    for j in range(1, cw):
        u = u + w_ref[j:j + 1, :] * abuf[first + j:first + j + tm, :]
    o_ref[...] = _layernorm_silu(u, lg_ref[...], lb_ref[...])
    cnew_ref[0] = abuf[tm + first:tm + halo, :]
    abuf[0:halo, :] = abuf[tm:tm + halo, :]


def _conf_prompt(proj, w, b, lg, lb, nseq, seq, tm):
    cw, dc = w.shape
    nt = seq // tm
    kern = functools.partial(_conf_prompt_kernel, tm=tm, cw=cw, dc=dc)
    return pl.pallas_call(
        kern,
        grid=(nseq, nt),
        in_specs=[pl.BlockSpec((tm, 2 * dc), lambda s, i: (s * nt + i, OFF["glu"] // (2 * dc))),
                  pl.BlockSpec((cw, dc), lambda s, i: (0, 0)),
                  pl.BlockSpec((1, dc), lambda s, i: (0, 0)),
                  pl.BlockSpec((1, dc), lambda s, i: (0, 0)),
                  pl.BlockSpec((1, dc), lambda s, i: (0, 0))],
        out_specs=[pl.BlockSpec((tm, dc), lambda s, i: (s * nt + i, 0)),
                   pl.BlockSpec((1, cw - 1, dc), lambda s, i: (s, 0, 0))],
        out_shape=[jax.ShapeDtypeStruct((nseq * seq, dc), F32),
                   jax.ShapeDtypeStruct((nseq, cw - 1, dc), F32)],
        scratch_shapes=[pltpu.VMEM((tm + 32, dc), F32)],
        compiler_params=_cp(("arbitrary", "arbitrary")),
        name="conformer_prompt",
    )(proj, w, b, lg, lb)


def _conf_sample_kernel(glu_ref, st_ref, w_ref, b_ref, lg_ref, lb_ref, o_ref, a_ref, *, cw, dc):
    glu = glu_ref[...]
    a = glu[:, :dc] * jax.nn.sigmoid(glu[:, dc:])
    u = b_ref[...] + w_ref[cw - 1:cw, :] * a
    for j in range(cw - 1):
        u = u + w_ref[j:j + 1, :] * st_ref[j]
    o_ref[...] = _layernorm_silu(u, lg_ref[...], lb_ref[...])
    a_ref[...] = a


def _conf_sample(proj, st_t, w, b, lg, lb):
    cw, dc = w.shape
    n = proj.shape[0]
    kern = functools.partial(_conf_sample_kernel, cw=cw, dc=dc)
    return pl.pallas_call(
        kern,
        grid=(1,),
        in_specs=[pl.BlockSpec((n, 2 * dc), lambda i: (0, OFF["glu"] // (2 * dc))),
                  pl.BlockSpec((cw - 1, n, dc), lambda i: (0, 0, 0)),
                  pl.BlockSpec((cw, dc), lambda i: (0, 0)),
                  pl.BlockSpec((1, dc), lambda i: (0, 0)),
                  pl.BlockSpec((1, dc), lambda i: (0, 0)),
                  pl.BlockSpec((1, dc), lambda i: (0, 0))],
        out_specs=[pl.BlockSpec((n, dc), lambda i: (0, 0)),
                   pl.BlockSpec((n, dc), lambda i: (0, 0))],
        out_shape=[jax.ShapeDtypeStruct((n, dc), F32), jax.ShapeDtypeStruct((n, dc), F32)],
        compiler_params=_cp(("arbitrary",)),
        name="conformer_sample",
    )(proj, st_t, w, b, lg, lb)


def _rope128(x, c, s):
    lane = lax.broadcasted_iota(I32, x.shape, 1) % HEAD_DIM
    partner = jnp.where(lane < 8, pltpu.roll(x, LANES - 8, 1), pltpu.roll(x, 8, 1))
    return x * c + partner * s


def _dsa_prep_kernel(q_ref, qi_ref, k_ref, v_ref, sm_ref, qg_ref, kg_ref, invf_ref, sgn_ref,
                     invf_s_ref, sgn_s_ref, bd_ref,
                     qn_ref, kn_ref, vo_ref, qir_ref, smr_ref, kir_ref, *, tm, seq, const_pos):
    if const_pos is None:
        base = (pl.program_id(0) * tm) % seq
        pos = (base + lax.broadcasted_iota(I32, (tm, LANES), 0)).astype(F32)
    else:
        pos = jnp.full((tm, LANES), const_pos, F32)
    ang = pos * invf_ref[...]
    c = jnp.cos(ang)
    s = jnp.sin(ang) * sgn_ref[...]
    ang_s = pos * invf_s_ref[...]
    c_s = jnp.cos(ang_s)
    s_s = jnp.sin(ang_s) * sgn_s_ref[...]

    def seg_rms(x, bd):
        x2 = x * x
        hi = x2.astype(BF16)
        lo = (x2 - hi.astype(F32)).astype(BF16)
        ms = (_dot(hi, bd) + _dot(lo, bd)) * (1.0 / HEAD_DIM)
        return x * lax.rsqrt(ms + EPS)

    qn = seg_rms(q_ref[...], bd_ref[...]) * qg_ref[...]
    qi = qi_ref[...]
    for j in range(q_ref.shape[1] // LANES):
        sl = slice(j * LANES, (j + 1) * LANES)
        qn_ref[:, sl] = _rope128(qn[:, sl], c, s)
        qir_ref[:, sl] = _rope128(qi[:, sl], c, s)
    kn = seg_rms(k_ref[...], bd_ref[0:LANES, 0:LANES]) * kg_ref[...]
    kn_ref[...] = _rope128(kn, c, s)
    vo_ref[...] = v_ref[...]
    smr = _rope128(sm_ref[...], c_s, s_s)
    smr_ref[...] = smr
    kir_ref[...] = smr[:, 0:D_IDX]


def _dsa_prep(proj, qg, kg, consts, tm, seq, const_pos):
    t = proj.shape[0]
    dq = N_HEADS * HEAD_DIM
    dk = N_KV_HEADS * HEAD_DIM
    kern = functools.partial(_dsa_prep_kernel, tm=tm, seq=seq, const_pos=const_pos)
    row = lambda w: pl.BlockSpec((1, w), lambda i: (0, 0))
    return pl.pallas_call(
        kern,
        grid=(t // tm,),
        in_specs=[pl.BlockSpec((tm, dq), lambda i: (i, OFF["q"] // dq)),
                  pl.BlockSpec((tm, dq), lambda i: (i, OFF["qi"] // dq)),
                  pl.BlockSpec((tm, dk), lambda i: (i, OFF["k"] // dk)),
                  pl.BlockSpec((tm, dk), lambda i: (i, OFF["v"] // dk)),
                  pl.BlockSpec((tm, LANES), lambda i: (i, OFF["small"] // LANES)),
                  row(dq), row(dk), row(LANES), row(LANES), row(LANES), row(LANES),
                  pl.BlockSpec((dq, dq), lambda i: (0, 0))],
        out_specs=[pl.BlockSpec((tm, dq), lambda i: (i, 0)),
                   pl.BlockSpec((tm, dk), lambda i: (i, 0)),
                   pl.BlockSpec((tm, dk), lambda i: (i, 0)),
                   pl.BlockSpec((tm, dq), lambda i: (i, 0)),
                   pl.BlockSpec((tm, LANES), lambda i: (i, 0)),
                   pl.BlockSpec((tm, D_IDX), lambda i: (i, 0))],
        out_shape=[jax.ShapeDtypeStruct((t, dq), F32), jax.ShapeDtypeStruct((t, dk), F32),
                   jax.ShapeDtypeStruct((t, dk), F32), jax.ShapeDtypeStruct((t, dq), F32),
                   jax.ShapeDtypeStruct((t, LANES), F32), jax.ShapeDtypeStruct((t, D_IDX), F32)],
        compiler_params=_cp(("parallel",)),
        name="dsa_prep",
    )(proj, proj, proj, proj, proj, qg, kg, consts["invf"], consts["sgn"], consts["invf_s"],
      consts["sgn_s"], consts["bd"])


def _dsa_prep_t_kernel(q_ref, qi_ref, k_ref, v_ref, sm_ref, qg_ref, kg_ref, invf_ref, sgn_ref,
                       invf_s_ref, sgn_s_ref, bd_ref,
                       kn_ref, vo_ref, kir_ref, knb_ref, smb_ref, qt_ref, qit_ref, vt_ref, smt_ref, *, tm):
    pos = (pl.program_id(1) * tm + lax.broadcasted_iota(I32, (tm, LANES), 0)).astype(F32)
    ang = pos * invf_ref[...]
    c = jnp.cos(ang)
    s = jnp.sin(ang) * sgn_ref[...]
    ang_s = pos * invf_s_ref[...]
    c_s = jnp.cos(ang_s)
    s_s = jnp.sin(ang_s) * sgn_s_ref[...]

    def seg_rms(x, bd):
        x2 = x * x
        hi = x2.astype(BF16)
        lo = (x2 - hi.astype(F32)).astype(BF16)
        ms = (_dot(hi, bd) + _dot(lo, bd)) * (1.0 / HEAD_DIM)
        return x * lax.rsqrt(ms + EPS)

    lo_half = lax.broadcasted_iota(I32, (tm, LANES), 1) < HEAD_DIM
    qn = seg_rms(q_ref[...], bd_ref[...]) * qg_ref[...]
    qi = qi_ref[...]
    for j in range(N_HEADS // 2):
        sl = slice(j * LANES, (j + 1) * LANES)
        q2 = _rope128(qn[:, sl], c, s) * (HEAD_DIM ** -0.5 * LOG2E)
        q2r = pltpu.roll(q2, HEAD_DIM, 1)
        if (2 * j) // (N_HEADS // N_KV_HEADS) == 0:
            qa, qb = jnp.where(lo_half, q2, 0.0), jnp.where(lo_half, q2r, 0.0)
        else:
            qa, qb = jnp.where(lo_half, 0.0, q2r), jnp.where(lo_half, 0.0, q2)
        qt_ref[0, 2 * j] = qa.T.astype(MXU_DT)
        qt_ref[0, 2 * j + 1] = qb.T.astype(MXU_DT)
        qi2 = _rope128(qi[:, sl], c, s)
        qit_ref[0, 2 * j] = jnp.where(lo_half, qi2, 0.0).T.astype(MXU_DT)
        qit_ref[0, 2 * j + 1] = jnp.where(lo_half, pltpu.roll(qi2, HEAD_DIM, 1), 0.0).T.astype(MXU_DT)
    kn = _rope128(seg_rms(k_ref[...], bd_ref[0:LANES, 0:LANES]) * kg_ref[...], c, s)
    kn_ref[0] = kn.T
    knb_ref[...] = kn.astype(MXU_DT)
    v_t = v_ref[...].T
    vo_ref[0] = v_t
    vt_ref[0, 0] = v_t.astype(MXU_DT)
    smr = _rope128(sm_ref[...], c_s, s_s)
    smb_ref[...] = smr.astype(MXU_DT)
    sm_t = smr.T
    kir_ref[0] = sm_t[0:D_IDX, :]
    smt_ref[0] = sm_t


def _dsa_prep_t(proj, qg, kg, consts, nseq, seq, tm):
    t = proj.shape[0]
    dq = N_HEADS * HEAD_DIM
    dk = N_KV_HEADS * HEAD_DIM
    nt = seq // tm
    kern = functools.partial(_dsa_prep_t_kernel, tm=tm)
    row = lambda w: pl.BlockSpec((1, w), lambda s, i: (0, 0))
    tok = lambda w, col: pl.BlockSpec((tm, w), lambda s, i: (s * nt + i, col))
    return pl.pallas_call(
        kern,
        grid=(nseq, nt),
        in_specs=[tok(dq, OFF["q"] // dq), tok(dq, OFF["qi"] // dq), tok(dk, OFF["k"] // dk),
                  tok(dk, OFF["v"] // dk), tok(LANES, OFF["small"] // LANES),
                  row(dq), row(dk), row(LANES), row(LANES), row(LANES), row(LANES),
                  pl.BlockSpec((dq, dq), lambda s, i: (0, 0))],
        out_specs=[pl.BlockSpec((1, dk, tm), lambda s, i: (s, 0, i)),
                   pl.BlockSpec((1, dk, tm), lambda s, i: (s, 0, i)),
                   pl.BlockSpec((1, D_IDX, tm), lambda s, i: (s, 0, i)),
                   tok(dk, 0), tok(LANES, 0),
                   pl.BlockSpec((1, N_HEADS, LANES, tm), lambda s, i: (s, 0, 0, i)),
                   pl.BlockSpec((1, N_IDX_HEADS, LANES, tm), lambda s, i: (s, 0, 0, i)),
                   pl.BlockSpec((1, 1, dk, tm), lambda s, i: (s, i, 0, 0)),
                   pl.BlockSpec((1, LANES, tm), lambda s, i: (s, 0, i))],
        out_shape=[jax.ShapeDtypeStruct((nseq, dk, seq), F32), jax.ShapeDtypeStruct((nseq, dk, seq), F32),
                   jax.ShapeDtypeStruct((nseq, D_IDX, seq), F32),
                   jax.ShapeDtypeStruct((t, dk), MXU_DT), jax.ShapeDtypeStruct((t, LANES), MXU_DT),
                   jax.ShapeDtypeStruct((nseq, N_HEADS, LANES, seq), MXU_DT),
                   jax.ShapeDtypeStruct((nseq, N_IDX_HEADS, LANES, seq), MXU_DT),
                   jax.ShapeDtypeStruct((nseq, nt, dk, tm), MXU_DT),
                   jax.ShapeDtypeStruct((nseq, LANES, seq), F32)],
        compiler_params=_cp(("parallel", "parallel")),
        name="dsa_prep_prompt",
    )(proj, proj, proj, proj, proj, qg, kg, consts["invf"], consts["sgn"], consts["invf_s"],
      consts["sgn_s"], consts["bd"])


def _dsa_prompt_t_kernel(qt_ref, qit_ref, smt_ref, smk_ref, k_ref, vt_ref, o_ref,
                         keys_ref, k16_ref, acc_ref, thr_ref, ngt_ref, neq_ref, *, tq, ck, ksel, nbits):
    i = pl.program_id(1)
    nc = (i * tq + tq + ck - 1) // ck
    qidx = i * tq + lax.broadcasted_iota(I32, (ck, tq), 1)
    krow = lax.broadcasted_iota(I32, (ck, tq), 0)

    def p1(c, carry):
        kc = smk_ref[pl.ds(pl.multiple_of(c * ck, ck), ck), :]
        acc = jnp.zeros((ck, tq), F32)
        for h in range(N_IDX_HEADS):
            s = _dot(kc, qit_ref[0, h])
            acc = acc + jnp.maximum(s, 0.0) * smt_ref[0, SM_WI + h:SM_WI + h + 1, :]
        causal = c * ck + krow <= qidx
        sc = acc * IDX_SCALE
        bits = jnp.where(sc == 0.0, 0, lax.bitcast_convert_type(sc, I32))
        keys_ref[c] = jnp.where(causal, jnp.where(bits < 0, bits ^ 0x7FFFFFFF, bits), INT_MIN)
        half = lax.bitcast_convert_type(jnp.where(causal, bits & -65536, -1), F32)
        k16_ref[c] = half.astype(BF16)
        return carry

    lax.fori_loop(0, nc, p1, 0)

    def count(pred):
        def body(c, part):
            hit = pred(keys_ref[c], c * ck + krow)
            return part + jnp.sum(hit.reshape(ck // 8, 8, tq), axis=0)

        part = lax.fori_loop(0, nc, body, jnp.zeros((8, tq), I32))
        return jnp.sum(part, axis=0, keepdims=True)

    def bisect(thr0, nsteps):
        def bit_step(t, thr):
            cand = thr + lax.shift_left(jnp.int32(1), nsteps - 1 - t)
            cnt = count(lambda kv, kidx: jnp.where(kv >= cand, 1, 0))
            return jnp.where(cnt >= ksel, cand, thr)

        return lax.fori_loop(0, nsteps, bit_step, thr0)

    one16 = jnp.ones((ck, tq), BF16)
    zero16 = jnp.zeros((ck, tq), BF16)

    def bit16_step(t, thr16):
        cand = thr16 + lax.shift_left(jnp.int32(1), 15 - t)
        raw = jnp.where(cand < 0, cand ^ 0x7FFF, cand)
        raw = jnp.where((raw > 0) & (raw < 0x80), 0x80, raw)
        cand_f = lax.bitcast_convert_type(lax.shift_left(raw, 16), F32).astype(BF16)

        def body(c, part):
            hit = jnp.where(k16_ref[c] >= cand_f, one16, zero16).reshape(ck // 16, 16, tq)
            tot = hit[0]
            for r in range(1, ck // 16):
                tot = tot + hit[r]
            return part + tot.astype(F32)

        part = lax.fori_loop(0, nc, body, jnp.zeros((16, tq), F32))
        cnt = jnp.sum(part, axis=0, keepdims=True)
        return jnp.where(cnt >= ksel, cand, thr16)

    thr16 = lax.fori_loop(0, 16, bit16_step, jnp.full((1, tq), -(2 ** 15), I32))
    thr = bisect(lax.shift_left(thr16, 16), 16)

    def tallies(thr):
        return (count(lambda kv, kidx: jnp.where(kv > thr, 1, 0)),
                count(lambda kv, kidx: jnp.where(kv == thr, 1, 0)))

    n_gt, n_eq = tallies(thr)
    thr_ref[...] = thr
    ngt_ref[...] = n_gt
    neq_ref[...] = n_eq
    missed = jnp.max(jnp.where((n_gt >= ksel) | (n_gt + n_eq < ksel), 1, 0))

    @pl.when(missed > 0)
    def _():
        thr_full = bisect(jnp.full((1, tq), INT_MIN, I32), 32)
        n_gt_full, n_eq_full = tallies(thr_full)
        thr_ref[...] = thr_full
        ngt_ref[...] = n_gt_full
        neq_ref[...] = n_eq_full

    thr = thr_ref[...]
    need = ksel - ngt_ref[...]
    excess = jnp.max(jnp.where((neq_ref[...] > need) & (thr > INT_MIN), 1, 0))

    @pl.when(excess > 0)
    def _():
        def y_step(t, y):
            cand = y + lax.shift_left(jnp.int32(1), nbits - 1 - t)
            g = count(lambda kv, kidx: jnp.where(kv == thr, jnp.where(kidx < cand, 1, 0), 0))
            return jnp.where(g < need, cand, y)

        y = lax.fori_loop(0, nbits, y_step, jnp.zeros((1, tq), I32))
        y = jnp.where(thr > INT_MIN, y, 2 ** 30)

        def demote(c, carry):
            kv = keys_ref[c]
            surplus = jnp.where(kv == thr, jnp.where(c * ck + krow > y, 1, 0), 0)
            keys_ref[c] = jnp.where(surplus > 0, thr - 1, kv)
            return carry

        lax.fori_loop(0, nc, demote, 0)

    thr_sel = jnp.maximum(thr, INT_MIN + 1)

    acc_ref[...] = jnp.zeros(acc_ref.shape, F32)

    def p3(c, carry):
        ms, ls = carry
        bias = jnp.where(keys_ref[c] >= thr_sel, 0.0, NEG_BIG)
        kk = k_ref[pl.ds(pl.multiple_of(c * ck, ck), ck), :]
        vt = vt_ref[0, c]
        m_out, l_out, ps, alphas = [], [], [], []
        for h in range(N_HEADS):
            s = _dot(kk, qt_ref[0, h]) + bias
            m_new = jnp.maximum(ms[h], jnp.max(s, axis=0, keepdims=True))
            alpha = jnp.exp2(ms[h] - m_new)
            p = jnp.exp2(s - m_new)
            m_out.append(m_new)
            l_out.append(alpha * ls[h] + jnp.sum(p, axis=0, keepdims=True))
            alphas.append(alpha)
            ps.append(p.astype(MXU_DT))
        for h in range(N_HEADS):
            acc_ref[h] = alphas[h] * acc_ref[h] + _dot(vt, ps[h])
        return tuple(m_out), tuple(l_out)

    init = (tuple(jnp.full((1, tq), NEG_BIG, F32) for _ in range(N_HEADS)),
            tuple(jnp.zeros((1, tq), F32) for _ in range(N_HEADS)))
    _, lrow = lax.fori_loop(0, nc, p3, init)

    lo_half = lax.broadcasted_iota(I32, (tq, LANES), 1) < HEAD_DIM
    for j in range(N_HEADS // 2):
        ea = (acc_ref[2 * j] / lrow[2 * j]).T
        eb = (acc_ref[2 * j + 1] / lrow[2 * j + 1]).T
        if (2 * j) // (N_HEADS // N_KV_HEADS) == 0:
            out2 = jnp.where(lo_half, ea, pltpu.roll(eb, HEAD_DIM, 1))
        else:
            out2 = jnp.where(lo_half, pltpu.roll(ea, HEAD_DIM, 1), eb)
        o_ref[:, j * LANES:(j + 1) * LANES] = out2


def _dsa_prompt_t(qt, qit, smt, smr, kn, vt, nseq, seq, tq, ck):
    dq = N_HEADS * HEAD_DIM
    dk = N_KV_HEADS * HEAD_DIM
    nq = seq // tq
    nck = seq // ck
    ksel = min(TOP_K, seq // 4)
    assert ck >= ksel and ck % LANES == 0 and seq % ck == 0 and seq % tq == 0 and vt.shape[3] == ck
    nbits = max(1, int(math.ceil(math.log2(seq))))
    kern = functools.partial(_dsa_prompt_t_kernel, tq=tq, ck=ck, ksel=ksel, nbits=nbits)
    return pl.pallas_call(
        kern,
        grid=(nseq, nq),
        in_specs=[pl.BlockSpec((1, N_HEADS, LANES, tq), lambda s, i: (s, 0, 0, i)),
                  pl.BlockSpec((1, N_IDX_HEADS, LANES, tq), lambda s, i: (s, 0, 0, i)),
                  pl.BlockSpec((1, LANES, tq), lambda s, i: (s, 0, i)),
                  pl.BlockSpec((seq, LANES), lambda s, i: (s, 0)),
                  pl.BlockSpec((seq, dk), lambda s, i: (s, 0)),
                  pl.BlockSpec((1, nck, dk, ck), lambda s, i: (s, 0, 0, 0))],
        out_specs=pl.BlockSpec((tq, dq), lambda s, i: (s * nq + i, 0)),
        out_shape=jax.ShapeDtypeStruct((nseq * seq, dq), F32),
        scratch_shapes=[pltpu.VMEM((nck, ck, tq), I32),
                        pltpu.VMEM((nck, ck, tq), BF16),
                        pltpu.VMEM((N_HEADS, dk, tq), F32),
                        pltpu.VMEM((1, tq), I32), pltpu.VMEM((1, tq), I32), pltpu.VMEM((1, tq), I32)],
        compiler_params=_cp(("arbitrary", "arbitrary")),
        name="dsa_prompt",
    )(qt, qit, smt, smr, kn, vt)


def _dsa_prompt_kernel(q_ref, qi_ref, smq_ref, smk_ref, k_ref, v_ref, o_ref,
                       keys_ref, qs_ref, qis_ref, m_ref, l_ref, acc_ref, thr_ref, y_ref,
                       *, tq, ck, ksel, nbits):
    i = pl.program_id(1)
    nc = (i * tq + tq + ck - 1) // ck
    ng = ck // LANES
    lane = lax.broadcasted_iota(I32, (tq, LANES), 1)
    lo = lane < HEAD_DIM
    rowg = i * tq + lax.broadcasted_iota(I32, (tq, LANES), 0)
    scale = HEAD_DIM ** -0.5

    for j in range(N_HEADS // 2):
        sl = slice(j * LANES, (j + 1) * LANES)
        qi2 = qi_ref[:, sl]
        qis_ref[2 * j] = jnp.where(lo, qi2, 0.0)
        qis_ref[2 * j + 1] = jnp.where(lo, pltpu.roll(qi2, HEAD_DIM, 1), 0.0)
        q2 = q_ref[:, sl] * scale
        q2r = pltpu.roll(q2, HEAD_DIM, 1)
        if (2 * j) // (N_HEADS // N_KV_HEADS) == 0:
            qs_ref[2 * j] = jnp.where(lo, q2, 0.0)
            qs_ref[2 * j + 1] = jnp.where(lo, q2r, 0.0)
        else:
            qs_ref[2 * j] = jnp.where(lo, 0.0, q2r)
            qs_ref[2 * j + 1] = jnp.where(lo, 0.0, q2)

    def p1(c, carry):
        kc = smk_ref[pl.ds(pl.multiple_of(c * ck, ck), ck), :]
        acc = jnp.zeros((tq, ck), F32)
        for h in range(N_IDX_HEADS):
            s = _nt_dot(qis_ref[h], kc)
            acc = acc + jnp.maximum(s, 0.0) * smq_ref[:, SM_WI + h:SM_WI + h + 1]
        sc = acc * IDX_SCALE
        for g in range(ng):
            colg = c * ck + g * LANES + lane
            scg = jnp.where(colg <= rowg, sc[:, g * LANES:(g + 1) * LANES], -jnp.inf)
            keys_ref[c, :, g * LANES:(g + 1) * LANES] = _sortable(scg)
        return carry

    lax.fori_loop(0, nc, p1, 0)

    def count(pred):
        def body(c, part):
            for g in range(ng):
                colg = c * ck + g * LANES + lane
                part = part + pred(keys_ref[c, :, g * LANES:(g + 1) * LANES], colg)
            return part

        part = lax.fori_loop(0, nc, body, jnp.zeros((tq, LANES), I32))
        return jnp.sum(part, axis=1, keepdims=True)

    def bit_step(t, thr):
        cand = thr + lax.shift_left(jnp.int32(1), 31 - t)
        cnt = count(lambda kv, colg: jnp.where(kv >= cand, 1, 0))
        return jnp.where(cnt >= ksel, cand, thr)

    thr = lax.fori_loop(0, 32, bit_step, jnp.full((tq, LANES), INT_MIN, I32))
    thr_ref[...] = thr

    n_gt = count(lambda kv, colg: jnp.where(kv > thr, 1, 0))
    n_eq = count(lambda kv, colg: jnp.where(kv == thr, 1, 0))
    need = ksel - n_gt
    y_ref[...] = jnp.full((tq, LANES), 2 ** 30, I32)
    excess = jnp.max(jnp.where(n_eq > need, 1, 0))

    @pl.when(excess > 0)
    def _():
        def y_step(t, y):
            cand = y + lax.shift_left(jnp.int32(1), nbits - 1 - t)
            g = count(lambda kv, colg: jnp.where(kv == thr, jnp.where(colg < cand, 1, 0), 0))
            return jnp.where(g < need, cand, y)

        y_ref[...] = lax.fori_loop(0, nbits, y_step, jnp.zeros((tq, LANES), I32))

    m_ref[...] = jnp.full(m_ref.shape, NEG_BIG, F32)
    l_ref[...] = jnp.zeros(l_ref.shape, F32)
    acc_ref[...] = jnp.zeros(acc_ref.shape, F32)

    def p3(c, carry):
        thr_v = thr_ref[...]
        y_v = y_ref[...]
        biases = []
        for g in range(ng):
            colg = c * ck + g * LANES + lane
            kv = keys_ref[c, :, g * LANES:(g + 1) * LANES]
            sel = jnp.where(kv > thr_v, 1, jnp.where(kv == thr_v, jnp.where(colg <= y_v, 1, 0), 0))
            sel = jnp.where(colg <= rowg, sel, 0)
            biases.append(jnp.where(sel > 0, 0.0, NEG_BIG))
        bias = jnp.concatenate(biases, axis=1)
        off = pl.multiple_of(c * ck, ck)
        kk = k_ref[pl.ds(off, ck), :]
        vv = v_ref[pl.ds(off, ck), :]
        for h in range(N_HEADS):
            s = _nt_dot(qs_ref[h], kk) + bias
            m_old = m_ref[h]
            m_new = jnp.maximum(m_old, jnp.max(s, axis=1, keepdims=True))
            alpha = jnp.exp(m_old - m_new)
            p = jnp.exp(s - m_new)
            l_ref[h] = alpha * l_ref[h] + jnp.sum(p, axis=1, keepdims=True)
            acc_ref[h] = alpha * acc_ref[h] + _dot(p, vv)
            m_ref[h] = m_new
        return carry

    lax.fori_loop(0, nc, p3, 0)

    for j in range(N_HEADS // 2):
        ea = acc_ref[2 * j] / l_ref[2 * j]
        eb = acc_ref[2 * j + 1] / l_ref[2 * j + 1]
        if (2 * j) // (N_HEADS // N_KV_HEADS) == 0:
            out2 = jnp.where(lo, ea, pltpu.roll(eb, HEAD_DIM, 1))
        else:
            out2 = jnp.where(lo, pltpu.roll(ea, HEAD_DIM, 1), eb)
        o_ref[:, j * LANES:(j + 1) * LANES] = out2


def _dsa_prompt(qn, qir, smr, kn, v, nseq, seq, tq, ck):
    dq = N_HEADS * HEAD_DIM
    dk = N_KV_HEADS * HEAD_DIM
    nq = seq // tq
    ksel = min(TOP_K, seq // 4)
    assert ck >= ksel and ck % LANES == 0 and seq % ck == 0 and seq % tq == 0
    nbits = max(1, int(math.ceil(math.log2(seq))))
    kern = functools.partial(_dsa_prompt_kernel, tq=tq, ck=ck, ksel=ksel, nbits=nbits)
    return pl.pallas_call(
        kern,
        grid=(nseq, nq),
        in_specs=[pl.BlockSpec((tq, dq), lambda s, i: (s * nq + i, 0)),
                  pl.BlockSpec((tq, dq), lambda s, i: (s * nq + i, 0)),
                  pl.BlockSpec((tq, LANES), lambda s, i: (s * nq + i, 0)),
                  pl.BlockSpec((seq, LANES), lambda s, i: (s, 0)),
                  pl.BlockSpec((seq, dk), lambda s, i: (s, 0)),
                  pl.BlockSpec((seq, dk), lambda s, i: (s, 0))],
        out_specs=pl.BlockSpec((tq, dq), lambda s, i: (s * nq + i, 0)),
        out_shape=jax.ShapeDtypeStruct((nseq * seq, dq), F32),
        scratch_shapes=[pltpu.VMEM((seq // ck, tq, ck), I32),
                        pltpu.VMEM((N_HEADS, tq, LANES), F32),
                        pltpu.VMEM((N_IDX_HEADS, tq, LANES), F32),
                        pltpu.VMEM((N_HEADS, tq, 1), F32),
                        pltpu.VMEM((N_HEADS, tq, 1), F32),
                        pltpu.VMEM((N_HEADS, tq, LANES), F32),
                        pltpu.VMEM((tq, LANES), I32),
                        pltpu.VMEM((tq, LANES), I32)],
        compiler_params=_cp(("arbitrary", "arbitrary")),
        name="dsa_prompt",
    )(qn, qir, smr, smr, kn, v)


def _dsa_sample_kernel(pt_ref, q8_ref, qi8_ref, wi_ref, kin_ref, kn_ref, vn_ref,
                       cki_hbm, ck_hbm, cv_hbm, o_ref, kib, kb, vb, sem,
                       *, n_pages, page, ksel, nbits):
    b = pl.program_id(0)
    past = n_pages * page

    def copies(p):
        pg = pt_ref[b, p]
        rows = pl.ds(pl.multiple_of(p * page, page), page)
        return (pltpu.make_async_copy(cki_hbm.at[pg], kib.at[rows, :], sem.at[0]),
                pltpu.make_async_copy(ck_hbm.at[pg], kb.at[rows, :], sem.at[1]),
                pltpu.make_async_copy(cv_hbm.at[pg], vb.at[rows, :], sem.at[2]))

    def start(p, carry):
        for cp in copies(p):
            cp.start()
        return carry

    def wait(p, carry):
        for cp in copies(p):
            cp.wait()
        return carry

    lax.fori_loop(0, n_pages, start, 0)
    lax.fori_loop(0, n_pages, wait, 0)

    wcol = wi_ref[0]
    qi8 = qi8_ref[0]
    s = _nt_dot(qi8, kib[...])
    sc = jnp.sum(jnp.maximum(s, 0.0) * wcol, axis=0, keepdims=True) * IDX_SCALE
    s_new = jnp.sum(qi8 * kin_ref[0], axis=1, keepdims=True)
    sc_new = jnp.sum(jnp.maximum(s_new, 0.0) * wcol, axis=0, keepdims=True) * IDX_SCALE
    keys = _sortable(sc)
    key_new = _sortable(sc_new)
    col = lax.broadcasted_iota(I32, (1, past), 1)

    def cnt(main, new):
        return jnp.sum(main, axis=1, keepdims=True) + new

    def bit_step(t, thr):
        cand = thr + lax.shift_left(jnp.int32(1), 31 - t)
        c = cnt(jnp.where(keys >= cand, 1, 0), jnp.where(key_new >= cand, 1, 0))
        return jnp.where(c >= ksel, cand, thr)

    thr = lax.fori_loop(0, 32, bit_step, jnp.full((1, 1), INT_MIN, I32))
    need = ksel - cnt(jnp.where(keys > thr, 1, 0), jnp.where(key_new > thr, 1, 0))

    def y_step(t, y):
        cand = y + lax.shift_left(jnp.int32(1), nbits - 1 - t)
        g = cnt(jnp.where(keys == thr, jnp.where(col < cand, 1, 0), 0),
                jnp.where(key_new == thr, jnp.where(past < cand, 1, 0), 0))
        return jnp.where(g < need, cand, y)

    y = lax.fori_loop(0, nbits, y_step, jnp.zeros((1, 1), I32))
    sel = jnp.where(keys > thr, 1, jnp.where(keys == thr, jnp.where(col <= y, 1, 0), 0))
    sel_new = jnp.where(key_new > thr, 1, jnp.where(key_new == thr, jnp.where(past <= y, 1, 0), 0))
    bias = jnp.where(sel > 0, 0.0, NEG_BIG)
    bias_new = jnp.where(sel_new > 0, 0.0, NEG_BIG)

    q8 = q8_ref[0] * (HEAD_DIM ** -0.5)
    sa = _nt_dot(q8, kb[...]) + bias
    sa_new = jnp.sum(q8 * kn_ref[0], axis=1, keepdims=True) + bias_new
    m = jnp.maximum(jnp.max(sa, axis=1, keepdims=True), sa_new)
    p = jnp.exp(sa - m)
    p_new = jnp.exp(sa_new - m)
    l = jnp.sum(p, axis=1, keepdims=True) + p_new
    o_ref[0] = (_dot(p, vb[...]) + p_new * vn_ref[0]) / l


def _dsa_sample(page_table, q8, qi8, wi, ki_new, k_new, v_new, cki, ck, cv):
    nb, n_pages = page_table.shape
    page = cki.shape[1]
    past = n_pages * page
    dk = N_KV_HEADS * HEAD_DIM
    ksel = min(TOP_K, (past + 1) // 4)
    nbits = int(math.floor(math.log2(past))) + 1
    kern = functools.partial(_dsa_sample_kernel, n_pages=n_pages, page=page, ksel=ksel, nbits=nbits)
    grid_spec = pltpu.PrefetchScalarGridSpec(
        num_scalar_prefetch=1,
        grid=(nb,),
        in_specs=[pl.BlockSpec((1, N_HEADS, dk), lambda b, pt: (b, 0, 0)),
                  pl.BlockSpec((1, N_IDX_HEADS, D_IDX), lambda b, pt: (b, 0, 0)),
                  pl.BlockSpec((1, N_IDX_HEADS, 1), lambda b, pt: (b, 0, 0)),
                  pl.BlockSpec((1, 1, D_IDX), lambda b, pt: (b, 0, 0)),
                  pl.BlockSpec((1, 1, dk), lambda b, pt: (b, 0, 0)),
                  pl.BlockSpec((1, 1, dk), lambda b, pt: (b, 0, 0)),
                  pl.BlockSpec(memory_space=pl.ANY),
                  pl.BlockSpec(memory_space=pl.ANY),
                  pl.BlockSpec(memory_space=pl.ANY)],
        out_specs=pl.BlockSpec((1, N_HEADS, dk), lambda b, pt: (b, 0, 0)),
        scratch_shapes=[pltpu.VMEM((past, D_IDX), F32),
                        pltpu.VMEM((past, dk), F32),
                        pltpu.VMEM((past, dk), F32),
                        pltpu.SemaphoreType.DMA((3,))],
    )
    return pl.pallas_call(
        kern,
        grid_spec=grid_spec,
        out_shape=jax.ShapeDtypeStruct((nb, N_HEADS, dk), F32),
        compiler_params=_cp(("arbitrary",)),
        name="dsa_sample",
    )(page_table, q8, qi8, wi, ki_new, k_new, v_new, cki, ck, cv)


def _page_copies(hbm, layer, pt_ref, b, buf, slot, sem, n_pages, page):
    return [pltpu.make_async_copy(hbm.at[layer, pt_ref[b, p]],
                                  buf.at[slot, :, p * page:(p + 1) * page], sem.at[slot])
            for p in range(n_pages)]


def _prefetch_pages(hbms, bufs, sems, layer, pt_ref, n_pages, page):
    b = pl.program_id(0)
    nb = pl.num_programs(0)
    slot = b % 2

    def start(bb, sl):
        for hbm, buf, sem in zip(hbms, bufs, sems):
            for cp in _page_copies(hbm, layer, pt_ref, bb, buf, sl, sem, n_pages, page):
                cp.start()

    @pl.when(b == 0)
    def _():
        start(0, 0)

    @pl.when(b + 1 < nb)
    def _():
        start(b + 1, 1 - slot)

    for hbm, buf, sem in zip(hbms, bufs, sems):
        for cp in _page_copies(hbm, layer, pt_ref, b, buf, slot, sem, n_pages, page):
            cp.wait()
    return slot


def _idx_sample_kernel(pt_ref, qi8_ref, wi_ref, kin_ref, cki_hbm, sc_ref, kibuf, sem,
                       *, layer, n_pages, page):
    past = n_pages * page
    slot = _prefetch_pages((cki_hbm,), (kibuf,), (sem,), layer, pt_ref, n_pages, page)
    wcol = wi_ref[0]
    qi8 = qi8_ref[0]
    s = _dot(qi8, kibuf[slot])
    sc_ref[0, :, 0:past] = jnp.sum(jnp.maximum(s, 0.0) * wcol, axis=0, keepdims=True) * IDX_SCALE
    s_new = jnp.sum(qi8 * kin_ref[0], axis=1, keepdims=True)
    sc_new = jnp.sum(jnp.maximum(s_new, 0.0) * wcol, axis=0, keepdims=True) * IDX_SCALE
    lane = lax.broadcasted_iota(I32, (1, LANES), 1)
    sc_ref[0, :, past:past + LANES] = jnp.where(lane == 0, sc_new, -jnp.inf)


def _idx_sample(page_table, qi8, wi, ki_new, cki_t, layer):
    nb, n_pages = page_table.shape
    page = cki_t.shape[3]
    past = n_pages * page
    kern = functools.partial(_idx_sample_kernel, layer=layer, n_pages=n_pages, page=page)
    grid_spec = pltpu.PrefetchScalarGridSpec(
        num_scalar_prefetch=1,
        grid=(nb,),
        in_specs=[pl.BlockSpec((1, N_IDX_HEADS, D_IDX), lambda b, pt: (b, 0, 0)),
                  pl.BlockSpec((1, N_IDX_HEADS, 1), lambda b, pt: (b, 0, 0)),
                  pl.BlockSpec((1, 1, D_IDX), lambda b, pt: (b, 0, 0)),
                  pl.BlockSpec(memory_space=pl.ANY)],
        out_specs=pl.BlockSpec((1, 1, past + LANES), lambda b, pt: (b, 0, 0)),
        scratch_shapes=[pltpu.VMEM((2, D_IDX, past), F32), pltpu.SemaphoreType.DMA((2,))],
    )
    return pl.pallas_call(
        kern, grid_spec=grid_spec,
        out_shape=jax.ShapeDtypeStruct((nb, 1, past + LANES), F32),
        compiler_params=_cp(("arbitrary",)),
        name="idx_sample",
    )(page_table, qi8, wi, ki_new, cki_t)


def _topk_bias_kernel(sc_ref, bias_ref, *, ksel, nbits):
    keys = _sortable(sc_ref[...])
    col = lax.broadcasted_iota(I32, keys.shape, 1)

    def cnt(hit):
        return jnp.sum(hit, axis=1, keepdims=True)

    def bit_step(t, thr):
        cand = thr + lax.shift_left(jnp.int32(1), 31 - t)
        return jnp.where(cnt(jnp.where(keys >= cand, 1, 0)) >= ksel, cand, thr)

    thr = lax.fori_loop(0, 32, bit_step, jnp.full((keys.shape[0], 1), INT_MIN, I32))
    need = ksel - cnt(jnp.where(keys > thr, 1, 0))

    def y_step(t, y):
        cand = y + lax.shift_left(jnp.int32(1), nbits - 1 - t)
        g = cnt(jnp.where(keys == thr, jnp.where(col < cand, 1, 0), 0))
        return jnp.where(g < need, cand, y)

    y = lax.fori_loop(0, nbits, y_step, jnp.zeros((keys.shape[0], 1), I32))
    sel = jnp.where(keys > thr, 1, jnp.where(keys == thr, jnp.where(col <= y, 1, 0), 0))
    bias_ref[...] = jnp.where(sel > 0, 0.0, NEG_BIG)


def _topk_bias(sc, ksel):
    nb, width = sc.shape
    nbits = int(math.floor(math.log2(width))) + 1
    kern = functools.partial(_topk_bias_kernel, ksel=ksel, nbits=nbits)
    return pl.pallas_call(
        kern, grid=(1,),
        in_specs=[pl.BlockSpec((nb, width), lambda i: (0, 0))],
        out_specs=pl.BlockSpec((nb, width), lambda i: (0, 0)),
        out_shape=jax.ShapeDtypeStruct((nb, width), F32),
        compiler_params=_cp(("arbitrary",)),
        name="topk_bias_sample",
    )(sc)


def _attn_sample_kernel(pt_ref, q8_ref, bias_ref, kn_ref, vn_ref, ck_hbm, cv_hbm, o_ref,
                        kbuf, vbuf, ksem, vsem, *, layer, n_pages, page):
    past = n_pages * page
    slot = _prefetch_pages((ck_hbm, cv_hbm), (kbuf, vbuf), (ksem, vsem), layer, pt_ref, n_pages, page)
    q8 = q8_ref[0] * (HEAD_DIM ** -0.5)
    sa = _dot(q8, kbuf[slot]) + bias_ref[0, :, 0:past]
    sa_new = jnp.sum(q8 * kn_ref[0], axis=1, keepdims=True) + bias_ref[0, :, past:past + 1]
    m = jnp.maximum(jnp.max(sa, axis=1, keepdims=True), sa_new)
    p = jnp.exp(sa - m)
    p_new = jnp.exp(sa_new - m)
    l = jnp.sum(p, axis=1, keepdims=True) + p_new
    o_ref[0] = (_nt_dot(p, vbuf[slot]) + p_new * vn_ref[0]) / l


def _attn_sample(page_table, q8, bias, k_new, v_new, ck_t, cv_t, layer):
    nb, n_pages = page_table.shape
    page = ck_t.shape[3]
    past = n_pages * page
    dk = N_KV_HEADS * HEAD_DIM
    kern = functools.partial(_attn_sample_kernel, layer=layer, n_pages=n_pages, page=page)
    grid_spec = pltpu.PrefetchScalarGridSpec(
        num_scalar_prefetch=1,
        grid=(nb,),
        in_specs=[pl.BlockSpec((1, N_HEADS, dk), lambda b, pt: (b, 0, 0)),
                  pl.BlockSpec((1, 1, past + LANES), lambda b, pt: (b, 0, 0)),
                  pl.BlockSpec((1, 1, dk), lambda b, pt: (b, 0, 0)),
                  pl.BlockSpec((1, 1, dk), lambda b, pt: (b, 0, 0)),
                  pl.BlockSpec(memory_space=pl.ANY),
                  pl.BlockSpec(memory_space=pl.ANY)],
        out_specs=pl.BlockSpec((1, N_HEADS, dk), lambda b, pt: (b, 0, 0)),
        scratch_shapes=[pltpu.VMEM((2, dk, past), F32), pltpu.VMEM((2, dk, past), F32),
                        pltpu.SemaphoreType.DMA((2,)), pltpu.SemaphoreType.DMA((2,))],
    )
    return pl.pallas_call(
        kern, grid_spec=grid_spec,
        out_shape=jax.ShapeDtypeStruct((nb, N_HEADS, dk), F32),
        compiler_params=_cp(("arbitrary",)),
        name="attn_sample",
    )(page_table, q8, bias, k_new, v_new, ck_t, cv_t)


def _ssd_prompt_kernel(xbc_ref, z_ref, sm_ref, cw_ref, cb_ref, dtb_ref, alog_ref, dskip_ref, ng_ref,
                       tril_ref, e64_ref, e128_ref, bmask_ref,
                       o_ref, cnew_ref, sst_ref, xbuf, st_ref, *, ts, kw, dxbc):
    halo = 8
    di = SSM_HEADS * SSM_HEAD_DIM
    dbc = SSM_GROUPS * D_STATE

    @pl.when(pl.program_id(1) == 0)
    def _():
        xbuf[0:halo, :] = jnp.zeros((halo, dxbc), F32)
        st_ref[...] = jnp.zeros(st_ref.shape, F32)

    xbuf[halo:halo + ts, :] = xbc_ref[...]
    first = halo - (kw - 1)
    conv = cb_ref[...] + cw_ref[0:1, :] * xbuf[first:first + ts, :]
    for j in range(1, kw):
        conv = conv + cw_ref[j:j + 1, :] * xbuf[first + j:first + j + ts, :]
    xc = _silu(conv)
    cnew_ref[0] = xbuf[ts + first:ts + halo, :]
    xbuf[0:halo, :] = xbuf[ts:ts + halo, :]

    lane = lax.broadcasted_iota(I32, (SSM_CHUNK, LANES), 1)
    head_lane = (lane[0:1, :] >= SM_DT) & (lane[0:1, :] < SM_DT + SSM_HEADS)
    a_row = jnp.where(head_lane, -jnp.exp(alog_ref[...]), 0.0)
    tri = lax.broadcasted_iota(I32, (SSM_CHUNK, SSM_CHUNK), 0) >= lax.broadcasted_iota(
        I32, (SSM_CHUNK, SSM_CHUNK), 1)
    glo = lane < D_STATE

    for k in range(ts // SSM_CHUNK):
        rows = slice(k * SSM_CHUNK, (k + 1) * SSM_CHUNK)
        dtf = _softplus(sm_ref[rows, :] + dtb_ref[...])
        adt = dtf * a_row
        a_cs = _sel_dot(tril_ref[...], adt)
        a_cs_t = a_cs.T
        acs_b = _dot_sel(a_cs, e128_ref[...])
        acs_f = _dot_sel(a_cs, e64_ref[...])
        dt_f = _dot_sel(dtf, e64_ref[...])
        alast_f = acs_f[SSM_CHUNK - 1:SSM_CHUNK, :]
        xs = xc[rows, 0:di]
        bm = xc[rows, di:di + dbc]
        cm = xc[rows, di + dbc:di + 2 * dbc]
        xdt = xs * dt_f
        xd = xdt * jnp.exp(alast_f - acs_f)
        bt = bm.T
        cb = (_dot(jnp.where(glo, cm, 0.0), bt), _dot(jnp.where(glo, 0.0, cm), bt))
        pairs = []
        for j in range(SSM_HEADS // 2):
            x2 = xdt[:, j * LANES:(j + 1) * LANES]
            acc = None
            for hh in range(2):
                h = 2 * j + hh
                seg = acs_b[:, h * LANES:(h + 1) * LANES] - a_cs_t[SM_DT + h:SM_DT + h + 1, :]
                lm = jnp.exp(jnp.where(tri, seg, -jnp.inf))
                sc = cb[h // (SSM_HEADS // SSM_GROUPS)] * lm
                xm = jnp.where(glo, x2, 0.0) if hh == 0 else jnp.where(glo, 0.0, x2)
                part = _dot(sc, xm)
                acc = part if acc is None else acc + part
            pairs.append(acc)
        y = jnp.concatenate(pairs, axis=1)
        y = y + _dot(cm, st_ref[...]) * jnp.exp(acs_f) + dskip_ref[...] * xs
        st_ref[...] = st_ref[...] * jnp.exp(alast_f) + bmask_ref[...] * _dot(bt, xd)
        yg = y * _silu(z_ref[rows, :])
        o_ref[rows, :] = _rms(yg) * ng_ref[...]
    sst_ref[0] = st_ref[...]


def _ssd_prompt(proj, cw, cb, dtb_row, alog_row, dskip_f, ng, consts, nseq, seq, ts):
    kw, dxbc = cw.shape
    di = SSM_HEADS * SSM_HEAD_DIM
    nt = seq // ts
    kern = functools.partial(_ssd_prompt_kernel, ts=ts, kw=kw, dxbc=dxbc)
    full = lambda a: pl.BlockSpec(a.shape, lambda s, i: (0,) * a.ndim)
    cs = (consts["tril"], consts["e64"], consts["e128"], consts["bmask"])
    return pl.pallas_call(
        kern,
        grid=(nseq, nt),
        in_specs=[pl.BlockSpec((ts, dxbc), lambda s, i: (s * nt + i, OFF["xbc"] // dxbc)),
                  pl.BlockSpec((ts, di), lambda s, i: (s * nt + i, OFF["z"] // di)),
                  pl.BlockSpec((ts, LANES), lambda s, i: (s * nt + i, OFF["small"] // LANES)),
                  full(cw), full(cb), full(dtb_row), full(alog_row), full(dskip_f), full(ng)]
                 + [full(c) for c in cs],
        out_specs=[pl.BlockSpec((ts, di), lambda s, i: (s * nt + i, 0)),
                   pl.BlockSpec((1, kw - 1, dxbc), lambda s, i: (s, 0, 0)),
                   pl.BlockSpec((1, SSM_GROUPS * D_STATE, di), lambda s, i: (s, 0, 0))],
        out_shape=[jax.ShapeDtypeStruct((nseq * seq, di), F32),
                   jax.ShapeDtypeStruct((nseq, kw - 1, dxbc), F32),
                   jax.ShapeDtypeStruct((nseq, SSM_GROUPS * D_STATE, di), F32)],
        scratch_shapes=[pltpu.VMEM((ts + 8, dxbc), F32),
                        pltpu.VMEM((SSM_GROUPS * D_STATE, di), F32)],
        compiler_params=_cp(("arbitrary", "arbitrary")),
        name="ssd_prompt",
    )(proj, proj, proj, cw, cb, dtb_row, alog_row, dskip_f, ng, *cs)


def _ssd_sample_kernel(xbc_ref, z_ref, sm_ref, st_ref, h0_ref, cw_ref, cb_ref, dtb_ref, alog_ref,
                       dskip_ref, ng_ref, e64_ref, o_ref, hn_ref, y_ref, *, nb, kw):
    di = SSM_HEADS * SSM_HEAD_DIM
    dbc = SSM_GROUPS * D_STATE
    conv = cb_ref[...] + cw_ref[kw - 1:kw, :] * xbc_ref[...]
    for j in range(kw - 1):
        conv = conv + cw_ref[j:j + 1, :] * st_ref[j]
    xc = _silu(conv)
    lane = lax.broadcasted_iota(I32, (1, LANES), 1)
    head_lane = (lane >= SM_DT) & (lane < SM_DT + SSM_HEADS)
    a_row = jnp.where(head_lane, -jnp.exp(alog_ref[...]), 0.0)
    dtf = _softplus(sm_ref[...] + dtb_ref[...])
    dec = jnp.exp(dtf * a_row)
    dt_f = _dot_sel(dtf, e64_ref[...])
    dec_f = _dot_sel(dec, e64_ref[...])
    xs = xc[:, 0:di]
    bm = xc[:, di:di + dbc]
    cm = xc[:, di + dbc:di + 2 * dbc]
    pad = jnp.zeros((LANES - nb, di), F32)
    xdt_t = jnp.concatenate([xs * dt_f, pad], axis=0).T
    dec_t = jnp.concatenate([dec_f, pad], axis=0).T
    bm_r = pltpu.roll(bm, D_STATE, 1)
    cm_r = pltpu.roll(cm, D_STATE, 1)
    rowi = lax.broadcasted_iota(I32, (di, D_STATE), 0)
    g0 = rowi < (SSM_HEADS // SSM_GROUPS) * SSM_HEAD_DIM
    lane_y = lax.broadcasted_iota(I32, (1, di), 1) < (SSM_HEADS // SSM_GROUPS) * SSM_HEAD_DIM
    row8 = lax.broadcasted_iota(I32, (8, D_STATE), 0)
    for b in range(nb):
        bsel = jnp.where(g0, bm[b:b + 1, 0:D_STATE], bm_r[b:b + 1, 0:D_STATE])
        hn = h0_ref[b] * dec_t[:, b:b + 1] + xdt_t[:, b:b + 1] * bsel
        hn_ref[b] = hn
        c2 = jnp.where(row8 == 0, cm[b:b + 1, 0:D_STATE],
                       jnp.where(row8 == 1, cm_r[b:b + 1, 0:D_STATE], 0.0))
        yr = _nt_dot(c2, hn)
        y_ref[b:b + 1, :] = jnp.where(lane_y, yr[0:1, :], yr[1:2, :])
    y = y_ref[...] + dskip_ref[...] * xs
    yg = y * _silu(z_ref[...])
    o_ref[...] = _rms(yg) * ng_ref[...]


def _ssd_sample(proj, st_t, h0, cw, cb, dtb_row, alog_row, dskip_f, ng, consts):
    kw, dxbc = cw.shape
    nb = proj.shape[0]
    di = SSM_HEADS * SSM_HEAD_DIM
    kern = functools.partial(_ssd_sample_kernel, nb=nb, kw=kw)
    full = lambda a: pl.BlockSpec(a.shape, lambda i: (0,) * a.ndim)
    return pl.pallas_call(
        kern,
        grid=(1,),
        in_specs=[pl.BlockSpec((nb, dxbc), lambda i: (0, OFF["xbc"] // dxbc)),
                  pl.BlockSpec((nb, di), lambda i: (0, OFF["z"] // di)),
                  pl.BlockSpec((nb, LANES), lambda i: (0, OFF["small"] // LANES)),
                  full(st_t), full(h0), full(cw), full(cb), full(dtb_row), full(alog_row),
                  full(dskip_f), full(ng), full(consts["e64"])],
        out_specs=[pl.BlockSpec((nb, di), lambda i: (0, 0)),
                   pl.BlockSpec(h0.shape, lambda i: (0, 0, 0))],
        out_shape=[jax.ShapeDtypeStruct((nb, di), F32), jax.ShapeDtypeStruct(h0.shape, F32)],
        scratch_shapes=[pltpu.VMEM((nb, di), F32)],
        compiler_params=_cp(("arbitrary",)),
        name="ssd_sample",
    )(proj, proj, proj, st_t, h0, cw, cb, dtb_row, alog_row, dskip_f, ng, consts["e64"])


def _mem_kv_kernel(x_ref, g_ref, w_ref, kg_ref, mk_ref, mv_ref):
    dm = mk_ref.shape[1]
    m = _dot(_rms(x_ref[...]) * g_ref[...], w_ref[...])
    for h in range(MEM_HEADS):
        sl = slice(h * MEM_HEAD_DIM, (h + 1) * MEM_HEAD_DIM)
        mk_ref[:, sl] = _rms(m[:, sl]) * kg_ref[...]
    mv_ref[...] = m[:, dm:]


def _mem_kv(mem2d, g, w, kg, rows):
    t, d = mem2d.shape
    dm = MEM_HEADS * MEM_HEAD_DIM
    return pl.pallas_call(
        _mem_kv_kernel,
        grid=(t // rows,),
        in_specs=[pl.BlockSpec((rows, d), lambda i: (i, 0)),
                  pl.BlockSpec((1, d), lambda i: (0, 0)),
                  pl.BlockSpec((d, 2 * dm), lambda i: (0, 0)),
                  pl.BlockSpec((1, MEM_HEAD_DIM), lambda i: (0, 0))],
        out_specs=[pl.BlockSpec((rows, dm), lambda i: (i, 0)),
                   pl.BlockSpec((rows, dm), lambda i: (i, 0))],
        out_shape=[jax.ShapeDtypeStruct((t, dm), F32), jax.ShapeDtypeStruct((t, dm), F32)],
        compiler_params=_cp(("parallel",)),
        name="mem_kv",
    )(mem2d, g, w, kg)


def _mem_attend_kernel(mq_ref, mk_ref, mv_ref, g_ref, o_ref, *, tm):
    mq = mq_ref[0]
    rows = max(tm, 8)
    if tm < rows:
        mq = jnp.broadcast_to(mq, (rows, mq.shape[1]))
    for h in range(MEM_HEADS):
        sl = slice(h * MEM_HEAD_DIM, (h + 1) * MEM_HEAD_DIM)
        qn = _rms(mq[:, sl]) * g_ref[...]
        s = _nt_dot(qn, mk_ref[0, :, sl]) * (MEM_HEAD_DIM ** -0.5)
        p = jnp.exp(s - jnp.max(s, axis=1, keepdims=True))
        o = _dot(p, mv_ref[0, :, sl]) / jnp.sum(p, axis=1, keepdims=True)
        o_ref[0, :, sl] = o[0:tm, :]


def _mem_attend(proj3, mk3, mv3, g, tm):
    nseq, seq, _ = proj3.shape
    nm = mk3.shape[1]
    dm = MEM_HEADS * MEM_HEAD_DIM
    kern = functools.partial(_mem_attend_kernel, tm=tm)
    return pl.pallas_call(
        kern,
        grid=(nseq, seq // tm),
        in_specs=[pl.BlockSpec((1, tm, dm), lambda s, i: (s, i, OFF["mq"] // dm)),
                  pl.BlockSpec((1, nm, dm), lambda s, i: (s, 0, 0)),
                  pl.BlockSpec((1, nm, dm), lambda s, i: (s, 0, 0)),
                  pl.BlockSpec((1, MEM_HEAD_DIM), lambda s, i: (0, 0))],
        out_specs=pl.BlockSpec((1, tm, dm), lambda s, i: (s, i, 0)),
        out_shape=jax.ShapeDtypeStruct((nseq, seq, dm), F32),
        compiler_params=_cp(("parallel", "arbitrary")),
        name="mem_attend",
    )(proj3, mk3, mv3, g)


def _merge_kernel(x_ref, ba_ref, bb_ref, bc_ref, bm_ref, gt_ref, wb_ref, wo_ref, o_ref):
    d = x_ref.shape[1]
    acc = None
    for n, br in enumerate((ba_ref, bb_ref, bc_ref, bm_ref)):
        term = jax.nn.sigmoid(gt_ref[:, n * d:(n + 1) * d]) * _dot(br[...].astype(MXU_DT), wb_ref[n])
        acc = term if acc is None else acc + term
    o_ref[...] = x_ref[...] + _dot(acc.astype(MXU_DT), wo_ref[...])


def _merge(x, proj, brs, wb, wo, tm):
    t, d = x.shape
    nbr, bw, _ = wb.shape
    return pl.pallas_call(
        _merge_kernel,
        grid=(t // tm,),
        in_specs=[pl.BlockSpec((tm, d), lambda i: (i, 0))]
                 + [pl.BlockSpec((tm, bw), lambda i: (i, 0))] * nbr
                 + [pl.BlockSpec((tm, nbr * d), lambda i: (i, OFF["gates"] // (nbr * d))),
                    pl.BlockSpec((nbr, bw, d), lambda i: (0, 0, 0)),
                    pl.BlockSpec((d, d), lambda i: (0, 0))],
        out_specs=pl.BlockSpec((tm, d), lambda i: (i, 0)),
        out_shape=jax.ShapeDtypeStruct((t, d), F32),
        compiler_params=_cp(("parallel",)),
        name="merge",
    )(x, *brs, proj, wb, wo)


def _ffn_prompt_kernel(x_ref, g_ref, wg_ref, wu_ref, cwg_ref, cwu_ref, cbg_ref, cbu_ref, wd_ref,
                       o_ref, unew_ref, h_ref, acc_ref, ubuf, carry, *, tm, tc, kw, nff):
    i = pl.program_id(1)
    c = pl.program_id(2)
    halo = 8
    first = halo - (kw - 1)

    @pl.when(c == 0)
    def _():
        h_ref[...] = (_rms(x_ref[...]) * g_ref[...]).astype(h_ref.dtype)
        acc_ref[...] = jnp.zeros(acc_ref.shape, F32)

    @pl.when(i == 0)
    def _():
        carry[c] = jnp.zeros((halo, 2 * tc), F32)

    h = h_ref[...]
    ubuf[0:halo, :] = carry[c]
    ubuf[halo:halo + tm, 0:tc] = _dot(h, wg_ref[...])
    ubuf[halo:halo + tm, tc:2 * tc] = _dot(h, wu_ref[...])
    carry[c] = ubuf[tm:tm + halo, :]
    fg =cbg_ref[...] + cwg_ref[0:1, :] * ubuf[first:first + tm, 0:tc]
    fu = cbu_ref[...] + cwu_ref[0:1, :] * ubuf[first:first + tm, tc:2 * tc]
    for j in range(1, kw):
        fg = fg + cwg_ref[j:j + 1, :] * ubuf[first + j:first + j + tm, 0:tc]
        fu = fu + cwu_ref[j:j + 1, :] * ubuf[first + j:first + j + tm, tc:2 * tc]
    acc_ref[...] += _dot((_silu(fg) * fu).astype(MXU_DT), wd_ref[...])

    @pl.when(c == nff - 1)
    def _():
        o_ref[...] = x_ref[...] + acc_ref[...]

    @pl.when((c == nff - 1) & (i == pl.num_programs(1) - 1))
    def _():
        for cc in range(nff):
            unew_ref[0, :, cc * tc:(cc + 1) * tc] = carry[cc, first:halo, 0:tc]
            unew_ref[0, :, (nff + cc) * tc:(nff + cc + 1) * tc] = carry[cc, first:halo, tc:2 * tc]


def _ffn_prompt(x, g, wup, cw, cb, wd, nseq, seq, tm, tc):
    t, d = x.shape
    dff = wd.shape[0]
    kw = cw.shape[0]
    nt = seq // tm
    nff = dff // tc
    kern = functools.partial(_ffn_prompt_kernel, tm=tm, tc=tc, kw=kw, nff=nff)
    return pl.pallas_call(
        kern,
        grid=(nseq, nt, nff),
        in_specs=[pl.BlockSpec((tm, d), lambda s, i, c: (s * nt + i, 0)),
                  pl.BlockSpec((1, d), lambda s, i, c: (0, 0)),
                  pl.BlockSpec((d, tc), lambda s, i, c: (0, c)),
                  pl.BlockSpec((d, tc), lambda s, i, c: (0, nff + c)),
                  pl.BlockSpec((kw, tc), lambda s, i, c: (0, c)),
                  pl.BlockSpec((kw, tc), lambda s, i, c: (0, nff + c)),
                  pl.BlockSpec((1, tc), lambda s, i, c: (0, c)),
                  pl.BlockSpec((1, tc), lambda s, i, c: (0, nff + c)),
                  pl.BlockSpec((tc, d), lambda s, i, c: (c, 0))],
        out_specs=[pl.BlockSpec((tm, d), lambda s, i, c: (s * nt + i, 0)),
                   pl.BlockSpec((1, kw - 1, 2 * dff), lambda s, i, c: (s, 0, 0))],
        out_shape=[jax.ShapeDtypeStruct((t, d), F32),
                   jax.ShapeDtypeStruct((nseq, kw - 1, 2 * dff), F32)],
        scratch_shapes=[pltpu.VMEM((tm, d), MXU_DT), pltpu.VMEM((tm, d), F32),
                        pltpu.VMEM((tm + 8, 2 * tc), F32), pltpu.VMEM((nff, 8, 2 * tc), F32)],
        compiler_params=_cp(("arbitrary", "arbitrary", "arbitrary")),
        name="ffn_prompt",
    )(x, g, wup, wup, cw, cw, cb, cb, wd)


def _ffn_sample_kernel(x_ref, g_ref, wg_ref, wu_ref, stg_ref, stu_ref, cwg_ref, cwu_ref, cbg_ref,
                       cbu_ref, wd_ref, o_ref, ug_ref, uu_ref, h_ref, acc_ref, *, kw):
    c = pl.program_id(0)

    @pl.when(c == 0)
    def _():
        h_ref[...] = (_rms(x_ref[...]) * g_ref[...]).astype(h_ref.dtype)
        acc_ref[...] = jnp.zeros(acc_ref.shape, F32)

    h = h_ref[...]
    ug = _dot(h, wg_ref[...])
    uu = _dot(h, wu_ref[...])
    ug_ref[...] = ug
    uu_ref[...] = uu
    fg = cbg_ref[...] + cwg_ref[kw - 1:kw, :] * ug
    fu = cbu_ref[...] + cwu_ref[kw - 1:kw, :] * uu
    for j in range(kw - 1):
        fg = fg + cwg_ref[j:j + 1, :] * stg_ref[j]
        fu = fu + cwu_ref[j:j + 1, :] * stu_ref[j]
    acc_ref[...] += _dot((_silu(fg) * fu).astype(MXU_DT), wd_ref[...])

    @pl.when(c == pl.num_programs(0) - 1)
    def _():
        o_ref[...] = x_ref[...] + acc_ref[...]


def _ffn_sample(x, g, wup, st_t, cw, cb, wd, tc):
    t, d = x.shape
    dff = wd.shape[0]
    kw = cw.shape[0]
    nff = dff // tc
    kern = functools.partial(_ffn_sample_kernel, kw=kw)
    return pl.pallas_call(
        kern,
        grid=(nff,),
        in_specs=[pl.BlockSpec((t, d), lambda c: (0, 0)),
                  pl.BlockSpec((1, d), lambda c: (0, 0)),
                  pl.BlockSpec((d, tc), lambda c: (0, c)),
                  pl.BlockSpec((d, tc), lambda c: (0, nff + c)),
                  pl.BlockSpec((kw - 1, t, tc), lambda c: (0, 0, c)),
                  pl.BlockSpec((kw - 1, t, tc), lambda c: (0, 0, nff + c)),
                  pl.BlockSpec((kw, tc), lambda c: (0, c)),
                  pl.BlockSpec((kw, tc), lambda c: (0, nff + c)),
                  pl.BlockSpec((1, tc), lambda c: (0, c)),
                  pl.BlockSpec((1, tc), lambda c: (0, nff + c)),
                  pl.BlockSpec((tc, d), lambda c: (c, 0))],
        out_specs=[pl.BlockSpec((t, d), lambda c: (0, 0)),
                   pl.BlockSpec((t, tc), lambda c: (0, c)),
                   pl.BlockSpec((t, tc), lambda c: (0, c))],
        out_shape=[jax.ShapeDtypeStruct((t, d), F32), jax.ShapeDtypeStruct((t, dff), F32),
                   jax.ShapeDtypeStruct((t, dff), F32)],
        scratch_shapes=[pltpu.VMEM((t, d), MXU_DT), pltpu.VMEM((t, d), F32)],
        compiler_params=_cp(("arbitrary",)),
        name="ffn_sample",
    )(x, g, wup, wup, st_t, st_t, cw, cw, cb, cb, wd)


def _constants():
    lane = np.arange(LANES)
    r = lane % HEAD_DIM
    rot = HEAD_DIM // 4
    half = rot // 2
    inv_freq = ROPE_THETA ** (-jnp.arange(half, dtype=F32) * (2.0 / rot))
    in_rot = r < rot
    invf = jnp.where(jnp.asarray(in_rot), inv_freq[jnp.asarray(r % half)], 0.0).astype(F32)[None, :]
    sgn = np.where(r < half, -1.0, np.where(in_rot, 1.0, 0.0)).astype(np.float32)[None, :]
    first_head = (lane < HEAD_DIM)[None, :]
    dq = N_HEADS * HEAD_DIM
    bd = (np.arange(dq)[:, None] // HEAD_DIM == np.arange(dq)[None, :] // HEAD_DIM)
    di = SSM_HEADS * SSM_HEAD_DIM
    e64 = np.zeros((LANES, di), np.float32)
    e128 = np.zeros((LANES, SSM_HEADS * LANES), np.float32)
    for h in range(SSM_HEADS):
        e64[SM_DT + h, h * SSM_HEAD_DIM:(h + 1) * SSM_HEAD_DIM] = 1.0
        e128[SM_DT + h, h * LANES:(h + 1) * LANES] = 1.0
    tril = np.tril(np.ones((SSM_CHUNK, SSM_CHUNK), np.float32))
    hpg = SSM_HEADS // SSM_GROUPS
    bmask = (np.arange(SSM_GROUPS * D_STATE)[:, None] // D_STATE
             == np.arange(di)[None, :] // (hpg * SSM_HEAD_DIM)).astype(np.float32)
    return dict(
        invf=invf, sgn=jnp.asarray(sgn),
        invf_s=jnp.where(jnp.asarray(first_head), invf, 0.0),
        sgn_s=jnp.asarray(np.where(first_head, sgn, 0.0).astype(np.float32)),
        bd=jnp.asarray(bd.astype(np.float32), dtype=BF16),
        e64=jnp.asarray(e64, dtype=BF16), e128=jnp.asarray(e128, dtype=BF16),
        tril=jnp.asarray(tril, dtype=BF16), bmask=jnp.asarray(bmask))


def _reorder_w_in(w_in):
    d = w_in.shape[0]
    sizes = dict(glu=1024, q=512, k=128, v=128, qi=512, ki=64, wi=8, z=512, xbc=768, dt=8, mq=512,
                 gates=4096)
    order_in = ["glu", "q", "k", "v", "qi", "ki", "wi", "z", "xbc", "dt", "mq", "gates"]
    parts, off = {}, 0
    for name in order_in:
        parts[name] = w_in[:, off:off + sizes[name]]
        off += sizes[name]
    assert off == w_in.shape[1]
    pad = jnp.zeros((d, LANES - sizes["ki"] - sizes["wi"] - sizes["dt"]), w_in.dtype)
    out = jnp.concatenate([parts[n] for n in ("glu", "q", "qi", "z", "mq", "xbc", "k", "v", "gates",
                                              "ki", "wi", "dt")] + [pad], axis=1)
    assert out.shape[1] == PROJ_COLS
    return out


def _pad_lanes(v, start):
    return jnp.zeros((1, LANES), F32).at[0, start:start + v.shape[0]].set(v)


def _tile(n, cap):
    return min(n, cap)


def _layer_params(l, prm, consts):
    p = dict(
        w_in=_reorder_w_in(prm["w_in"][l]).astype(MXU_DT),
        norm_mix_g=prm["norm_mix_g"][l][None, :],
        conv_a_w=prm["conv_a_w"][l], conv_a_b=prm["conv_a_b"][l][None, :],
        ln_a_g=prm["ln_a_g"][l][None, :], ln_a_b=prm["ln_a_b"][l][None, :],
        qg=jnp.tile(prm["q_norm_g"][l], N_HEADS)[None, :],
        kg=jnp.tile(prm["k_norm_g"][l], N_KV_HEADS)[None, :],
        ssm_conv_w=prm["ssm_conv_w"][l], ssm_conv_b=prm["ssm_conv_b"][l][None, :],
        dtb_row=_pad_lanes(prm["dt_bias"][l], SM_DT), alog_row=_pad_lanes(prm["a_log"][l], SM_DT),
        dskip_f=jnp.repeat(prm["d_skip"][l], SSM_HEAD_DIM)[None, :],
        ssm_norm_g=prm["ssm_norm_g"][l][None, :],
        mem_norm_g=prm["mem_norm_g"][l][None, :], w_mem_kv=prm["w_mem_kv"][l],
        mq_norm_g=prm["mq_norm_g"][l][None, :], mk_norm_g=prm["mk_norm_g"][l][None, :],
        w_branch=prm["w_branch"][l].astype(MXU_DT), w_out=prm["w_out"][l].astype(MXU_DT),
        norm_ffn_g=prm["norm_ffn_g"][l][None, :], w_ffn_up=prm["w_ffn_up"][l].astype(MXU_DT),
        ffn_conv_w=prm["ffn_conv_w"][l], ffn_conv_b=prm["ffn_conv_b"][l][None, :],
        w_ffn_down=prm["w_ffn_down"][l].astype(MXU_DT))
    return p


def _ssm_state_from_slab(slab):
    nseq = slab.shape[0]
    hpg = SSM_HEADS // SSM_GROUPS
    s = slab.reshape(nseq, SSM_GROUPS, D_STATE, SSM_HEADS, SSM_HEAD_DIM)
    per_head = [s[:, h // hpg, :, h, :] for h in range(SSM_HEADS)]
    return jnp.swapaxes(jnp.stack(per_head, axis=1), 2, 3)


def _prompt_layer(x, p, consts, mem2d, nseq, seq):
    t, d = x.shape
    nm = mem2d.shape[0] // nseq
    proj = _norm_proj(x, p["norm_mix_g"], p["w_in"], _tile(t, 1024), PROJ_COLS // 5)
    br_a, conf_new = _conf_prompt(proj, p["conv_a_w"], p["conv_a_b"], p["ln_a_g"], p["ln_a_b"],
                                  nseq, seq, _tile(seq, 512))
    ck = _tile(seq, 512)
    kn, v, kir, knb, smb, qt, qit, vt, smt = _dsa_prep_t(proj, p["qg"], p["kg"], consts, nseq, seq, ck)
    br_b = _dsa_prompt_t(qt, qit, smt, smb, knb, vt, nseq, seq, _tile(seq, 256), ck)
    br_c, sconv_new, sslab = _ssd_prompt(proj, p["ssm_conv_w"], p["ssm_conv_b"], p["dtb_row"],
                                         p["alog_row"], p["dskip_f"], p["ssm_norm_g"], consts,
                                         nseq, seq, _tile(seq, 512))
    mk, mv = _mem_kv(mem2d, p["mem_norm_g"], p["w_mem_kv"], p["mk_norm_g"], nm)
    dm = MEM_HEADS * MEM_HEAD_DIM
    br_m = _mem_attend(proj.reshape(nseq, seq, PROJ_COLS), mk.reshape(nseq, nm, dm),
                       mv.reshape(nseq, nm, dm), p["mq_norm_g"], _tile(seq, 512)).reshape(t, dm)
    x = _merge(x, proj, (br_a, br_b, br_c, br_m), p["w_branch"], p["w_out"], _tile(t, 256))
    x, ffn_new = _ffn_prompt(x, p["norm_ffn_g"], p["w_ffn_up"], p["ffn_conv_w"], p["ffn_conv_b"],
                             p["w_ffn_down"], nseq, seq, _tile(seq, 512), p["w_ffn_down"].shape[0] // 2)
    to_tok = lambda a: jnp.transpose(a.reshape(nseq, N_KV_HEADS, HEAD_DIM, seq), (0, 3, 1, 2))
    state = (to_tok(kn), to_tok(v), jnp.transpose(kir, (0, 2, 1)),
             mk.reshape(nseq, nm, MEM_HEADS, MEM_HEAD_DIM), mv.reshape(nseq, nm, MEM_HEADS, MEM_HEAD_DIM),
             conf_new, sconv_new, _ssm_state_from_slab(sslab), ffn_new)
    return x, state


def _sample_layer(x, p, consts, layer, page_table, ck_t, cv_t, cki_t, cmk, cmv, st_conf, st_sconv, st_ssm,
                  st_ffn):
    nb, d = x.shape
    n_pages = page_table.shape[1]
    page = ck_t.shape[3]
    past = n_pages * page
    dq = N_HEADS * HEAD_DIM
    dk = N_KV_HEADS * HEAD_DIM
    proj = _norm_proj(x, p["norm_mix_g"], p["w_in"], nb, PROJ_COLS // 5)
    br_a, a_new = _conf_sample(proj, jnp.swapaxes(st_conf, 0, 1), p["conv_a_w"], p["conv_a_b"],
                               p["ln_a_g"], p["ln_a_b"])
    qn, kn, v, qir, smr, kir = _dsa_prep(proj, p["qg"], p["kg"], consts, nb, 1, float(past))
    hpk = N_HEADS // N_KV_HEADS
    qh = qn.reshape(nb, N_HEADS, HEAD_DIM)
    grp = (np.arange(N_HEADS)[:, None] // hpk == np.arange(dk)[None, :] // HEAD_DIM)
    q8 = jnp.where(jnp.asarray(grp)[None], jnp.tile(qh, (1, 1, N_KV_HEADS)), 0.0)
    sc = _idx_sample(page_table, qir.reshape(nb, N_IDX_HEADS, D_IDX),
                     smr[:, SM_WI:SM_WI + N_IDX_HEADS].reshape(nb, N_IDX_HEADS, 1),
                     kir.reshape(nb, 1, D_IDX), cki_t, layer)
    bias = _topk_bias(sc.reshape(nb, past + LANES), min(TOP_K, (past + 1) // 4))
    o8 = _attn_sample(page_table, q8, bias.reshape(nb, 1, past + LANES), kn.reshape(nb, 1, dk),
                      v.reshape(nb, 1, dk), ck_t, cv_t, layer)
    o8 = o8.reshape(nb, N_HEADS, N_KV_HEADS, HEAD_DIM)
    br_b = jnp.stack([o8[:, h, h // hpk, :] for h in range(N_HEADS)], axis=1).reshape(nb, dq)
    h0 = st_ssm.reshape(nb, SSM_HEADS * SSM_HEAD_DIM, D_STATE)
    br_c, h_new = _ssd_sample(proj, jnp.swapaxes(st_sconv, 0, 1), h0, p["ssm_conv_w"], p["ssm_conv_b"],
                              p["dtb_row"], p["alog_row"], p["dskip_f"], p["ssm_norm_g"], consts)
    nm = cmk.shape[1]
    dm = MEM_HEADS * MEM_HEAD_DIM
    br_m = _mem_attend(proj.reshape(nb, 1, PROJ_COLS), cmk.reshape(nb, nm, dm), cmv.reshape(nb, nm, dm),
                       p["mq_norm_g"], 1).reshape(nb, dm)
    x = _merge(x, proj, (br_a, br_b, br_c, br_m), p["w_branch"], p["w_out"], nb)
    x, ug, uu = _ffn_sample(x, p["norm_ffn_g"], p["w_ffn_up"], jnp.swapaxes(st_ffn, 0, 1),
                            p["ffn_conv_w"], p["ffn_conv_b"], p["w_ffn_down"], 256)
    xbc_raw = proj[:, OFF["xbc"]:OFF["xbc"] + st_sconv.shape[-1]]
    state = (kn.reshape(nb, 1, N_KV_HEADS, HEAD_DIM), v.reshape(nb, 1, N_KV_HEADS, HEAD_DIM),
             kir.reshape(nb, 1, D_IDX),
             jnp.concatenate([st_conf[:, 1:], a_new[:, None]], axis=1),
             jnp.concatenate([st_sconv[:, 1:], xbc_raw[:, None]], axis=1),
             h_new.reshape(st_ssm.shape),
             jnp.concatenate([st_ffn[:, 1:], jnp.concatenate([ug, uu], axis=-1)[:, None]], axis=1))
    return x, state


def kernel(x_prompt, x_sample, cache_k, cache_v, cache_kidx, cache_mem_k, cache_mem_v, state_conformer, state_ssm_conv, state_ssm, state_ffn_conv, page_table, mem_prompt, norm_mix_g, w_in, conv_a_w, conv_a_b, ln_a_g, ln_a_b, q_norm_g, k_norm_g, ssm_conv_w, ssm_conv_b, dt_bias, a_log, d_skip, ssm_norm_g, mem_norm_g, w_mem_kv, mq_norm_g, mk_norm_g, w_branch, w_out, norm_ffn_g, w_ffn_up, ffn_conv_w, ffn_conv_b, w_ffn_down):
    prm = dict(norm_mix_g=norm_mix_g, w_in=w_in, conv_a_w=conv_a_w, conv_a_b=conv_a_b, ln_a_g=ln_a_g,
               ln_a_b=ln_a_b, q_norm_g=q_norm_g, k_norm_g=k_norm_g, ssm_conv_w=ssm_conv_w,
               ssm_conv_b=ssm_conv_b, dt_bias=dt_bias, a_log=a_log, d_skip=d_skip, ssm_norm_g=ssm_norm_g,
               mem_norm_g=mem_norm_g, w_mem_kv=w_mem_kv, mq_norm_g=mq_norm_g, mk_norm_g=mk_norm_g,
               w_branch=w_branch, w_out=w_out, norm_ffn_g=norm_ffn_g, w_ffn_up=w_ffn_up,
               ffn_conv_w=ffn_conv_w, ffn_conv_b=ffn_conv_b, w_ffn_down=w_ffn_down)
    depth = w_in.shape[0]
    nseq, seq, d = x_prompt.shape
    nb, dseq, _ = x_sample.shape
    assert dseq == 1
    consts = _constants()
    xp = x_prompt.reshape(nseq * seq, d)
    xs = x_sample.reshape(nb, d)
    mem2d = mem_prompt.reshape(nseq * mem_prompt.shape[1], d)
    n_phys, page = cache_k.shape[1], cache_k.shape[2]
    ck_t = jnp.transpose(cache_k, (0, 1, 3, 4, 2)).reshape(depth, n_phys, N_KV_HEADS * HEAD_DIM, page)
    cv_t = jnp.transpose(cache_v, (0, 1, 3, 4, 2)).reshape(depth, n_phys, N_KV_HEADS * HEAD_DIM, page)
    cki_t = jnp.transpose(cache_kidx, (0, 1, 3, 2))
    p_states, s_states = [], []
    for l in range(depth):
        p = _layer_params(l, prm, consts)
        xp, st = _prompt_layer(xp, p, consts, mem2d, nseq, seq)
        p_states.append(st)
        xs, st = _sample_layer(xs, p, consts, l, page_table, ck_t, cv_t, cki_t,
                               cache_mem_k[l], cache_mem_v[l], state_conformer[l], state_ssm_conv[l],
                               state_ssm[l], state_ffn_conv[l])
        s_states.append(st)
    stack = lambda states, k: jnp.stack([s[k] for s in states])
    return ((xp.reshape(nseq, seq, d), xs.reshape(nb, 1, d))
            + tuple(stack(p_states, k) for k in range(9))
            + tuple(stack(s_states, k) for k in range(7)))
```

```python
import functools
import math

import numpy as np
import jax
import jax.numpy as jnp
from jax import lax
from jax.experimental import pallas as pl
from jax.experimental.pallas import tpu as pltpu

F32 = jnp.float32
BF16 = jnp.bfloat16
I32 = jnp.int32
MXU_DT = BF16

EPS = 1e-6
ROPE_THETA = 500000.0
LANES = 128
V7X_VMEM_LIMIT = 56 * 1024 * 1024

N_HEADS = 8
HEAD_DIM = 64
N_KV_HEADS = 2
N_IDX_HEADS = 8
D_IDX = 64
TOP_K = 256
SSM_HEADS = 8
SSM_HEAD_DIM = 64
SSM_GROUPS = 2
D_STATE = 64
SSM_CHUNK = 128
MEM_HEADS = 4
MEM_HEAD_DIM = 128
IDX_SCALE = (D_IDX ** -0.5) * (N_IDX_HEADS ** -0.5)
NEG_BIG = -1e30
LOG2E = math.log2(math.e)
INT_MIN = -2 ** 31

SM_KI = 0
SM_WI = 64
SM_DT = 72

OFF = dict(glu=0, q=1024, qi=1536, z=2048, mq=2560, xbc=3072, k=3840, v=3968, gates=4096, small=8192)
PROJ_COLS = 8320


def _cp(sem):
    return pltpu.CompilerParams(dimension_semantics=sem, vmem_limit_bytes=V7X_VMEM_LIMIT)


def _nt_dot(a, b):
    return lax.dot_general(a, b, (((1,), (1,)), ((), ())), preferred_element_type=F32)


def _dot(a, b):
    return jnp.dot(a, b, preferred_element_type=F32)


def _split3(x):
    hi = x.astype(BF16)
    r = x - hi.astype(F32)
    mid = r.astype(BF16)
    lo = (r - mid.astype(F32)).astype(BF16)
    return hi, mid, lo


def _dot_sel(x, e):
    hi, mid, lo = _split3(x)
    return _dot(hi, e) + _dot(mid, e) + _dot(lo, e)


def _sel_dot(e, x):
    hi, mid, lo = _split3(x)
    return _dot(e, hi) + _dot(e, mid) + _dot(e, lo)


def _silu(x):
    return x * jax.nn.sigmoid(x)


def _softplus(x):
    return jnp.maximum(x, 0.0) + jnp.log1p(jnp.exp(-jnp.abs(x)))


def _rms(x):
    return x * lax.rsqrt(jnp.mean(x * x, axis=-1, keepdims=True) + EPS)


def _sortable(x):
    b = lax.bitcast_convert_type(x, I32)
    b = jnp.where(x == 0.0, 0, b)
    return jnp.where(b < 0, b ^ 0x7FFFFFFF, b)


def _norm_proj_kernel(x_ref, g_ref, w_ref, o_ref, h_ref):
    @pl.when(pl.program_id(1) == 0)
    def _():
        h_ref[...] = (_rms(x_ref[...]) * g_ref[...]).astype(h_ref.dtype)

    o_ref[...] = _dot(h_ref[...], w_ref[...])


def _norm_proj(x, g, w, tm, tn):
    t, d = x.shape
    n = w.shape[1]
    return pl.pallas_call(
        _norm_proj_kernel,
        grid=(t // tm, n // tn),
        in_specs=[pl.BlockSpec((tm, d), lambda i, j: (i, 0)),
                  pl.BlockSpec((1, d), lambda i, j: (0, 0)),
                  pl.BlockSpec((d, tn), lambda i, j: (0, j))],
        out_specs=pl.BlockSpec((tm, tn), lambda i, j: (i, j)),
        out_shape=jax.ShapeDtypeStruct((t, n), F32),
        scratch_shapes=[pltpu.VMEM((tm, d), MXU_DT)],
        compiler_params=_cp(("parallel", "arbitrary")),
        name="norm_proj",
    )(x, g, w)


def _layernorm_silu(u, g, b):
    xc = u - jnp.mean(u, axis=-1, keepdims=True)
    var = jnp.mean(xc * xc, axis=-1, keepdims=True)
    return _silu(xc * lax.rsqrt(var + EPS) * g + b)


def _conf_prompt_kernel(glu_ref, w_ref, b_ref, lg_ref, lb_ref, o_ref, cnew_ref, abuf, *, tm, cw, dc):
    halo = 32

    @pl.when(pl.program_id(1) == 0)
    def _():
        abuf[0:halo, :] = jnp.zeros((halo, dc), F32)

    glu = glu_ref[...]
    abuf[halo:halo + tm, :] = glu[:, :dc] * jax.nn.sigmoid(glu[:, dc:])
    first = halo - (cw - 1)
    u = b_ref[...] + w_ref[0:1, :] * abuf[first:first + tm, :]
    for j in range(1, cw):
        u = u + w_ref[j:j + 1, :] * abuf[first + j:first + j + tm, :]
    o_ref[...] = _layernorm_silu(u, lg_ref[...], lb_ref[...])
    cnew_ref[0] = abuf[tm + first:tm + halo, :]
    abuf[0:halo, :] = abuf[tm:tm + halo, :]


def _conf_prompt(proj, w, b, lg, lb, nseq, seq, tm):
    cw, dc = w.shape
    nt = seq // tm
    kern = functools.partial(_conf_prompt_kernel, tm=tm, cw=cw, dc=dc)
    return pl.pallas_call(
        kern,
        grid=(nseq, nt),
        in_specs=[pl.BlockSpec((tm, 2 * dc), lambda s, i: (s * nt + i, OFF["glu"] // (2 * dc))),
                  pl.BlockSpec((cw, dc), lambda s, i: (0, 0)),
                  pl.BlockSpec((1, dc), lambda s, i: (0, 0)),
                  pl.BlockSpec((1, dc), lambda s, i: (0, 0)),
                  pl.BlockSpec((1, dc), lambda s, i: (0, 0))],
        out_specs=[pl.BlockSpec((tm, dc), lambda s, i: (s * nt + i, 0)),
                   pl.BlockSpec((1, cw - 1, dc), lambda s, i: (s, 0, 0))],
        out_shape=[jax.ShapeDtypeStruct((nseq * seq, dc), F32),
                   jax.ShapeDtypeStruct((nseq, cw - 1, dc), F32)],
        scratch_shapes=[pltpu.VMEM((tm + 32, dc), F32)],
        compiler_params=_cp(("arbitrary", "arbitrary")),
        name="conformer_prompt",
    )(proj, w, b, lg, lb)


def _conf_sample_kernel(glu_ref, st_ref, w_ref, b_ref, lg_ref, lb_ref, o_ref, a_ref, *, cw, dc):
    glu = glu_ref[...]
    a = glu[:, :dc] * jax.nn.sigmoid(glu[:, dc:])
    u = b_ref[...] + w_ref[cw - 1:cw, :] * a
    for j in range(cw - 1):
        u = u + w_ref[j:j + 1, :] * st_ref[j]
    o_ref[...] = _layernorm_silu(u, lg_ref[...], lb_ref[...])
    a_ref[...] = a


def _conf_sample(proj, st_t, w, b, lg, lb):
    cw, dc = w.shape
    n = proj.shape[0]
    kern = functools.partial(_conf_sample_kernel, cw=cw, dc=dc)
    return pl.pallas_call(
        kern,
        grid=(1,),
        in_specs=[pl.BlockSpec((n, 2 * dc), lambda i: (0, OFF["glu"] // (2 * dc))),
                  pl.BlockSpec((cw - 1, n, dc), lambda i: (0, 0, 0)),
                  pl.BlockSpec((cw, dc), lambda i: (0, 0)),
                  pl.BlockSpec((1, dc), lambda i: (0, 0)),
                  pl.BlockSpec((1, dc), lambda i: (0, 0)),
                  pl.BlockSpec((1, dc), lambda i: (0, 0))],
        out_specs=[pl.BlockSpec((n, dc), lambda i: (0, 0)),
                   pl.BlockSpec((n, dc), lambda i: (0, 0))],
        out_shape=[jax.ShapeDtypeStruct((n, dc), F32), jax.ShapeDtypeStruct((n, dc), F32)],
        compiler_params=_cp(("arbitrary",)),
        name="conformer_sample",
    )(proj, st_t, w, b, lg, lb)


def _rope128(x, c, s):
    lane = lax.broadcasted_iota(I32, x.shape, 1) % HEAD_DIM
    partner = jnp.where(lane < 8, pltpu.roll(x, LANES - 8, 1), pltpu.roll(x, 8, 1))
    return x * c + partner * s


def _dsa_prep_kernel(q_ref, qi_ref, k_ref, v_ref, sm_ref, qg_ref, kg_ref, invf_ref, sgn_ref,
                     invf_s_ref, sgn_s_ref, bd_ref,
                     qn_ref, kn_ref, vo_ref, qir_ref, smr_ref, kir_ref, *, tm, seq, const_pos):
    if const_pos is None:
        base = (pl.program_id(0) * tm) % seq
        pos = (base + lax.broadcasted_iota(I32, (tm, LANES), 0)).astype(F32)
    else:
        pos = jnp.full((tm, LANES), const_pos, F32)
    ang = pos * invf_ref[...]
    c = jnp.cos(ang)
    s = jnp.sin(ang) * sgn_ref[...]
    ang_s = pos * invf_s_ref[...]
    c_s = jnp.cos(ang_s)
    s_s = jnp.sin(ang_s) * sgn_s_ref[...]

    def seg_rms(x, bd):
        x2 = x * x
        hi = x2.astype(BF16)
        lo = (x2 - hi.astype(F32)).astype(BF16)
        ms = (_dot(hi, bd) + _dot(lo, bd)) * (1.0 / HEAD_DIM)
        return x * lax.rsqrt(ms + EPS)

    qn = seg_rms(q_ref[...], bd_ref[...]) * qg_ref[...]
    qi = qi_ref[...]
    for j in range(q_ref.shape[1] // LANES):
        sl = slice(j * LANES, (j + 1) * LANES)
        qn_ref[:, sl] = _rope128(qn[:, sl], c, s)
        qir_ref[:, sl] = _rope128(qi[:, sl], c, s)
    kn = seg_rms(k_ref[...], bd_ref[0:LANES, 0:LANES]) * kg_ref[...]
    kn_ref[...] = _rope128(kn, c, s)
    vo_ref[...] = v_ref[...]
    smr = _rope128(sm_ref[...], c_s, s_s)
    smr_ref[...] = smr
    kir_ref[...] = smr[:, 0:D_IDX]


def _dsa_prep(proj, qg, kg, consts, tm, seq, const_pos):
    t = proj.shape[0]
    dq = N_HEADS * HEAD_DIM
    dk = N_KV_HEADS * HEAD_DIM
    kern = functools.partial(_dsa_prep_kernel, tm=tm, seq=seq, const_pos=const_pos)
    row = lambda w: pl.BlockSpec((1, w), lambda i: (0, 0))
    return pl.pallas_call(
        kern,
        grid=(t // tm,),
        in_specs=[pl.BlockSpec((tm, dq), lambda i: (i, OFF["q"] // dq)),
                  pl.BlockSpec((tm, dq), lambda i: (i, OFF["qi"] // dq)),
                  pl.BlockSpec((tm, dk), lambda i: (i, OFF["k"] // dk)),
                  pl.BlockSpec((tm, dk), lambda i: (i, OFF["v"] // dk)),
                  pl.BlockSpec((tm, LANES), lambda i: (i, OFF["small"] // LANES)),
                  row(dq), row(dk), row(LANES), row(LANES), row(LANES), row(LANES),
                  pl.BlockSpec((dq, dq), lambda i: (0, 0))],
        out_specs=[pl.BlockSpec((tm, dq), lambda i: (i, 0)),
                   pl.BlockSpec((tm, dk), lambda i: (i, 0)),
                   pl.BlockSpec((tm, dk), lambda i: (i, 0)),
                   pl.BlockSpec((tm, dq), lambda i: (i, 0)),
                   pl.BlockSpec((tm, LANES), lambda i: (i, 0)),
                   pl.BlockSpec((tm, D_IDX), lambda i: (i, 0))],
        out_shape=[jax.ShapeDtypeStruct((t, dq), F32), jax.ShapeDtypeStruct((t, dk), F32),
                   jax.ShapeDtypeStruct((t, dk), F32), jax.ShapeDtypeStruct((t, dq), F32),
                   jax.ShapeDtypeStruct((t, LANES), F32), jax.ShapeDtypeStruct((t, D_IDX), F32)],
        compiler_params=_cp(("parallel",)),
        name="dsa_prep",
    )(proj, proj, proj, proj, proj, qg, kg, consts["invf"], consts["sgn"], consts["invf_s"],
      consts["sgn_s"], consts["bd"])


def _dsa_prep_t_kernel(q_ref, qi_ref, k_ref, v_ref, sm_ref, qg_ref, kg_ref, invf_ref, sgn_ref,
                       invf_s_ref, sgn_s_ref, bd_ref,
                       kn_ref, vo_ref, kir_ref, knb_ref, smb_ref, qt_ref, qit_ref, vt_ref, smt_ref, *, tm):
    pos = (pl.program_id(1) * tm + lax.broadcasted_iota(I32, (tm, LANES), 0)).astype(F32)
    ang = pos * invf_ref[...]
    c = jnp.cos(ang)
    s = jnp.sin(ang) * sgn_ref[...]
    ang_s = pos * invf_s_ref[...]
    c_s = jnp.cos(ang_s)
    s_s = jnp.sin(ang_s) * sgn_s_ref[...]

    def seg_rms(x, bd):
        x2 = x * x
        hi = x2.astype(BF16)
        lo = (x2 - hi.astype(F32)).astype(BF16)
        ms = (_dot(hi, bd) + _dot(lo, bd)) * (1.0 / HEAD_DIM)
        return x * lax.rsqrt(ms + EPS)

    lo_half = lax.broadcasted_iota(I32, (tm, LANES), 1) < HEAD_DIM
    qn = seg_rms(q_ref[...], bd_ref[...]) * qg_ref[...]
    qi = qi_ref[...]
    for j in range(N_HEADS // 2):
        sl = slice(j * LANES, (j + 1) * LANES)
        q2 = _rope128(qn[:, sl], c, s) * (HEAD_DIM ** -0.5 * LOG2E)
        q2r = pltpu.roll(q2, HEAD_DIM, 1)
        if (2 * j) // (N_HEADS // N_KV_HEADS) == 0:
            qa, qb = jnp.where(lo_half, q2, 0.0), jnp.where(lo_half, q2r, 0.0)
        else:
            qa, qb = jnp.where(lo_half, 0.0, q2r), jnp.where(lo_half, 0.0, q2)
        qt_ref[0, 2 * j] = qa.T.astype(MXU_DT)
        qt_ref[0, 2 * j + 1] = qb.T.astype(MXU_DT)
        qi2 = _rope128(qi[:, sl], c, s)
        qit_ref[0, 2 * j] = jnp.where(lo_half, qi2, 0.0).T.astype(MXU_DT)
        qit_ref[0, 2 * j + 1] = jnp.where(lo_half, pltpu.roll(qi2, HEAD_DIM, 1), 0.0).T.astype(MXU_DT)
    kn = _rope128(seg_rms(k_ref[...], bd_ref[0:LANES, 0:LANES]) * kg_ref[...], c, s)
    kn_ref[0] = kn.T
    knb_ref[...] = kn.astype(MXU_DT)
    v_t = v_ref[...].T
    vo_ref[0] = v_t
    vt_ref[0, 0] = v_t.astype(MXU_DT)
    smr = _rope128(sm_ref[...], c_s, s_s)
    smb_ref[...] = smr.astype(MXU_DT)
    sm_t = smr.T
    kir_ref[0] = sm_t[0:D_IDX, :]
    smt_ref[0] = sm_t


def _dsa_prep_t(proj, qg, kg, consts, nseq, seq, tm):
    t = proj.shape[0]
    dq = N_HEADS * HEAD_DIM
    dk = N_KV_HEADS * HEAD_DIM
    nt = seq // tm
    kern = functools.partial(_dsa_prep_t_kernel, tm=tm)
    row = lambda w: pl.BlockSpec((1, w), lambda s, i: (0, 0))
    tok = lambda w, col: pl.BlockSpec((tm, w), lambda s, i: (s * nt + i, col))
    return pl.pallas_call(
        kern,
        grid=(nseq, nt),
        in_specs=[tok(dq, OFF["q"] // dq), tok(dq, OFF["qi"] // dq), tok(dk, OFF["k"] // dk),
                  tok(dk, OFF["v"] // dk), tok(LANES, OFF["small"] // LANES),
                  row(dq), row(dk), row(LANES), row(LANES), row(LANES), row(LANES),
                  pl.BlockSpec((dq, dq), lambda s, i: (0, 0))],
        out_specs=[pl.BlockSpec((1, dk, tm), lambda s, i: (s, 0, i)),
                   pl.BlockSpec((1, dk, tm), lambda s, i: (s, 0, i)),
                   pl.BlockSpec((1, D_IDX, tm), lambda s, i: (s, 0, i)),
                   tok(dk, 0), tok(LANES, 0),
                   pl.BlockSpec((1, N_HEADS, LANES, tm), lambda s, i: (s, 0, 0, i)),
                   pl.BlockSpec((1, N_IDX_HEADS, LANES, tm), lambda s, i: (s, 0, 0, i)),
                   pl.BlockSpec((1, 1, dk, tm), lambda s, i: (s, i, 0, 0)),
                   pl.BlockSpec((1, LANES, tm), lambda s, i: (s, 0, i))],
        out_shape=[jax.ShapeDtypeStruct((nseq, dk, seq), F32), jax.ShapeDtypeStruct((nseq, dk, seq), F32),
                   jax.ShapeDtypeStruct((nseq, D_IDX, seq), F32),
                   jax.ShapeDtypeStruct((t, dk), MXU_DT), jax.ShapeDtypeStruct((t, LANES), MXU_DT),
                   jax.ShapeDtypeStruct((nseq, N_HEADS, LANES, seq), MXU_DT),
                   jax.ShapeDtypeStruct((nseq, N_IDX_HEADS, LANES, seq), MXU_DT),
                   jax.ShapeDtypeStruct((nseq, nt, dk, tm), MXU_DT),
                   jax.ShapeDtypeStruct((nseq, LANES, seq), F32)],
        compiler_params=_cp(("parallel", "parallel")),
        name="dsa_prep_prompt",
    )(proj, proj, proj, proj, proj, qg, kg, consts["invf"], consts["sgn"], consts["invf_s"],
      consts["sgn_s"], consts["bd"])


def _dsa_prompt_t_kernel(qt_ref, qit_ref, smt_ref, smk_ref, k_ref, vt_ref, tri_ref, o_ref,
                         keys_ref, k16_ref, acc_ref, thr_ref, ngt_ref, neq_ref, *, tq, ck, ksel, nbits):
    i = pl.program_id(1)
    nc = (i * tq + tq + ck - 1) // ck
    qidx = i * tq + lax.broadcasted_iota(I32, (ck, tq), 1)
    krow = lax.broadcasted_iota(I32, (ck, tq), 0)

    def p1(c, carry):
        kc = smk_ref[pl.ds(pl.multiple_of(c * ck, ck), ck), :]
        acc = jnp.zeros((ck, tq), F32)
        for h in range(N_IDX_HEADS):
            s = _dot(kc, qit_ref[0, h])
            acc = acc + jnp.maximum(s, 0.0) * smt_ref[0, SM_WI + h:SM_WI + h + 1, :]
        causal = c * ck + krow <= qidx
        sc = acc * IDX_SCALE
        bits = jnp.where(sc == 0.0, 0, lax.bitcast_convert_type(sc, I32))
        keys_ref[c] = jnp.where(causal, jnp.where(bits < 0, bits ^ 0x7FFFFFFF, bits), INT_MIN)
        half = lax.bitcast_convert_type(jnp.where(causal, bits & -65536, -1), F32)
        k16_ref[c] = half.astype(BF16)
        return carry

    lax.fori_loop(0, nc, p1, 0)

    def count(pred):
        def body(c, part):
            hit = pred(keys_ref[c], c * ck + krow)
            return part + jnp.sum(hit.reshape(ck // 8, 8, tq), axis=0)

        part = lax.fori_loop(0, nc, body, jnp.zeros((8, tq), I32))
        return jnp.sum(part, axis=0, keepdims=True)

    def bisect(thr0, nsteps):
        def bit_step(t, thr):
            cand = thr + lax.shift_left(jnp.int32(1), nsteps - 1 - t)
            cnt = count(lambda kv, kidx: jnp.where(kv >= cand, 1, 0))
            return jnp.where(cnt >= ksel, cand, thr)

        return lax.fori_loop(0, nsteps, bit_step, thr0)

    one16 = jnp.ones((ck, tq), BF16)
    zero16 = jnp.zeros((ck, tq), BF16)

    def bit16_step(t, thr16):
        cand = thr16 + lax.shift_left(jnp.int32(1), 15 - t)
        raw = jnp.where(cand < 0, cand ^ 0x7FFF, cand)
        raw = jnp.where((raw > 0) & (raw < 0x80), 0x80, raw)
        cand_f = lax.bitcast_convert_type(lax.shift_left(raw, 16), F32).astype(BF16)

        def body(c, part):
            hit = jnp.where(k16_ref[c] >= cand_f, one16, zero16).reshape(ck // 16, 16, tq)
            tot = hit[0]
            for r in range(1, ck // 16):
                tot = tot + hit[r]
            return part + tot.astype(F32)

        part = lax.fori_loop(0, nc, body, jnp.zeros((16, tq), F32))
        cnt = jnp.sum(part, axis=0, keepdims=True)
        return jnp.where(cnt >= ksel, cand, thr16)

    thr16 = lax.fori_loop(0, 16, bit16_step, jnp.full((1, tq), -(2 ** 15), I32))
    thr = bisect(lax.shift_left(thr16, 16), 16)

    def tallies(thr):
        return (count(lambda kv, kidx: jnp.where(kv > thr, 1, 0)),
                count(lambda kv, kidx: jnp.where(kv == thr, 1, 0)))

    n_gt, n_eq = tallies(thr)
    thr_ref[...] = thr
    ngt_ref[...] = n_gt
    neq_ref[...] = n_eq
    missed = jnp.max(jnp.where((n_gt >= ksel) | (n_gt + n_eq < ksel), 1, 0))

    @pl.when(missed > 0)
    def _():
        thr_full = bisect(jnp.full((1, tq), INT_MIN, I32), 32)
        n_gt_full, n_eq_full = tallies(thr_full)
        thr_ref[...] = thr_full
        ngt_ref[...] = n_gt_full
        neq_ref[...] = n_eq_full

    thr = thr_ref[...]
    need = ksel - ngt_ref[...]
    excess = jnp.max(jnp.where((neq_ref[...] > need) & (thr > INT_MIN), 1, 0))

    @pl.when(excess > 0)
    def _():
        keep = jnp.where(thr > INT_MIN, need, 2 ** 30).astype(F32)

        def demote(c, seen):
            kv = keys_ref[c]
            tied = jnp.where(kv == thr, 1.0, 0.0)
            prefix = _dot(tri_ref[...], tied.astype(BF16))
            surplus = jnp.where(seen + prefix > keep, tied, 0.0)
            keys_ref[c] = jnp.where(surplus > 0.0, thr - 1, kv)
            return seen + prefix[ck - 1:ck, :]

        lax.fori_loop(0, nc, demote, jnp.zeros((1, tq), F32))

    thr_sel = jnp.maximum(thr, INT_MIN + 1)

    acc_ref[...] = jnp.zeros(acc_ref.shape, F32)

    def p3(c, carry):
        ms, ls = carry
        bias = jnp.where(keys_ref[c] >= thr_sel, 0.0, NEG_BIG)
        kk = k_ref[pl.ds(pl.multiple_of(c * ck, ck), ck), :]
        vt = vt_ref[0, c]
        m_out, l_out, ps, alphas = [], [], [], []
        for h in range(N_HEADS):
            s = _dot(kk, qt_ref[0, h]) + bias
            m_new = jnp.maximum(ms[h], jnp.max(s, axis=0, keepdims=True))
            alpha = jnp.exp2(ms[h] - m_new)
            p = jnp.exp2(s - m_new)
            m_out.append(m_new)
            l_out.append(alpha * ls[h] + jnp.sum(p, axis=0, keepdims=True))
            alphas.append(alpha)
            ps.append(p.astype(MXU_DT))
        for h in range(N_HEADS):
            acc_ref[h] = alphas[h] * acc_ref[h] + _dot(vt, ps[h])
        return tuple(m_out), tuple(l_out)

    init = (tuple(jnp.full((1, tq), NEG_BIG, F32) for _ in range(N_HEADS)),
            tuple(jnp.zeros((1, tq), F32) for _ in range(N_HEADS)))
    _, lrow = lax.fori_loop(0, nc, p3, init)

    lo_half = lax.broadcasted_iota(I32, (tq, LANES), 1) < HEAD_DIM
    for j in range(N_HEADS // 2):
        ea = (acc_ref[2 * j] / lrow[2 * j]).T
        eb = (acc_ref[2 * j + 1] / lrow[2 * j + 1]).T
        if (2 * j) // (N_HEADS // N_KV_HEADS) == 0:
            out2 = jnp.where(lo_half, ea, pltpu.roll(eb, HEAD_DIM, 1))
        else:
            out2 = jnp.where(lo_half, pltpu.roll(ea, HEAD_DIM, 1), eb)
        o_ref[:, j * LANES:(j + 1) * LANES] = out2


def _dsa_prompt_t(qt, qit, smt, smr, kn, vt, nseq, seq, tq, ck):
    dq = N_HEADS * HEAD_DIM
    dk = N_KV_HEADS * HEAD_DIM
    nq = seq // tq
    nck = seq // ck
    ksel = min(TOP_K, seq // 4)
    assert ck >= ksel and ck % LANES == 0 and seq % ck == 0 and seq % tq == 0 and vt.shape[3] == ck
    nbits = max(1, int(math.ceil(math.log2(seq))))
    kern = functools.partial(_dsa_prompt_t_kernel, tq=tq, ck=ck, ksel=ksel, nbits=nbits)
    return pl.pallas_call(
        kern,
        grid=(nseq, nq),
        in_specs=[pl.BlockSpec((1, N_HEADS, LANES, tq), lambda s, i: (s, 0, 0, i)),
                  pl.BlockSpec((1, N_IDX_HEADS, LANES, tq), lambda s, i: (s, 0, 0, i)),
                  pl.BlockSpec((1, LANES, tq), lambda s, i: (s, 0, i)),
                  pl.BlockSpec((seq, LANES), lambda s, i: (s, 0)),
                  pl.BlockSpec((seq, dk), lambda s, i: (s, 0)),
                  pl.BlockSpec((1, nck, dk, ck), lambda s, i: (s, 0, 0, 0)),
                  pl.BlockSpec((ck, ck), lambda s, i: (0, 0))],
        out_specs=pl.BlockSpec((tq, dq), lambda s, i: (s * nq + i, 0)),
        out_shape=jax.ShapeDtypeStruct((nseq * seq, dq), F32),
        scratch_shapes=[pltpu.VMEM((nck, ck, tq), I32),
                        pltpu.VMEM((nck, ck, tq), BF16),
                        pltpu.VMEM((N_HEADS, dk, tq), F32),
                        pltpu.VMEM((1, tq), I32), pltpu.VMEM((1, tq), I32), pltpu.VMEM((1, tq), I32)],
        compiler_params=_cp(("arbitrary", "arbitrary")),
        name="dsa_prompt",
    )(qt, qit, smt, smr, kn, vt, jnp.asarray(np.tril(np.ones((ck, ck), np.float32)), dtype=BF16))


def _dsa_prompt_kernel(q_ref, qi_ref, smq_ref, smk_ref, k_ref, v_ref, o_ref,
                       keys_ref, qs_ref, qis_ref, m_ref, l_ref, acc_ref, thr_ref, y_ref,
                       *, tq, ck, ksel, nbits):
    i = pl.program_id(1)
    nc = (i * tq + tq + ck - 1) // ck
    ng = ck // LANES
    lane = lax.broadcasted_iota(I32, (tq, LANES), 1)
    lo = lane < HEAD_DIM
    rowg = i * tq + lax.broadcasted_iota(I32, (tq, LANES), 0)
    scale = HEAD_DIM ** -0.5

    for j in range(N_HEADS // 2):
        sl = slice(j * LANES, (j + 1) * LANES)
        qi2 = qi_ref[:, sl]
        qis_ref[2 * j] = jnp.where(lo, qi2, 0.0)
        qis_ref[2 * j + 1] = jnp.where(lo, pltpu.roll(qi2, HEAD_DIM, 1), 0.0)
        q2 = q_ref[:, sl] * scale
        q2r = pltpu.roll(q2, HEAD_DIM, 1)
        if (2 * j) // (N_HEADS // N_KV_HEADS) == 0:
            qs_ref[2 * j] = jnp.where(lo, q2, 0.0)
            qs_ref[2 * j + 1] = jnp.where(lo, q2r, 0.0)
        else:
            qs_ref[2 * j] = jnp.where(lo, 0.0, q2r)
            qs_ref[2 * j + 1] = jnp.where(lo, 0.0, q2)

    def p1(c, carry):
        kc = smk_ref[pl.ds(pl.multiple_of(c * ck, ck), ck), :]
        acc = jnp.zeros((tq, ck), F32)
        for h in range(N_IDX_HEADS):
            s = _nt_dot(qis_ref[h], kc)
            acc = acc + jnp.maximum(s, 0.0) * smq_ref[:, SM_WI + h:SM_WI + h + 1]
        sc = acc * IDX_SCALE
        for g in range(ng):
            colg = c * ck + g * LANES + lane
            scg = jnp.where(colg <= rowg, sc[:, g * LANES:(g + 1) * LANES], -jnp.inf)
            keys_ref[c, :, g * LANES:(g + 1) * LANES] = _sortable(scg)
        return carry

    lax.fori_loop(0, nc, p1, 0)

    def count(pred):
        def body(c, part):
            for g in range(ng):
                colg = c * ck + g * LANES + lane
                part = part + pred(keys_ref[c, :, g * LANES:(g + 1) * LANES], colg)
            return part

        part = lax.fori_loop(0, nc, body, jnp.zeros((tq, LANES), I32))
        return jnp.sum(part, axis=1, keepdims=True)

    def bit_step(t, thr):
        cand = thr + lax.shift_left(jnp.int32(1), 31 - t)
        cnt = count(lambda kv, colg: jnp.where(kv >= cand, 1, 0))
        return jnp.where(cnt >= ksel, cand, thr)

    thr = lax.fori_loop(0, 32, bit_step, jnp.full((tq, LANES), INT_MIN, I32))
    thr_ref[...] = thr

    n_gt = count(lambda kv, colg: jnp.where(kv > thr, 1, 0))
    n_eq = count(lambda kv, colg: jnp.where(kv == thr, 1, 0))
    need = ksel - n_gt
    y_ref[...] = jnp.full((tq, LANES), 2 ** 30, I32)
    excess = jnp.max(jnp.where(n_eq > need, 1, 0))

    @pl.when(excess > 0)
    def _():
        def y_step(t, y):
            cand = y + lax.shift_left(jnp.int32(1), nbits - 1 - t)
            g = count(lambda kv, colg: jnp.where(kv == thr, jnp.where(colg < cand, 1, 0), 0))
            return jnp.where(g < need, cand, y)

        y_ref[...] = lax.fori_loop(0, nbits, y_step, jnp.zeros((tq, LANES), I32))

    m_ref[...] = jnp.full(m_ref.shape, NEG_BIG, F32)
    l_ref[...] = jnp.zeros(l_ref.shape, F32)
    acc_ref[...] = jnp.zeros(acc_ref.shape, F32)

    def p3(c, carry):
        thr_v = thr_ref[...]
        y_v = y_ref[...]
        biases = []
        for g in range(ng):
            colg = c * ck + g * LANES + lane
            kv = keys_ref[c, :, g * LANES:(g + 1) * LANES]
            sel = jnp.where(kv > thr_v, 1, jnp.where(kv == thr_v, jnp.where(colg <= y_v, 1, 0), 0))
            sel = jnp.where(colg <= rowg, sel, 0)
            biases.append(jnp.where(sel > 0, 0.0, NEG_BIG))
        bias = jnp.concatenate(biases, axis=1)
        off = pl.multiple_of(c * ck, ck)
        kk = k_ref[pl.ds(off, ck), :]
        vv = v_ref[pl.ds(off, ck), :]
        for h in range(N_HEADS):
            s = _nt_dot(qs_ref[h], kk) + bias
            m_old = m_ref[h]
            m_new = jnp.maximum(m_old, jnp.max(s, axis=1, keepdims=True))
            alpha = jnp.exp(m_old - m_new)
            p = jnp.exp(s - m_new)
            l_ref[h] = alpha * l_ref[h] + jnp.sum(p, axis=1, keepdims=True)
            acc_ref[h] = alpha * acc_ref[h] + _dot(p, vv)
            m_ref[h] = m_new
        return carry

    lax.fori_loop(0, nc, p3, 0)

    for j in range(N_HEADS // 2):
        ea = acc_ref[2 * j] / l_ref[2 * j]
        eb = acc_ref[2 * j + 1] / l_ref[2 * j + 1]
        if (2 * j) // (N_HEADS // N_KV_HEADS) == 0:
            out2 = jnp.where(lo, ea, pltpu.roll(eb, HEAD_DIM, 1))
        else:
            out2 = jnp.where(lo, pltpu.roll(ea, HEAD_DIM, 1), eb)
        o_ref[:, j * LANES:(j + 1) * LANES] = out2


def _dsa_prompt(qn, qir, smr, kn, v, nseq, seq, tq, ck):
    dq = N_HEADS * HEAD_DIM
    dk = N_KV_HEADS * HEAD_DIM
    nq = seq // tq
    ksel = min(TOP_K, seq // 4)
    assert ck >= ksel and ck % LANES == 0 and seq % ck == 0 and seq % tq == 0
    nbits = max(1, int(math.ceil(math.log2(seq))))
    kern = functools.partial(_dsa_prompt_kernel, tq=tq, ck=ck, ksel=ksel, nbits=nbits)
    return pl.pallas_call(
        kern,
        grid=(nseq, nq),
        in_specs=[pl.BlockSpec((tq, dq), lambda s, i: (s * nq + i, 0)),
                  pl.BlockSpec((tq, dq), lambda s, i: (s * nq + i, 0)),
                  pl.BlockSpec((tq, LANES), lambda s, i: (s * nq + i, 0)),
                  pl.BlockSpec((seq, LANES), lambda s, i: (s, 0)),
                  pl.BlockSpec((seq, dk), lambda s, i: (s, 0)),
                  pl.BlockSpec((seq, dk), lambda s, i: (s, 0))],
        out_specs=pl.BlockSpec((tq, dq), lambda s, i: (s * nq + i, 0)),
        out_shape=jax.ShapeDtypeStruct((nseq * seq, dq), F32),
        scratch_shapes=[pltpu.VMEM((seq // ck, tq, ck), I32),
                        pltpu.VMEM((N_HEADS, tq, LANES), F32),
                        pltpu.VMEM((N_IDX_HEADS, tq, LANES), F32),
                        pltpu.VMEM((N_HEADS, tq, 1), F32),
                        pltpu.VMEM((N_HEADS, tq, 1), F32),
                        pltpu.VMEM((N_HEADS, tq, LANES), F32),
                        pltpu.VMEM((tq, LANES), I32),
                        pltpu.VMEM((tq, LANES), I32)],
        compiler_params=_cp(("arbitrary", "arbitrary")),
        name="dsa_prompt",
    )(qn, qir, smr, smr, kn, v)


def _dsa_sample_kernel(pt_ref, q8_ref, qi8_ref, wi_ref, kin_ref, kn_ref, vn_ref,
                       cki_hbm, ck_hbm, cv_hbm, o_ref, kib, kb, vb, sem,
                       *, n_pages, page, ksel, nbits):
    b = pl.program_id(0)
    past = n_pages * page

    def copies(p):
        pg = pt_ref[b, p]
        rows = pl.ds(pl.multiple_of(p * page, page), page)
        return (pltpu.make_async_copy(cki_hbm.at[pg], kib.at[rows, :], sem.at[0]),
                pltpu.make_async_copy(ck_hbm.at[pg], kb.at[rows, :], sem.at[1]),
                pltpu.make_async_copy(cv_hbm.at[pg], vb.at[rows, :], sem.at[2]))

    def start(p, carry):
        for cp in copies(p):
            cp.start()
        return carry

    def wait(p, carry):
        for cp in copies(p):
            cp.wait()
        return carry

    lax.fori_loop(0, n_pages, start, 0)
    lax.fori_loop(0, n_pages, wait, 0)

    wcol = wi_ref[0]
    qi8 = qi8_ref[0]
    s = _nt_dot(qi8, kib[...])
    sc = jnp.sum(jnp.maximum(s, 0.0) * wcol, axis=0, keepdims=True) * IDX_SCALE
    s_new = jnp.sum(qi8 * kin_ref[0], axis=1, keepdims=True)
    sc_new = jnp.sum(jnp.maximum(s_new, 0.0) * wcol, axis=0, keepdims=True) * IDX_SCALE
    keys = _sortable(sc)
    key_new = _sortable(sc_new)
    col = lax.broadcasted_iota(I32, (1, past), 1)

    def cnt(main, new):
        return jnp.sum(main, axis=1, keepdims=True) + new

    def bit_step(t, thr):
        cand = thr + lax.shift_left(jnp.int32(1), 31 - t)
        c = cnt(jnp.where(keys >= cand, 1, 0), jnp.where(key_new >= cand, 1, 0))
        return jnp.where(c >= ksel, cand, thr)

    thr = lax.fori_loop(0, 32, bit_step, jnp.full((1, 1), INT_MIN, I32))
    need = ksel - cnt(jnp.where(keys > thr, 1, 0), jnp.where(key_new > thr, 1, 0))

    def y_step(t, y):
        cand = y + lax.shift_left(jnp.int32(1), nbits - 1 - t)
        g = cnt(jnp.where(keys == thr, jnp.where(col < cand, 1, 0), 0),
                jnp.where(key_new == thr, jnp.where(past < cand, 1, 0), 0))
        return jnp.where(g < need, cand, y)

    y = lax.fori_loop(0, nbits, y_step, jnp.zeros((1, 1), I32))
    sel = jnp.where(keys > thr, 1, jnp.where(keys == thr, jnp.where(col <= y, 1, 0), 0))
    sel_new = jnp.where(key_new > thr, 1, jnp.where(key_new == thr, jnp.where(past <= y, 1, 0), 0))
    bias = jnp.where(sel > 0, 0.0, NEG_BIG)
    bias_new = jnp.where(sel_new > 0, 0.0, NEG_BIG)

    q8 = q8_ref[0] * (HEAD_DIM ** -0.5)
    sa = _nt_dot(q8, kb[...]) + bias
    sa_new = jnp.sum(q8 * kn_ref[0], axis=1, keepdims=True) + bias_new
    m = jnp.maximum(jnp.max(sa, axis=1, keepdims=True), sa_new)
    p = jnp.exp(sa - m)
    p_new = jnp.exp(sa_new - m)
    l = jnp.sum(p, axis=1, keepdims=True) + p_new
    o_ref[0] = (_dot(p, vb[...]) + p_new * vn_ref[0]) / l


def _dsa_sample(page_table, q8, qi8, wi, ki_new, k_new, v_new, cki, ck, cv):
    nb, n_pages = page_table.shape
    page = cki.shape[1]
    past = n_pages * page
    dk = N_KV_HEADS * HEAD_DIM
    ksel = min(TOP_K, (past + 1) // 4)
    nbits = int(math.floor(math.log2(past))) + 1
    kern = functools.partial(_dsa_sample_kernel, n_pages=n_pages, page=page, ksel=ksel, nbits=nbits)
    grid_spec = pltpu.PrefetchScalarGridSpec(
        num_scalar_prefetch=1,
        grid=(nb,),
        in_specs=[pl.BlockSpec((1, N_HEADS, dk), lambda b, pt: (b, 0, 0)),
                  pl.BlockSpec((1, N_IDX_HEADS, D_IDX), lambda b, pt: (b, 0, 0)),
                  pl.BlockSpec((1, N_IDX_HEADS, 1), lambda b, pt: (b, 0, 0)),
                  pl.BlockSpec((1, 1, D_IDX), lambda b, pt: (b, 0, 0)),
                  pl.BlockSpec((1, 1, dk), lambda b, pt: (b, 0, 0)),
                  pl.BlockSpec((1, 1, dk), lambda b, pt: (b, 0, 0)),
                  pl.BlockSpec(memory_space=pl.ANY),
                  pl.BlockSpec(memory_space=pl.ANY),
                  pl.BlockSpec(memory_space=pl.ANY)],
        out_specs=pl.BlockSpec((1, N_HEADS, dk), lambda b, pt: (b, 0, 0)),
        scratch_shapes=[pltpu.VMEM((past, D_IDX), F32),
                        pltpu.VMEM((past, dk), F32),
                        pltpu.VMEM((past, dk), F32),
                        pltpu.SemaphoreType.DMA((3,))],
    )
    return pl.pallas_call(
        kern,
        grid_spec=grid_spec,
        out_shape=jax.ShapeDtypeStruct((nb, N_HEADS, dk), F32),
        compiler_params=_cp(("arbitrary",)),
        name="dsa_sample",
    )(page_table, q8, qi8, wi, ki_new, k_new, v_new, cki, ck, cv)


def _page_copies(hbm, layer, pt_ref, b, buf, slot, sem, n_pages, page):
    return [pltpu.make_async_copy(hbm.at[layer, pt_ref[b, p]],
                                  buf.at[slot, :, p * page:(p + 1) * page], sem.at[slot])
            for p in range(n_pages)]


def _prefetch_pages(hbms, bufs, sems, layer, pt_ref, n_pages, page):
    b = pl.program_id(0)
    nb = pl.num_programs(0)
    slot = b % 2

    def start(bb, sl):
        for hbm, buf, sem in zip(hbms, bufs, sems):
            for cp in _page_copies(hbm, layer, pt_ref, bb, buf, sl, sem, n_pages, page):
                cp.start()

    @pl.when(b == 0)
    def _():
        start(0, 0)

    @pl.when(b + 1 < nb)
    def _():
        start(b + 1, 1 - slot)

    for hbm, buf, sem in zip(hbms, bufs, sems):
        for cp in _page_copies(hbm, layer, pt_ref, b, buf, slot, sem, n_pages, page):
            cp.wait()
    return slot


def _idx_sample_kernel(pt_ref, qi8_ref, wi_ref, kin_ref, cki_hbm, sc_ref, kibuf, sem,
                       *, layer, n_pages, page):
    past = n_pages * page
    slot = _prefetch_pages((cki_hbm,), (kibuf,), (sem,), layer, pt_ref, n_pages, page)
    wcol = wi_ref[0]
    qi8 = qi8_ref[0]
    s = _dot(qi8, kibuf[slot])
    sc_ref[0, :, 0:past] = jnp.sum(jnp.maximum(s, 0.0) * wcol, axis=0, keepdims=True) * IDX_SCALE
    s_new = jnp.sum(qi8 * kin_ref[0], axis=1, keepdims=True)
    sc_new = jnp.sum(jnp.maximum(s_new, 0.0) * wcol, axis=0, keepdims=True) * IDX_SCALE
    lane = lax.broadcasted_iota(I32, (1, LANES), 1)
    sc_ref[0, :, past:past + LANES] = jnp.where(lane == 0, sc_new, -jnp.inf)


def _idx_sample(page_table, qi8, wi, ki_new, cki_t, layer):
    nb, n_pages = page_table.shape
    page = cki_t.shape[3]
    past = n_pages * page
    kern = functools.partial(_idx_sample_kernel, layer=layer, n_pages=n_pages, page=page)
    grid_spec = pltpu.PrefetchScalarGridSpec(
        num_scalar_prefetch=1,
        grid=(nb,),
        in_specs=[pl.BlockSpec((1, N_IDX_HEADS, D_IDX), lambda b, pt: (b, 0, 0)),
                  pl.BlockSpec((1, N_IDX_HEADS, 1), lambda b, pt: (b, 0, 0)),
                  pl.BlockSpec((1, 1, D_IDX), lambda b, pt: (b, 0, 0)),
                  pl.BlockSpec(memory_space=pl.ANY)],
        out_specs=pl.BlockSpec((1, 1, past + LANES), lambda b, pt: (b, 0, 0)),
        scratch_shapes=[pltpu.VMEM((2, D_IDX, past), F32), pltpu.SemaphoreType.DMA((2,))],
    )
    return pl.pallas_call(
        kern, grid_spec=grid_spec,
        out_shape=jax.ShapeDtypeStruct((nb, 1, past + LANES), F32),
        compiler_params=_cp(("arbitrary",)),
        name="idx_sample",
    )(page_table, qi8, wi, ki_new, cki_t)


def _topk_bias_kernel(sc_ref, bias_ref, *, ksel, nbits):
    keys = _sortable(sc_ref[...])
    col = lax.broadcasted_iota(I32, keys.shape, 1)

    def cnt(hit):
        return jnp.sum(hit, axis=1, keepdims=True)

    def bit_step(t, thr):
        cand = thr + lax.shift_left(jnp.int32(1), 31 - t)
        return jnp.where(cnt(jnp.where(keys >= cand, 1, 0)) >= ksel, cand, thr)

    thr = lax.fori_loop(0, 32, bit_step, jnp.full((keys.shape[0], 1), INT_MIN, I32))
    need = ksel - cnt(jnp.where(keys > thr, 1, 0))

    def y_step(t, y):
        cand = y + lax.shift_left(jnp.int32(1), nbits - 1 - t)
        g = cnt(jnp.where(keys == thr, jnp.where(col < cand, 1, 0), 0))
        return jnp.where(g < need, cand, y)

    y = lax.fori_loop(0, nbits, y_step, jnp.zeros((keys.shape[0], 1), I32))
    sel = jnp.where(keys > thr, 1, jnp.where(keys == thr, jnp.where(col <= y, 1, 0), 0))
    bias_ref[...] = jnp.where(sel > 0, 0.0, NEG_BIG)


def _topk_bias(sc, ksel):
    nb, width = sc.shape
    nbits = int(math.floor(math.log2(width))) + 1
    kern = functools.partial(_topk_bias_kernel, ksel=ksel, nbits=nbits)
    return pl.pallas_call(
        kern, grid=(1,),
        in_specs=[pl.BlockSpec((nb, width), lambda i: (0, 0))],
        out_specs=pl.BlockSpec((nb, width), lambda i: (0, 0)),
        out_shape=jax.ShapeDtypeStruct((nb, width), F32),
        compiler_params=_cp(("arbitrary",)),
        name="topk_bias_sample",
    )(sc)


def _attn_sample_kernel(pt_ref, q8_ref, bias_ref, kn_ref, vn_ref, ck_hbm, cv_hbm, o_ref,
                        kbuf, vbuf, ksem, vsem, *, layer, n_pages, page):
    past = n_pages * page
    slot = _prefetch_pages((ck_hbm, cv_hbm), (kbuf, vbuf), (ksem, vsem), layer, pt_ref, n_pages, page)
    q8 = q8_ref[0] * (HEAD_DIM ** -0.5)
    sa = _dot(q8, kbuf[slot]) + bias_ref[0, :, 0:past]
    sa_new = jnp.sum(q8 * kn_ref[0], axis=1, keepdims=True) + bias_ref[0, :, past:past + 1]
    m = jnp.maximum(jnp.max(sa, axis=1, keepdims=True), sa_new)
    p = jnp.exp(sa - m)
    p_new = jnp.exp(sa_new - m)
    l = jnp.sum(p, axis=1, keepdims=True) + p_new
    o_ref[0] = (_nt_dot(p, vbuf[slot]) + p_new * vn_ref[0]) / l


def _attn_sample(page_table, q8, bias, k_new, v_new, ck_t, cv_t, layer):
    nb, n_pages = page_table.shape
    page = ck_t.shape[3]
    past = n_pages * page
    dk = N_KV_HEADS * HEAD_DIM
    kern = functools.partial(_attn_sample_kernel, layer=layer, n_pages=n_pages, page=page)
    grid_spec = pltpu.PrefetchScalarGridSpec(
        num_scalar_prefetch=1,
        grid=(nb,),
        in_specs=[pl.BlockSpec((1, N_HEADS, dk), lambda b, pt: (b, 0, 0)),
                  pl.BlockSpec((1, 1, past + LANES), lambda b, pt: (b, 0, 0)),
                  pl.BlockSpec((1, 1, dk), lambda b, pt: (b, 0, 0)),
                  pl.BlockSpec((1, 1, dk), lambda b, pt: (b, 0, 0)),
                  pl.BlockSpec(memory_space=pl.ANY),
                  pl.BlockSpec(memory_space=pl.ANY)],
        out_specs=pl.BlockSpec((1, N_HEADS, dk), lambda b, pt: (b, 0, 0)),
        scratch_shapes=[pltpu.VMEM((2, dk, past), F32), pltpu.VMEM((2, dk, past), F32),
                        pltpu.SemaphoreType.DMA((2,)), pltpu.SemaphoreType.DMA((2,))],
    )
    return pl.pallas_call(
        kern, grid_spec=grid_spec,
        out_shape=jax.ShapeDtypeStruct((nb, N_HEADS, dk), F32),
        compiler_params=_cp(("arbitrary",)),
        name="attn_sample",
    )(page_table, q8, bias, k_new, v_new, ck_t, cv_t)


def _ssd_prompt_kernel(xbc_ref, z_ref, sm_ref, cw_ref, cb_ref, dtb_ref, alog_ref, dskip_ref, ng_ref,
                       tril_ref, e64_ref, e128_ref, bmask_ref,
                       o_ref, cnew_ref, sst_ref, xbuf, st_ref, *, ts, kw, dxbc):
    halo = 8
    di = SSM_HEADS * SSM_HEAD_DIM
    dbc = SSM_GROUPS * D_STATE

    @pl.when(pl.program_id(1) == 0)
    def _():
        xbuf[0:halo, :] = jnp.zeros((halo, dxbc), F32)
        st_ref[...] = jnp.zeros(st_ref.shape, F32)

    xbuf[halo:halo + ts, :] = xbc_ref[...]
    first = halo - (kw - 1)
    conv = cb_ref[...] + cw_ref[0:1, :] * xbuf[first:first + ts, :]
    for j in range(1, kw):
        conv = conv + cw_ref[j:j + 1, :] * xbuf[first + j:first + j + ts, :]
    xc = _silu(conv)
    cnew_ref[0] = xbuf[ts + first:ts + halo, :]
    xbuf[0:halo, :] = xbuf[ts:ts + halo, :]

    lane = lax.broadcasted_iota(I32, (SSM_CHUNK, LANES), 1)
    head_lane = (lane[0:1, :] >= SM_DT) & (lane[0:1, :] < SM_DT + SSM_HEADS)
    a_row = jnp.where(head_lane, -jnp.exp(alog_ref[...]), 0.0)
    tri = lax.broadcasted_iota(I32, (SSM_CHUNK, SSM_CHUNK), 0) >= lax.broadcasted_iota(
        I32, (SSM_CHUNK, SSM_CHUNK), 1)
    glo = lane < D_STATE

    for k in range(ts // SSM_CHUNK):
        rows = slice(k * SSM_CHUNK, (k + 1) * SSM_CHUNK)
        dtf = _softplus(sm_ref[rows, :] + dtb_ref[...])
        adt = dtf * a_row
        a_cs = _sel_dot(tril_ref[...], adt)
        a_cs_t = a_cs.T
        acs_b = _dot_sel(a_cs, e128_ref[...])
        acs_f = _dot_sel(a_cs, e64_ref[...])
        dt_f = _dot_sel(dtf, e64_ref[...])
        alast_f = acs_f[SSM_CHUNK - 1:SSM_CHUNK, :]
        xs = xc[rows, 0:di]
        bm = xc[rows, di:di + dbc]
        cm = xc[rows, di + dbc:di + 2 * dbc]
        xdt = xs * dt_f
        xd = xdt * jnp.exp(alast_f - acs_f)
        bt = bm.T
        cb = (_dot(jnp.where(glo, cm, 0.0), bt), _dot(jnp.where(glo, 0.0, cm), bt))
        pairs = []
        for j in range(SSM_HEADS // 2):
            x2 = xdt[:, j * LANES:(j + 1) * LANES]
            acc = None
            for hh in range(2):
                h = 2 * j + hh
                seg = acs_b[:, h * LANES:(h + 1) * LANES] - a_cs_t[SM_DT + h:SM_DT + h + 1, :]
                lm = jnp.exp(jnp.where(tri, seg, -jnp.inf))
                sc = cb[h // (SSM_HEADS // SSM_GROUPS)] * lm
                xm = jnp.where(glo, x2, 0.0) if hh == 0 else jnp.where(glo, 0.0, x2)
                part = _dot(sc, xm)
                acc = part if acc is None else acc + part
            pairs.append(acc)
        y = jnp.concatenate(pairs, axis=1)
        y = y + _dot(cm, st_ref[...]) * jnp.exp(acs_f) + dskip_ref[...] * xs
        st_ref[...] = st_ref[...] * jnp.exp(alast_f) + bmask_ref[...] * _dot(bt, xd)
        yg = y * _silu(z_ref[rows, :])
        o_ref[rows, :] = _rms(yg) * ng_ref[...]
    sst_ref[0] = st_ref[...]


def _ssd_prompt(proj, cw, cb, dtb_row, alog_row, dskip_f, ng, consts, nseq, seq, ts):
    kw, dxbc = cw.shape
    di = SSM_HEADS * SSM_HEAD_DIM
    nt = seq // ts
    kern = functools.partial(_ssd_prompt_kernel, ts=ts, kw=kw, dxbc=dxbc)
    full = lambda a: pl.BlockSpec(a.shape, lambda s, i: (0,) * a.ndim)
    cs = (consts["tril"], consts["e64"], consts["e128"], consts["bmask"])
    return pl.pallas_call(
        kern,
        grid=(nseq, nt),
        in_specs=[pl.BlockSpec((ts, dxbc), lambda s, i: (s * nt + i, OFF["xbc"] // dxbc)),
                  pl.BlockSpec((ts, di), lambda s, i: (s * nt + i, OFF["z"] // di)),
                  pl.BlockSpec((ts, LANES), lambda s, i: (s * nt + i, OFF["small"] // LANES)),
                  full(cw), full(cb), full(dtb_row), full(alog_row), full(dskip_f), full(ng)]
                 + [full(c) for c in cs],
        out_specs=[pl.BlockSpec((ts, di), lambda s, i: (s * nt + i, 0)),
                   pl.BlockSpec((1, kw - 1, dxbc), lambda s, i: (s, 0, 0)),
                   pl.BlockSpec((1, SSM_GROUPS * D_STATE, di), lambda s, i: (s, 0, 0))],
        out_shape=[jax.ShapeDtypeStruct((nseq * seq, di), F32),
                   jax.ShapeDtypeStruct((nseq, kw - 1, dxbc), F32),
                   jax.ShapeDtypeStruct((nseq, SSM_GROUPS * D_STATE, di), F32)],
        scratch_shapes=[pltpu.VMEM((ts + 8, dxbc), F32),
                        pltpu.VMEM((SSM_GROUPS * D_STATE, di), F32)],
        compiler_params=_cp(("arbitrary", "arbitrary")),
        name="ssd_prompt",
    )(proj, proj, proj, cw, cb, dtb_row, alog_row, dskip_f, ng, *cs)


def _ssd_sample_kernel(xbc_ref, z_ref, sm_ref, st_ref, h0_ref, cw_ref, cb_ref, dtb_ref, alog_ref,
                       dskip_ref, ng_ref, e64_ref, o_ref, hn_ref, y_ref, *, nb, kw):
    di = SSM_HEADS * SSM_HEAD_DIM
    dbc = SSM_GROUPS * D_STATE
    conv = cb_ref[...] + cw_ref[kw - 1:kw, :] * xbc_ref[...]
    for j in range(kw - 1):
        conv = conv + cw_ref[j:j + 1, :] * st_ref[j]
    xc = _silu(conv)
    lane = lax.broadcasted_iota(I32, (1, LANES), 1)
    head_lane = (lane >= SM_DT) & (lane < SM_DT + SSM_HEADS)
    a_row = jnp.where(head_lane, -jnp.exp(alog_ref[...]), 0.0)
    dtf = _softplus(sm_ref[...] + dtb_ref[...])
    dec = jnp.exp(dtf * a_row)
    dt_f = _dot_sel(dtf, e64_ref[...])
    dec_f = _dot_sel(dec, e64_ref[...])
    xs = xc[:, 0:di]
    bm = xc[:, di:di + dbc]
    cm = xc[:, di + dbc:di + 2 * dbc]
    pad = jnp.zeros((LANES - nb, di), F32)
    xdt_t = jnp.concatenate([xs * dt_f, pad], axis=0).T
    dec_t = jnp.concatenate([dec_f, pad], axis=0).T
    bm_r = pltpu.roll(bm, D_STATE, 1)
    cm_r = pltpu.roll(cm, D_STATE, 1)
    rowi = lax.broadcasted_iota(I32, (di, D_STATE), 0)
    g0 = rowi < (SSM_HEADS // SSM_GROUPS) * SSM_HEAD_DIM
    lane_y = lax.broadcasted_iota(I32, (1, di), 1) < (SSM_HEADS // SSM_GROUPS) * SSM_HEAD_DIM
    row8 = lax.broadcasted_iota(I32, (8, D_STATE), 0)
    for b in range(nb):
        bsel = jnp.where(g0, bm[b:b + 1, 0:D_STATE], bm_r[b:b + 1, 0:D_STATE])
        hn = h0_ref[b] * dec_t[:, b:b + 1] + xdt_t[:, b:b + 1] * bsel
        hn_ref[b] = hn
        c2 = jnp.where(row8 == 0, cm[b:b + 1, 0:D_STATE],
                       jnp.where(row8 == 1, cm_r[b:b + 1, 0:D_STATE], 0.0))
        yr = _nt_dot(c2, hn)
        y_ref[b:b + 1, :] = jnp.where(lane_y, yr[0:1, :], yr[1:2, :])
    y = y_ref[...] + dskip_ref[...] * xs
    yg = y * _silu(z_ref[...])
    o_ref[...] = _rms(yg) * ng_ref[...]


def _ssd_sample(proj, st_t, h0, cw, cb, dtb_row, alog_row, dskip_f, ng, consts):
    kw, dxbc = cw.shape
    nb = proj.shape[0]
    di = SSM_HEADS * SSM_HEAD_DIM
    kern = functools.partial(_ssd_sample_kernel, nb=nb, kw=kw)
    full = lambda a: pl.BlockSpec(a.shape, lambda i: (0,) * a.ndim)
    return pl.pallas_call(
        kern,
        grid=(1,),
        in_specs=[pl.BlockSpec((nb, dxbc), lambda i: (0, OFF["xbc"] // dxbc)),
                  pl.BlockSpec((nb, di), lambda i: (0, OFF["z"] // di)),
                  pl.BlockSpec((nb, LANES), lambda i: (0, OFF["small"] // LANES)),
                  full(st_t), full(h0), full(cw), full(cb), full(dtb_row), full(alog_row),
                  full(dskip_f), full(ng), full(consts["e64"])],
        out_specs=[pl.BlockSpec((nb, di), lambda i: (0, 0)),
                   pl.BlockSpec(h0.shape, lambda i: (0, 0, 0))],
        out_shape=[jax.ShapeDtypeStruct((nb, di), F32), jax.ShapeDtypeStruct(h0.shape, F32)],
        scratch_shapes=[pltpu.VMEM((nb, di), F32)],
        compiler_params=_cp(("arbitrary",)),
        name="ssd_sample",
    )(proj, proj, proj, st_t, h0, cw, cb, dtb_row, alog_row, dskip_f, ng, consts["e64"])


def _mem_kv_kernel(x_ref, g_ref, w_ref, kg_ref, mk_ref, mv_ref):
    dm = mk_ref.shape[1]
    m = _dot(_rms(x_ref[...]) * g_ref[...], w_ref[...])
    for h in range(MEM_HEADS):
        sl = slice(h * MEM_HEAD_DIM, (h + 1) * MEM_HEAD_DIM)
        mk_ref[:, sl] = _rms(m[:, sl]) * kg_ref[...]
    mv_ref[...] = m[:, dm:]


def _mem_kv(mem2d, g, w, kg, rows):
    t, d = mem2d.shape
    dm = MEM_HEADS * MEM_HEAD_DIM
    return pl.pallas_call(
        _mem_kv_kernel,
        grid=(t // rows,),
        in_specs=[pl.BlockSpec((rows, d), lambda i: (i, 0)),
                  pl.BlockSpec((1, d), lambda i: (0, 0)),
                  pl.BlockSpec((d, 2 * dm), lambda i: (0, 0)),
                  pl.BlockSpec((1, MEM_HEAD_DIM), lambda i: (0, 0))],
        out_specs=[pl.BlockSpec((rows, dm), lambda i: (i, 0)),
                   pl.BlockSpec((rows, dm), lambda i: (i, 0))],
        out_shape=[jax.ShapeDtypeStruct((t, dm), F32), jax.ShapeDtypeStruct((t, dm), F32)],
        compiler_params=_cp(("parallel",)),
        name="mem_kv",
    )(mem2d, g, w, kg)


def _mem_attend_kernel(mq_ref, mk_ref, mv_ref, g_ref, o_ref, *, tm):
    mq = mq_ref[0]
    rows = max(tm, 8)
    if tm < rows:
        mq = jnp.broadcast_to(mq, (rows, mq.shape[1]))
    for h in range(MEM_HEADS):
        sl = slice(h * MEM_HEAD_DIM, (h + 1) * MEM_HEAD_DIM)
        qn = _rms(mq[:, sl]) * g_ref[...]
        s = _nt_dot(qn, mk_ref[0, :, sl]) * (MEM_HEAD_DIM ** -0.5)
        p = jnp.exp(s - jnp.max(s, axis=1, keepdims=True))
        o = _dot(p, mv_ref[0, :, sl]) / jnp.sum(p, axis=1, keepdims=True)
        o_ref[0, :, sl] = o[0:tm, :]


def _mem_attend(proj3, mk3, mv3, g, tm):
    nseq, seq, _ = proj3.shape
    nm = mk3.shape[1]
    dm = MEM_HEADS * MEM_HEAD_DIM
    kern = functools.partial(_mem_attend_kernel, tm=tm)
    return pl.pallas_call(
        kern,
        grid=(nseq, seq // tm),
        in_specs=[pl.BlockSpec((1, tm, dm), lambda s, i: (s, i, OFF["mq"] // dm)),
                  pl.BlockSpec((1, nm, dm), lambda s, i: (s, 0, 0)),
                  pl.BlockSpec((1, nm, dm), lambda s, i: (s, 0, 0)),
                  pl.BlockSpec((1, MEM_HEAD_DIM), lambda s, i: (0, 0))],
        out_specs=pl.BlockSpec((1, tm, dm), lambda s, i: (s, i, 0)),
        out_shape=jax.ShapeDtypeStruct((nseq, seq, dm), F32),
        compiler_params=_cp(("parallel", "arbitrary")),
        name="mem_attend",
    )(proj3, mk3, mv3, g)


def _merge_kernel(x_ref, ba_ref, bb_ref, bc_ref, bm_ref, gt_ref, wb_ref, wo_ref, o_ref):
    d = x_ref.shape[1]
    acc = None
    for n, br in enumerate((ba_ref, bb_ref, bc_ref, bm_ref)):
        term = jax.nn.sigmoid(gt_ref[:, n * d:(n + 1) * d]) * _dot(br[...].astype(MXU_DT), wb_ref[n])
        acc = term if acc is None else acc + term
    o_ref[...] = x_ref[...] + _dot(acc.astype(MXU_DT), wo_ref[...])


def _merge(x, proj, brs, wb, wo, tm):
    t, d = x.shape
    nbr, bw, _ = wb.shape
    return pl.pallas_call(
        _merge_kernel,
        grid=(t // tm,),
        in_specs=[pl.BlockSpec((tm, d), lambda i: (i, 0))]
                 + [pl.BlockSpec((tm, bw), lambda i: (i, 0))] * nbr
                 + [pl.BlockSpec((tm, nbr * d), lambda i: (i, OFF["gates"] // (nbr * d))),
                    pl.BlockSpec((nbr, bw, d), lambda i: (0, 0, 0)),
                    pl.BlockSpec((d, d), lambda i: (0, 0))],
        out_specs=pl.BlockSpec((tm, d), lambda i: (i, 0)),
        out_shape=jax.ShapeDtypeStruct((t, d), F32),
        compiler_params=_cp(("parallel",)),
        name="merge",
    )(x, *brs, proj, wb, wo)


def _ffn_prompt_kernel(x_ref, g_ref, wg_ref, wu_ref, cwg_ref, cwu_ref, cbg_ref, cbu_ref, wd_ref,
                       o_ref, unew_ref, h_ref, acc_ref, ubuf, carry, *, tm, tc, kw, nff):
    i = pl.program_id(1)
    c = pl.program_id(2)
    halo = 8
    first = halo - (kw - 1)

    @pl.when(c == 0)
    def _():
        h_ref[...] = (_rms(x_ref[...]) * g_ref[...]).astype(h_ref.dtype)
        acc_ref[...] = jnp.zeros(acc_ref.shape, F32)

    @pl.when(i == 0)
    def _():
        carry[c] = jnp.zeros((halo, 2 * tc), F32)

    h = h_ref[...]
    ubuf[0:halo, :] = carry[c]
    ubuf[halo:halo + tm, 0:tc] = _dot(h, wg_ref[...])
    ubuf[halo:halo + tm, tc:2 * tc] = _dot(h, wu_ref[...])
    carry[c] = ubuf[tm:tm + halo, :]
    fg =cbg_ref[...] + cwg_ref[0:1, :] * ubuf[first:first + tm, 0:tc]
    fu = cbu_ref[...] + cwu_ref[0:1, :] * ubuf[first:first + tm, tc:2 * tc]
    for j in range(1, kw):
        fg = fg + cwg_ref[j:j + 1, :] * ubuf[first + j:first + j + tm, 0:tc]
        fu = fu + cwu_ref[j:j + 1, :] * ubuf[first + j:first + j + tm, tc:2 * tc]
    acc_ref[...] += _dot((_silu(fg) * fu).astype(MXU_DT), wd_ref[...])

    @pl.when(c == nff - 1)
    def _():
        o_ref[...] = x_ref[...] + acc_ref[...]

    @pl.when((c == nff - 1) & (i == pl.num_programs(1) - 1))
    def _():
        for cc in range(nff):
            unew_ref[0, :, cc * tc:(cc + 1) * tc] = carry[cc, first:halo, 0:tc]
            unew_ref[0, :, (nff + cc) * tc:(nff + cc + 1) * tc] = carry[cc, first:halo, tc:2 * tc]


def _ffn_prompt(x, g, wup, cw, cb, wd, nseq, seq, tm, tc):
    t, d = x.shape
    dff = wd.shape[0]
    kw = cw.shape[0]
    nt = seq // tm
    nff = dff // tc
    kern = functools.partial(_ffn_prompt_kernel, tm=tm, tc=tc, kw=kw, nff=nff)
    return pl.pallas_call(
        kern,
        grid=(nseq, nt, nff),
        in_specs=[pl.BlockSpec((tm, d), lambda s, i, c: (s * nt + i, 0)),
                  pl.BlockSpec((1, d), lambda s, i, c: (0, 0)),
                  pl.BlockSpec((d, tc), lambda s, i, c: (0, c)),
                  pl.BlockSpec((d, tc), lambda s, i, c: (0, nff + c)),
                  pl.BlockSpec((kw, tc), lambda s, i, c: (0, c)),
                  pl.BlockSpec((kw, tc), lambda s, i, c: (0, nff + c)),
                  pl.BlockSpec((1, tc), lambda s, i, c: (0, c)),
                  pl.BlockSpec((1, tc), lambda s, i, c: (0, nff + c)),
                  pl.BlockSpec((tc, d), lambda s, i, c: (c, 0))],
        out_specs=[pl.BlockSpec((tm, d), lambda s, i, c: (s * nt + i, 0)),
                   pl.BlockSpec((1, kw - 1, 2 * dff), lambda s, i, c: (s, 0, 0))],
        out_shape=[jax.ShapeDtypeStruct((t, d), F32),
                   jax.ShapeDtypeStruct((nseq, kw - 1, 2 * dff), F32)],
        scratch_shapes=[pltpu.VMEM((tm, d), MXU_DT), pltpu.VMEM((tm, d), F32),
                        pltpu.VMEM((tm + 8, 2 * tc), F32), pltpu.VMEM((nff, 8, 2 * tc), F32)],
        compiler_params=_cp(("arbitrary", "arbitrary", "arbitrary")),
        name="ffn_prompt",
    )(x, g, wup, wup, cw, cw, cb, cb, wd)


def _ffn_sample_kernel(x_ref, g_ref, wg_ref, wu_ref, stg_ref, stu_ref, cwg_ref, cwu_ref, cbg_ref,
                       cbu_ref, wd_ref, o_ref, ug_ref, uu_ref, h_ref, acc_ref, *, kw):
    c = pl.program_id(0)

    @pl.when(c == 0)
    def _():
        h_ref[...] = (_rms(x_ref[...]) * g_ref[...]).astype(h_ref.dtype)
        acc_ref[...] = jnp.zeros(acc_ref.shape, F32)

    h = h_ref[...]
    ug = _dot(h, wg_ref[...])
    uu = _dot(h, wu_ref[...])
    ug_ref[...] = ug
    uu_ref[...] = uu
    fg = cbg_ref[...] + cwg_ref[kw - 1:kw, :] * ug
    fu = cbu_ref[...] + cwu_ref[kw - 1:kw, :] * uu
    for j in range(kw - 1):
        fg = fg + cwg_ref[j:j + 1, :] * stg_ref[j]
        fu = fu + cwu_ref[j:j + 1, :] * stu_ref[j]
    acc_ref[...] += _dot((_silu(fg) * fu).astype(MXU_DT), wd_ref[...])

    @pl.when(c == pl.num_programs(0) - 1)
    def _():
        o_ref[...] = x_ref[...] + acc_ref[...]


def _ffn_sample(x, g, wup, st_t, cw, cb, wd, tc):
    t, d = x.shape
    dff = wd.shape[0]
    kw = cw.shape[0]
    nff = dff // tc
    kern = functools.partial(_ffn_sample_kernel, kw=kw)
    return pl.pallas_call(
        kern,
        grid=(nff,),
        in_specs=[pl.BlockSpec((t, d), lambda c: (0, 0)),
                  pl.BlockSpec((1, d), lambda c: (0, 0)),
                  pl.BlockSpec((d, tc), lambda c: (0, c)),
                  pl.BlockSpec((d, tc), lambda c: (0, nff + c)),
                  pl.BlockSpec((kw - 1, t, tc), lambda c: (0, 0, c)),
                  pl.BlockSpec((kw - 1, t, tc), lambda c: (0, 0, nff + c)),
                  pl.BlockSpec((kw, tc), lambda c: (0, c)),
                  pl.BlockSpec((kw, tc), lambda c: (0, nff + c)),
                  pl.BlockSpec((1, tc), lambda c: (0, c)),
                  pl.BlockSpec((1, tc), lambda c: (0, nff + c)),
                  pl.BlockSpec((tc, d), lambda c: (c, 0))],
        out_specs=[pl.BlockSpec((t, d), lambda c: (0, 0)),
                   pl.BlockSpec((t, tc), lambda c: (0, c)),
                   pl.BlockSpec((t, tc), lambda c: (0, c))],
        out_shape=[jax.ShapeDtypeStruct((t, d), F32), jax.ShapeDtypeStruct((t, dff), F32),
                   jax.ShapeDtypeStruct((t, dff), F32)],
        scratch_shapes=[pltpu.VMEM((t, d), MXU_DT), pltpu.VMEM((t, d), F32)],
        compiler_params=_cp(("arbitrary",)),
        name="ffn_sample",
    )(x, g, wup, wup, st_t, st_t, cw, cw, cb, cb, wd)


def _constants():
    lane = np.arange(LANES)
    r = lane % HEAD_DIM
    rot = HEAD_DIM // 4
    half = rot // 2
    inv_freq = ROPE_THETA ** (-jnp.arange(half, dtype=F32) * (2.0 / rot))
    in_rot = r < rot
    invf = jnp.where(jnp.asarray(in_rot), inv_freq[jnp.asarray(r % half)], 0.0).astype(F32)[None, :]
    sgn = np.where(r < half, -1.0, np.where(in_rot, 1.0, 0.0)).astype(np.float32)[None, :]
    first_head = (lane < HEAD_DIM)[None, :]
    dq = N_HEADS * HEAD_DIM
    bd = (np.arange(dq)[:, None] // HEAD_DIM == np.arange(dq)[None, :] // HEAD_DIM)
    di = SSM_HEADS * SSM_HEAD_DIM
    e64 = np.zeros((LANES, di), np.float32)
    e128 = np.zeros((LANES, SSM_HEADS * LANES), np.float32)
    for h in range(SSM_HEADS):
        e64[SM_DT + h, h * SSM_HEAD_DIM:(h + 1) * SSM_HEAD_DIM] = 1.0
        e128[SM_DT + h, h * LANES:(h + 1) * LANES] = 1.0
    tril = np.tril(np.ones((SSM_CHUNK, SSM_CHUNK), np.float32))
    hpg = SSM_HEADS // SSM_GROUPS
    bmask = (np.arange(SSM_GROUPS * D_STATE)[:, None] // D_STATE
             == np.arange(di)[None, :] // (hpg * SSM_HEAD_DIM)).astype(np.float32)
    return dict(
        invf=invf, sgn=jnp.asarray(sgn),
        invf_s=jnp.where(jnp.asarray(first_head), invf, 0.0),
        sgn_s=jnp.asarray(np.where(first_head, sgn, 0.0).astype(np.float32)),
        bd=jnp.asarray(bd.astype(np.float32), dtype=BF16),
        e64=jnp.asarray(e64, dtype=BF16), e128=jnp.asarray(e128, dtype=BF16),
        tril=jnp.asarray(tril, dtype=BF16), bmask=jnp.asarray(bmask))


def _reorder_w_in(w_in):
    d = w_in.shape[0]
    sizes = dict(glu=1024, q=512, k=128, v=128, qi=512, ki=64, wi=8, z=512, xbc=768, dt=8, mq=512,
                 gates=4096)
    order_in = ["glu", "q", "k", "v", "qi", "ki", "wi", "z", "xbc", "dt", "mq", "gates"]
    parts, off = {}, 0
    for name in order_in:
        parts[name] = w_in[:, off:off + sizes[name]]
        off += sizes[name]
    assert off == w_in.shape[1]
    pad = jnp.zeros((d, LANES - sizes["ki"] - sizes["wi"] - sizes["dt"]), w_in.dtype)
    out = jnp.concatenate([parts[n] for n in ("glu", "q", "qi", "z", "mq", "xbc", "k", "v", "gates",
                                              "ki", "wi", "dt")] + [pad], axis=1)
    assert out.shape[1] == PROJ_COLS
    return out


def _pad_lanes(v, start):
    return jnp.zeros((1, LANES), F32).at[0, start:start + v.shape[0]].set(v)


def _tile(n, cap):
    return min(n, cap)


def _layer_params(l, prm, consts):
    p = dict(
        w_in=_reorder_w_in(prm["w_in"][l]).astype(MXU_DT),
        norm_mix_g=prm["norm_mix_g"][l][None, :],
        conv_a_w=prm["conv_a_w"][l], conv_a_b=prm["conv_a_b"][l][None, :],
        ln_a_g=prm["ln_a_g"][l][None, :], ln_a_b=prm["ln_a_b"][l][None, :],
        qg=jnp.tile(prm["q_norm_g"][l], N_HEADS)[None, :],
        kg=jnp.tile(prm["k_norm_g"][l], N_KV_HEADS)[None, :],
        ssm_conv_w=prm["ssm_conv_w"][l], ssm_conv_b=prm["ssm_conv_b"][l][None, :],
        dtb_row=_pad_lanes(prm["dt_bias"][l], SM_DT), alog_row=_pad_lanes(prm["a_log"][l], SM_DT),
        dskip_f=jnp.repeat(prm["d_skip"][l], SSM_HEAD_DIM)[None, :],
        ssm_norm_g=prm["ssm_norm_g"][l][None, :],
        mem_norm_g=prm["mem_norm_g"][l][None, :], w_mem_kv=prm["w_mem_kv"][l],
        mq_norm_g=prm["mq_norm_g"][l][None, :], mk_norm_g=prm["mk_norm_g"][l][None, :],
        w_branch=prm["w_branch"][l].astype(MXU_DT), w_out=prm["w_out"][l].astype(MXU_DT),
        norm_ffn_g=prm["norm_ffn_g"][l][None, :], w_ffn_up=prm["w_ffn_up"][l].astype(MXU_DT),
        ffn_conv_w=prm["ffn_conv_w"][l], ffn_conv_b=prm["ffn_conv_b"][l][None, :],
        w_ffn_down=prm["w_ffn_down"][l].astype(MXU_DT))
    return p


def _ssm_state_from_slab(slab):
    nseq = slab.shape[0]
    hpg = SSM_HEADS // SSM_GROUPS
    s = slab.reshape(nseq, SSM_GROUPS, D_STATE, SSM_HEADS, SSM_HEAD_DIM)
    per_head = [s[:, h // hpg, :, h, :] for h in range(SSM_HEADS)]
    return jnp.swapaxes(jnp.stack(per_head, axis=1), 2, 3)


def _prompt_layer(x, p, consts, mem2d, nseq, seq):
    t, d = x.shape
    nm = mem2d.shape[0] // nseq
    proj = _norm_proj(x, p["norm_mix_g"], p["w_in"], _tile(t, 1024), PROJ_COLS // 5)
    br_a, conf_new = _conf_prompt(proj, p["conv_a_w"], p["conv_a_b"], p["ln_a_g"], p["ln_a_b"],
                                  nseq, seq, _tile(seq, 512))
    ck = _tile(seq, 512)
    kn, v, kir, knb, smb, qt, qit, vt, smt = _dsa_prep_t(proj, p["qg"], p["kg"], consts, nseq, seq, ck)
    br_b = _dsa_prompt_t(qt, qit, smt, smb, knb, vt, nseq, seq, _tile(seq, 256), ck)
    br_c, sconv_new, sslab = _ssd_prompt(proj, p["ssm_conv_w"], p["ssm_conv_b"], p["dtb_row"],
                                         p["alog_row"], p["dskip_f"], p["ssm_norm_g"], consts,
                                         nseq, seq, _tile(seq, 512))
    mk, mv = _mem_kv(mem2d, p["mem_norm_g"], p["w_mem_kv"], p["mk_norm_g"], nm)
    dm = MEM_HEADS * MEM_HEAD_DIM
    br_m = _mem_attend(proj.reshape(nseq, seq, PROJ_COLS), mk.reshape(nseq, nm, dm),
                       mv.reshape(nseq, nm, dm), p["mq_norm_g"], _tile(seq, 512)).reshape(t, dm)
    x = _merge(x, proj, (br_a, br_b, br_c, br_m), p["w_branch"], p["w_out"], _tile(t, 256))
    x, ffn_new = _ffn_prompt(x, p["norm_ffn_g"], p["w_ffn_up"], p["ffn_conv_w"], p["ffn_conv_b"],
                             p["w_ffn_down"], nseq, seq, _tile(seq, 512), p["w_ffn_down"].shape[0] // 2)
    to_tok = lambda a: jnp.transpose(a.reshape(nseq, N_KV_HEADS, HEAD_DIM, seq), (0, 3, 1, 2))
    state = (to_tok(kn), to_tok(v), jnp.transpose(kir, (0, 2, 1)),
             mk.reshape(nseq, nm, MEM_HEADS, MEM_HEAD_DIM), mv.reshape(nseq, nm, MEM_HEADS, MEM_HEAD_DIM),
             conf_new, sconv_new, _ssm_state_from_slab(sslab), ffn_new)
    return x, state


def _sample_layer(x, p, consts, layer, page_table, ck_t, cv_t, cki_t, cmk, cmv, st_conf, st_sconv, st_ssm,
                  st_ffn):
    nb, d = x.shape
    n_pages = page_table.shape[1]
    page = ck_t.shape[3]
    past = n_pages * page
    dq = N_HEADS * HEAD_DIM
    dk = N_KV_HEADS * HEAD_DIM
    proj = _norm_proj(x, p["norm_mix_g"], p["w_in"], nb, PROJ_COLS // 5)
    br_a, a_new = _conf_sample(proj, jnp.swapaxes(st_conf, 0, 1), p["conv_a_w"], p["conv_a_b"],
                               p["ln_a_g"], p["ln_a_b"])
    qn, kn, v, qir, smr, kir = _dsa_prep(proj, p["qg"], p["kg"], consts, nb, 1, float(past))
    hpk = N_HEADS // N_KV_HEADS
    qh = qn.reshape(nb, N_HEADS, HEAD_DIM)
    grp = (np.arange(N_HEADS)[:, None] // hpk == np.arange(dk)[None, :] // HEAD_DIM)
    q8 = jnp.where(jnp.asarray(grp)[None], jnp.tile(qh, (1, 1, N_KV_HEADS)), 0.0)
    sc = _idx_sample(page_table, qir.reshape(nb, N_IDX_HEADS, D_IDX),
                     smr[:, SM_WI:SM_WI + N_IDX_HEADS].reshape(nb, N_IDX_HEADS, 1),
                     kir.reshape(nb, 1, D_IDX), cki_t, layer)
    bias = _topk_bias(sc.reshape(nb, past + LANES), min(TOP_K, (past + 1) // 4))
    o8 = _attn_sample(page_table, q8, bias.reshape(nb, 1, past + LANES), kn.reshape(nb, 1, dk),
                      v.reshape(nb, 1, dk), ck_t, cv_t, layer)
    o8 = o8.reshape(nb, N_HEADS, N_KV_HEADS, HEAD_DIM)
    br_b = jnp.stack([o8[:, h, h // hpk, :] for h in range(N_HEADS)], axis=1).reshape(nb, dq)
    h0 = st_ssm.reshape(nb, SSM_HEADS * SSM_HEAD_DIM, D_STATE)
    br_c, h_new = _ssd_sample(proj, jnp.swapaxes(st_sconv, 0, 1), h0, p["ssm_conv_w"], p["ssm_conv_b"],
                              p["dtb_row"], p["alog_row"], p["dskip_f"], p["ssm_norm_g"], consts)
    nm = cmk.shape[1]
    dm = MEM_HEADS * MEM_HEAD_DIM
    br_m = _mem_attend(proj.reshape(nb, 1, PROJ_COLS), cmk.reshape(nb, nm, dm), cmv.reshape(nb, nm, dm),
                       p["mq_norm_g"], 1).reshape(nb, dm)
    x = _merge(x, proj, (br_a, br_b, br_c, br_m), p["w_branch"], p["w_out"], nb)
    x, ug, uu = _ffn_sample(x, p["norm_ffn_g"], p["w_ffn_up"], jnp.swapaxes(st_ffn, 0, 1),
                            p["ffn_conv_w"], p["ffn_conv_b"], p["w_ffn_down"], 256)
    xbc_raw = proj[:, OFF["xbc"]:OFF["xbc"] + st_sconv.shape[-1]]
    state = (kn.reshape(nb, 1, N_KV_HEADS, HEAD_DIM), v.reshape(nb, 1, N_KV_HEADS, HEAD_DIM),
             kir.reshape(nb, 1, D_IDX),
             jnp.concatenate([st_conf[:, 1:], a_new[:, None]], axis=1),
             jnp.concatenate([st_sconv[:, 1:], xbc_raw[:, None]], axis=1),
             h_new.reshape(st_ssm.shape),
             jnp.concatenate([st_ffn[:, 1:], jnp.concatenate([ug, uu], axis=-1)[:, None]], axis=1))
    return x, state


def kernel(x_prompt, x_sample, cache_k, cache_v, cache_kidx, cache_mem_k, cache_mem_v, state_conformer, state_ssm_conv, state_ssm, state_ffn_conv, page_table, mem_prompt, norm_mix_g, w_in, conv_a_w, conv_a_b, ln_a_g, ln_a_b, q_norm_g, k_norm_g, ssm_conv_w, ssm_conv_b, dt_bias, a_log, d_skip, ssm_norm_g, mem_norm_g, w_mem_kv, mq_norm_g, mk_norm_g, w_branch, w_out, norm_ffn_g, w_ffn_up, ffn_conv_w, ffn_conv_b, w_ffn_down):
    prm = dict(norm_mix_g=norm_mix_g, w_in=w_in, conv_a_w=conv_a_w, conv_a_b=conv_a_b, ln_a_g=ln_a_g,
               ln_a_b=ln_a_b, q_norm_g=q_norm_g, k_norm_g=k_norm_g, ssm_conv_w=ssm_conv_w,
               ssm_conv_b=ssm_conv_b, dt_bias=dt_bias, a_log=a_log, d_skip=d_skip, ssm_norm_g=ssm_norm_g,
               mem_norm_g=mem_norm_g, w_mem_kv=w_mem_kv, mq_norm_g=mq_norm_g, mk_norm_g=mk_norm_g,
               w_branch=w_branch, w_out=w_out, norm_ffn_g=norm_ffn_g, w_ffn_up=w_ffn_up,
               ffn_conv_w=ffn_conv_w, ffn_conv_b=ffn_conv_b, w_ffn_down=w_ffn_down)
    depth = w_in.shape[0]
    nseq, seq, d = x_prompt.shape
    nb, dseq, _ = x_sample.shape
    assert dseq == 1
    consts = _constants()
    xp = x_prompt.reshape(nseq * seq, d)
    xs = x_sample.reshape(nb, d)
    mem2d = mem_prompt.reshape(nseq * mem_prompt.shape[1], d)
    n_phys, page = cache_k.shape[1], cache_k.shape[2]
    ck_t = jnp.transpose(cache_k, (0, 1, 3, 4, 2)).reshape(depth, n_phys, N_KV_HEADS * HEAD_DIM, page)
    cv_t = jnp.transpose(cache_v, (0, 1, 3, 4, 2)).reshape(depth, n_phys, N_KV_HEADS * HEAD_DIM, page)
    cki_t = jnp.transpose(cache_kidx, (0, 1, 3, 2))
    p_states, s_states = [], []
    for l in range(depth):
        p = _layer_params(l, prm, consts)
        xp, st = _prompt_layer(xp, p, consts, mem2d, nseq, seq)
        p_states.append(st)
        xs, st = _sample_layer(xs, p, consts, l, page_table, ck_t, cv_t, cki_t,
                               cache_mem_k[l], cache_mem_v[l], state_conformer[l], state_ssm_conv[l],
                               state_ssm[l], state_ffn_conv[l])
        s_states.append(st)
    stack = lambda states, k: jnp.stack([s[k] for s in states])
    return ((xp.reshape(nseq, seq, d), xs.reshape(nb, 1, d))
            + tuple(stack(p_states, k) for k in range(9))
            + tuple(stack(s_states, k) for k in range(7)))
```

```python
import functools
import math

import numpy as np
import jax
import jax.numpy as jnp
from jax import lax
from jax.experimental import pallas as pl
from jax.experimental.pallas import tpu as pltpu

F32 = jnp.float32
BF16 = jnp.bfloat16
I32 = jnp.int32
MXU_DT = BF16

EPS = 1e-6
ROPE_THETA = 500000.0
LANES = 128
V7X_VMEM_LIMIT = 56 * 1024 * 1024

N_HEADS = 8
HEAD_DIM = 64
N_KV_HEADS = 2
N_IDX_HEADS = 8
D_IDX = 64
TOP_K = 256
SSM_HEADS = 8
SSM_HEAD_DIM = 64
SSM_GROUPS = 2
D_STATE = 64
SSM_CHUNK = 128
MEM_HEADS = 4
MEM_HEAD_DIM = 128
IDX_SCALE = (D_IDX ** -0.5) * (N_IDX_HEADS ** -0.5)
NEG_BIG = -1e30
LOG2E = math.log2(math.e)
INT_MIN = -2 ** 31

SM_KI = 0
SM_WI = 64
SM_DT = 72

OFF = dict(glu=0, q=1024, qi=1536, z=2048, mq=2560, xbc=3072, k=3840, v=3968, gates=4096, small=8192)
PROJ_COLS = 8320


def _cp(sem):
    return pltpu.CompilerParams(dimension_semantics=sem, vmem_limit_bytes=V7X_VMEM_LIMIT)


def _nt_dot(a, b):
    return lax.dot_general(a, b, (((1,), (1,)), ((), ())), preferred_element_type=F32)


def _dot(a, b):
    return jnp.dot(a, b, preferred_element_type=F32)


def _split3(x):
    hi = x.astype(BF16)
    r = x - hi.astype(F32)
    mid = r.astype(BF16)
    lo = (r - mid.astype(F32)).astype(BF16)
    return hi, mid, lo


def _dot_sel(x, e):
    hi, mid, lo = _split3(x)
    return _dot(hi, e) + _dot(mid, e) + _dot(lo, e)


def _sel_dot(e, x):
    hi, mid, lo = _split3(x)
    return _dot(e, hi) + _dot(e, mid) + _dot(e, lo)


def _silu(x):
    return x * jax.nn.sigmoid(x)


def _softplus(x):
    return jnp.maximum(x, 0.0) + jnp.log1p(jnp.exp(-jnp.abs(x)))


def _rms(x):
    return x * lax.rsqrt(jnp.mean(x * x, axis=-1, keepdims=True) + EPS)


def _sortable(x):
    b = lax.bitcast_convert_type(x, I32)
    b = jnp.where(x == 0.0, 0, b)
    return jnp.where(b < 0, b ^ 0x7FFFFFFF, b)


def _norm_proj_kernel(x_ref, g_ref, w_ref, o_ref, h_ref):
    @pl.when(pl.program_id(1) == 0)
    def _():
        h_ref[...] = (_rms(x_ref[...]) * g_ref[...]).astype(h_ref.dtype)

    o_ref[...] = _dot(h_ref[...], w_ref[...])


def _norm_proj(x, g, w, tm, tn):
    t, d = x.shape
    n = w.shape[1]
    return pl.pallas_call(
        _norm_proj_kernel,
        grid=(t // tm, n // tn),
        in_specs=[pl.BlockSpec((tm, d), lambda i, j: (i, 0)),
                  pl.BlockSpec((1, d), lambda i, j: (0, 0)),
                  pl.BlockSpec((d, tn), lambda i, j: (0, j))],
        out_specs=pl.BlockSpec((tm, tn), lambda i, j: (i, j)),
        out_shape=jax.ShapeDtypeStruct((t, n), F32),
        scratch_shapes=[pltpu.VMEM((tm, d), MXU_DT)],
        compiler_params=_cp(("parallel", "arbitrary")),
        name="norm_proj",
    )(x, g, w)


def _layernorm_silu(u, g, b):
    xc = u - jnp.mean(u, axis=-1, keepdims=True)
    var = jnp.mean(xc * xc, axis=-1, keepdims=True)
    return _silu(xc * lax.rsqrt(var + EPS) * g + b)


def _conf_prompt_kernel(glu_ref, w_ref, b_ref, lg_ref, lb_ref, o_ref, cnew_ref, abuf, *, tm, cw, dc):
    halo = 32

    @pl.when(pl.program_id(1) == 0)
    def _():
        abuf[0:halo, :] = jnp.zeros((halo, dc), F32)

    glu = glu_ref[...]
    abuf[halo:halo + tm, :] = glu[:, :dc] * jax.nn.sigmoid(glu[:, dc:])
    first = halo - (cw - 1)
    u = b_ref[...] + w_ref[0:1, :] * abuf[first:first + tm, :]
    for j in range(1, cw):
        u = u + w_ref[j:j + 1, :] * abuf[first + j:first + j + tm, :]
    o_ref[...] = _layernorm_silu(u, lg_ref[...], lb_ref[...])
    cnew_ref[0] = abuf[tm + first:tm + halo, :]
    abuf[0:halo, :] = abuf[tm:tm + halo, :]


def _conf_prompt(proj, w, b, lg, lb, nseq, seq, tm):
    cw, dc = w.shape
    nt = seq // tm
    kern = functools.partial(_conf_prompt_kernel, tm=tm, cw=cw, dc=dc)
    return pl.pallas_call(
        kern,
        grid=(nseq, nt),
        in_specs=[pl.BlockSpec((tm, 2 * dc), lambda s, i: (s * nt + i, OFF["glu"] // (2 * dc))),
                  pl.BlockSpec((cw, dc), lambda s, i: (0, 0)),
                  pl.BlockSpec((1, dc), lambda s, i: (0, 0)),
                  pl.BlockSpec((1, dc), lambda s, i: (0, 0)),
                  pl.BlockSpec((1, dc), lambda s, i: (0, 0))],
        out_specs=[pl.BlockSpec((tm, dc), lambda s, i: (s * nt + i, 0)),
                   pl.BlockSpec((1, cw - 1, dc), lambda s, i: (s, 0, 0))],
        out_shape=[jax.ShapeDtypeStruct((nseq * seq, dc), F32),
                   jax.ShapeDtypeStruct((nseq, cw - 1, dc), F32)],
        scratch_shapes=[pltpu.VMEM((tm + 32, dc), F32)],
        compiler_params=_cp(("arbitrary", "arbitrary")),
        name="conformer_prompt",
    )(proj, w, b, lg, lb)


def _conf_sample_kernel(glu_ref, st_ref, w_ref, b_ref, lg_ref, lb_ref, o_ref, a_ref, *, cw, dc):
    glu = glu_ref[...]
    a = glu[:, :dc] * jax.nn.sigmoid(glu[:, dc:])
    u = b_ref[...] + w_ref[cw - 1:cw, :] * a
    for j in range(cw - 1):
        u = u + w_ref[j:j + 1, :] * st_ref[j]
    o_ref[...] = _layernorm_silu(u, lg_ref[...], lb_ref[...])
    a_ref[...] = a


def _conf_sample(proj, st_t, w, b, lg, lb):
    cw, dc = w.shape
    n = proj.shape[0]
    kern = functools.partial(_conf_sample_kernel, cw=cw, dc=dc)
    return pl.pallas_call(
        kern,
        grid=(1,),
        in_specs=[pl.BlockSpec((n, 2 * dc), lambda i: (0, OFF["glu"] // (2 * dc))),
                  pl.BlockSpec((cw - 1, n, dc), lambda i: (0, 0, 0)),
                  pl.BlockSpec((cw, dc), lambda i: (0, 0)),
                  pl.BlockSpec((1, dc), lambda i: (0, 0)),
                  pl.BlockSpec((1, dc), lambda i: (0, 0)),
                  pl.BlockSpec((1, dc), lambda i: (0, 0))],
        out_specs=[pl.BlockSpec((n, dc), lambda i: (0, 0)),
                   pl.BlockSpec((n, dc), lambda i: (0, 0))],
        out_shape=[jax.ShapeDtypeStruct((n, dc), F32), jax.ShapeDtypeStruct((n, dc), F32)],
        compiler_params=_cp(("arbitrary",)),
        name="conformer_sample",
    )(proj, st_t, w, b, lg, lb)


def _rope128(x, c, s):
    lane = lax.broadcasted_iota(I32, x.shape, 1) % HEAD_DIM
    partner = jnp.where(lane < 8, pltpu.roll(x, LANES - 8, 1), pltpu.roll(x, 8, 1))
    return x * c + partner * s


def _dsa_prep_kernel(q_ref, qi_ref, k_ref, v_ref, sm_ref, qg_ref, kg_ref, invf_ref, sgn_ref,
                     invf_s_ref, sgn_s_ref, bd_ref,
                     qn_ref, kn_ref, vo_ref, qir_ref, smr_ref, kir_ref, *, tm, seq, const_pos):
    if const_pos is None:
        base = (pl.program_id(0) * tm) % seq
        pos = (base + lax.broadcasted_iota(I32, (tm, LANES), 0)).astype(F32)
    else:
        pos = jnp.full((tm, LANES), const_pos, F32)
    ang = pos * invf_ref[...]
    c = jnp.cos(ang)
    s = jnp.sin(ang) * sgn_ref[...]
    ang_s = pos * invf_s_ref[...]
    c_s = jnp.cos(ang_s)
    s_s = jnp.sin(ang_s) * sgn_s_ref[...]

    def seg_rms(x, bd):
        x2 = x * x
        hi = x2.astype(BF16)
        lo = (x2 - hi.astype(F32)).astype(BF16)
        ms = (_dot(hi, bd) + _dot(lo, bd)) * (1.0 / HEAD_DIM)
        return x * lax.rsqrt(ms + EPS)

    qn = seg_rms(q_ref[...], bd_ref[...]) * qg_ref[...]
    qi = qi_ref[...]
    for j in range(q_ref.shape[1] // LANES):
        sl = slice(j * LANES, (j + 1) * LANES)
        qn_ref[:, sl] = _rope128(qn[:, sl], c, s)
        qir_ref[:, sl] = _rope128(qi[:, sl], c, s)
    kn = seg_rms(k_ref[...], bd_ref[0:LANES, 0:LANES]) * kg_ref[...]
    kn_ref[...] = _rope128(kn, c, s)
    vo_ref[...] = v_ref[...]
    smr = _rope128(sm_ref[...], c_s, s_s)
    smr_ref[...] = smr
    kir_ref[...] = smr[:, 0:D_IDX]


def _dsa_prep(proj, qg, kg, consts, tm, seq, const_pos):
    t = proj.shape[0]
    dq = N_HEADS * HEAD_DIM
    dk = N_KV_HEADS * HEAD_DIM
    kern = functools.partial(_dsa_prep_kernel, tm=tm, seq=seq, const_pos=const_pos)
    row = lambda w: pl.BlockSpec((1, w), lambda i: (0, 0))
    return pl.pallas_call(
        kern,
        grid=(t // tm,),
        in_specs=[pl.BlockSpec((tm, dq), lambda i: (i, OFF["q"] // dq)),
                  pl.BlockSpec((tm, dq), lambda i: (i, OFF["qi"] // dq)),
                  pl.BlockSpec((tm, dk), lambda i: (i, OFF["k"] // dk)),
                  pl.BlockSpec((tm, dk), lambda i: (i, OFF["v"] // dk)),
                  pl.BlockSpec((tm, LANES), lambda i: (i, OFF["small"] // LANES)),
                  row(dq), row(dk), row(LANES), row(LANES), row(LANES), row(LANES),
                  pl.BlockSpec((dq, dq), lambda i: (0, 0))],
        out_specs=[pl.BlockSpec((tm, dq), lambda i: (i, 0)),
                   pl.BlockSpec((tm, dk), lambda i: (i, 0)),
                   pl.BlockSpec((tm, dk), lambda i: (i, 0)),
                   pl.BlockSpec((tm, dq), lambda i: (i, 0)),
                   pl.BlockSpec((tm, LANES), lambda i: (i, 0)),
                   pl.BlockSpec((tm, D_IDX), lambda i: (i, 0))],
        out_shape=[jax.ShapeDtypeStruct((t, dq), F32), jax.ShapeDtypeStruct((t, dk), F32),
                   jax.ShapeDtypeStruct((t, dk), F32), jax.ShapeDtypeStruct((t, dq), F32),
                   jax.ShapeDtypeStruct((t, LANES), F32), jax.ShapeDtypeStruct((t, D_IDX), F32)],
        compiler_params=_cp(("parallel",)),
        name="dsa_prep",
    )(proj, proj, proj, proj, proj, qg, kg, consts["invf"], consts["sgn"], consts["invf_s"],
      consts["sgn_s"], consts["bd"])


def _dsa_prep_t_kernel(q_ref, qi_ref, k_ref, v_ref, sm_ref, qg_ref, kg_ref, invf_ref, sgn_ref,
                       invf_s_ref, sgn_s_ref, bd_ref,
                       kn_ref, vo_ref, kir_ref, knb_ref, smb_ref, qt_ref, qit_ref, vt_ref, smt_ref, *, tm):
    pos = (pl.program_id(1) * tm + lax.broadcasted_iota(I32, (tm, LANES), 0)).astype(F32)
    ang = pos * invf_ref[...]
    c = jnp.cos(ang)
    s = jnp.sin(ang) * sgn_ref[...]
    ang_s = pos * invf_s_ref[...]
    c_s = jnp.cos(ang_s)
    s_s = jnp.sin(ang_s) * sgn_s_ref[...]

    def seg_rms(x, bd):
        x2 = x * x
        hi = x2.astype(BF16)
        lo = (x2 - hi.astype(F32)).astype(BF16)
        ms = (_dot(hi, bd) + _dot(lo, bd)) * (1.0 / HEAD_DIM)
        return x * lax.rsqrt(ms + EPS)

    lo_half = lax.broadcasted_iota(I32, (tm, LANES), 1) < HEAD_DIM
    qn = seg_rms(q_ref[...], bd_ref[...]) * qg_ref[...]
    qi = qi_ref[...]
    for j in range(N_HEADS // 2):
        sl = slice(j * LANES, (j + 1) * LANES)
        q2 = _rope128(qn[:, sl], c, s) * (HEAD_DIM ** -0.5 * LOG2E)
        q2r = pltpu.roll(q2, HEAD_DIM, 1)
        if (2 * j) // (N_HEADS // N_KV_HEADS) == 0:
            qa, qb = jnp.where(lo_half, q2, 0.0), jnp.where(lo_half, q2r, 0.0)
        else:
            qa, qb = jnp.where(lo_half, 0.0, q2r), jnp.where(lo_half, 0.0, q2)
        qt_ref[0, 2 * j] = qa.T.astype(MXU_DT)
        qt_ref[0, 2 * j + 1] = qb.T.astype(MXU_DT)
        qi2 = _rope128(qi[:, sl], c, s)
        qit_ref[0, 2 * j] = jnp.where(lo_half, qi2, 0.0).T.astype(MXU_DT)
        qit_ref[0, 2 * j + 1] = jnp.where(lo_half, pltpu.roll(qi2, HEAD_DIM, 1), 0.0).T.astype(MXU_DT)
    kn = _rope128(seg_rms(k_ref[...], bd_ref[0:LANES, 0:LANES]) * kg_ref[...], c, s)
    kn_ref[0] = kn.T
    knb_ref[...] = kn.astype(MXU_DT)
    v_t = v_ref[...].T
    vo_ref[0] = v_t
    vt_ref[0, 0] = v_t.astype(MXU_DT)
    smr = _rope128(sm_ref[...], c_s, s_s)
    smb_ref[...] = smr.astype(MXU_DT)
    sm_t = smr.T
    kir_ref[0] = sm_t[0:D_IDX, :]
    smt_ref[0] = sm_t


def _dsa_prep_t(proj, qg, kg, consts, nseq, seq, tm):
    t = proj.shape[0]
    dq = N_HEADS * HEAD_DIM
    dk = N_KV_HEADS * HEAD_DIM
    nt = seq // tm
    kern = functools.partial(_dsa_prep_t_kernel, tm=tm)
    row = lambda w: pl.BlockSpec((1, w), lambda s, i: (0, 0))
    tok = lambda w, col: pl.BlockSpec((tm, w), lambda s, i: (s * nt + i, col))
    return pl.pallas_call(
        kern,
        grid=(nseq, nt),
        in_specs=[tok(dq, OFF["q"] // dq), tok(dq, OFF["qi"] // dq), tok(dk, OFF["k"] // dk),
                  tok(dk, OFF["v"] // dk), tok(LANES, OFF["small"] // LANES),
                  row(dq), row(dk), row(LANES), row(LANES), row(LANES), row(LANES),
                  pl.BlockSpec((dq, dq), lambda s, i: (0, 0))],
        out_specs=[pl.BlockSpec((1, dk, tm), lambda s, i: (s, 0, i)),
                   pl.BlockSpec((1, dk, tm), lambda s, i: (s, 0, i)),
                   pl.BlockSpec((1, D_IDX, tm), lambda s, i: (s, 0, i)),
                   tok(dk, 0), tok(LANES, 0),
                   pl.BlockSpec((1, N_HEADS, LANES, tm), lambda s, i: (s, 0, 0, i)),
                   pl.BlockSpec((1, N_IDX_HEADS, LANES, tm), lambda s, i: (s, 0, 0, i)),
                   pl.BlockSpec((1, 1, dk, tm), lambda s, i: (s, i, 0, 0)),
                   pl.BlockSpec((1, LANES, tm), lambda s, i: (s, 0, i))],
        out_shape=[jax.ShapeDtypeStruct((nseq, dk, seq), F32), jax.ShapeDtypeStruct((nseq, dk, seq), F32),
                   jax.ShapeDtypeStruct((nseq, D_IDX, seq), F32),
                   jax.ShapeDtypeStruct((t, dk), MXU_DT), jax.ShapeDtypeStruct((t, LANES), MXU_DT),
                   jax.ShapeDtypeStruct((nseq, N_HEADS, LANES, seq), MXU_DT),
                   jax.ShapeDtypeStruct((nseq, N_IDX_HEADS, LANES, seq), MXU_DT),
                   jax.ShapeDtypeStruct((nseq, nt, dk, tm), MXU_DT),
                   jax.ShapeDtypeStruct((nseq, LANES, seq), F32)],
        compiler_params=_cp(("parallel", "parallel")),
        name="dsa_prep_prompt",
    )(proj, proj, proj, proj, proj, qg, kg, consts["invf"], consts["sgn"], consts["invf_s"],
      consts["sgn_s"], consts["bd"])


def _dsa_prompt_t_kernel(qt_ref, qit_ref, smt_ref, smk_ref, k_ref, vt_ref, tri_ref, o_ref,
                         keys_ref, k16_ref, acc_ref, thr_ref, ngt_ref, neq_ref, *, tq, ck, ksel, nbits):
    i = pl.program_id(1)
    nc = (i * tq + tq + ck - 1) // ck
    qidx = i * tq + lax.broadcasted_iota(I32, (ck, tq), 1)
    krow = lax.broadcasted_iota(I32, (ck, tq), 0)

    def p1(c, carry):
        kc = smk_ref[pl.ds(pl.multiple_of(c * ck, ck), ck), :]
        acc = jnp.zeros((ck, tq), F32)
        for h in range(N_IDX_HEADS):
            s = _dot(kc, qit_ref[0, h])
            acc = acc + jnp.maximum(s, 0.0) * smt_ref[0, SM_WI + h:SM_WI + h + 1, :]
        causal = c * ck + krow <= qidx
        sc = acc * IDX_SCALE
        bits = jnp.where(sc == 0.0, 0, lax.bitcast_convert_type(sc, I32))
        keys_ref[c] = jnp.where(causal, jnp.where(bits < 0, bits ^ 0x7FFFFFFF, bits), INT_MIN)
        half = lax.bitcast_convert_type(jnp.where(causal, bits & -65536, -1), F32)
        k16_ref[c] = half.astype(BF16)
        return carry

    lax.fori_loop(0, nc, p1, 0)

    def count(pred):
        def body(c, part):
            hit = pred(keys_ref[c], c * ck + krow)
            return part + jnp.sum(hit.reshape(ck // 8, 8, tq), axis=0)

        part = lax.fori_loop(0, nc, body, jnp.zeros((8, tq), I32))
        return jnp.sum(part, axis=0, keepdims=True)

    def bisect(thr0, nsteps):
        def bit_step(t, thr):
            cand = thr + lax.shift_left(jnp.int32(1), nsteps - 1 - t)
            cnt = count(lambda kv, kidx: jnp.where(kv >= cand, 1, 0))
            return jnp.where(cnt >= ksel, cand, thr)

        return lax.fori_loop(0, nsteps, bit_step, thr0)

    one16 = jnp.ones((ck, tq), BF16)
    zero16 = jnp.zeros((ck, tq), BF16)

    def bit16_step(t, thr16):
        cand = thr16 + lax.shift_left(jnp.int32(1), 15 - t)
        raw = jnp.where(cand < 0, cand ^ 0x7FFF, cand)
        raw = jnp.where((raw > 0) & (raw < 0x80), 0x80, raw)
        cand_f = lax.bitcast_convert_type(lax.shift_left(raw, 16), F32).astype(BF16)

        def body(c, part):
            hit = jnp.where(k16_ref[c] >= cand_f, one16, zero16).reshape(ck // 16, 16, tq)
            tot = hit[0]
            for r in range(1, ck // 16):
                tot = tot + hit[r]
            return part + tot.astype(F32)

        part = lax.fori_loop(0, nc, body, jnp.zeros((16, tq), F32))
        cnt = jnp.sum(part, axis=0, keepdims=True)
        return jnp.where(cnt >= ksel, cand, thr16)

    thr16 = lax.fori_loop(0, 16, bit16_step, jnp.full((1, tq), -(2 ** 15), I32))
    thr = bisect(lax.shift_left(thr16, 16), 16)

    def tallies(thr):
        return (count(lambda kv, kidx: jnp.where(kv > thr, 1, 0)),
                count(lambda kv, kidx: jnp.where(kv == thr, 1, 0)))

    n_gt, n_eq = tallies(thr)
    thr_ref[...] = thr
    ngt_ref[...] = n_gt
    neq_ref[...] = n_eq
    missed = jnp.max(jnp.where((n_gt >= ksel) | (n_gt + n_eq < ksel), 1, 0))

    @pl.when(missed > 0)
    def _():
        thr_full = bisect(jnp.full((1, tq), INT_MIN, I32), 32)
        n_gt_full, n_eq_full = tallies(thr_full)
        thr_ref[...] = thr_full
        ngt_ref[...] = n_gt_full
        neq_ref[...] = n_eq_full

    thr = thr_ref[...]
    need = ksel - ngt_ref[...]
    excess = jnp.max(jnp.where((neq_ref[...] > need) & (thr > INT_MIN), 1, 0))

    @pl.when(excess > 0)
    def _():
        keep = jnp.where(thr > INT_MIN, need, 2 ** 30).astype(F32)

        def demote(c, seen):
            kv = keys_ref[c]
            tied = jnp.where(kv == thr, 1.0, 0.0)
            prefix = _dot(tri_ref[...], tied.astype(BF16))
            surplus = jnp.where(seen + prefix > keep, tied, 0.0)
            keys_ref[c] = jnp.where(surplus > 0.0, thr - 1, kv)
            return seen + prefix[ck - 1:ck, :]

        lax.fori_loop(0, nc, demote, jnp.zeros((1, tq), F32))

    thr_sel = jnp.maximum(thr, INT_MIN + 1)

    acc_ref[...] = jnp.zeros(acc_ref.shape, F32)

    def p3(c, carry):
        ms, ls = carry
        bias = jnp.where(keys_ref[c] >= thr_sel, 0.0, NEG_BIG)
        kk = k_ref[pl.ds(pl.multiple_of(c * ck, ck), ck), :]
        vt = vt_ref[0, c]
        m_out, l_out, ps, alphas = [], [], [], []
        for h in range(N_HEADS):
            s = _dot(kk, qt_ref[0, h]) + bias
            m_new = jnp.maximum(ms[h], jnp.max(s, axis=0, keepdims=True))
            alpha = jnp.exp2(ms[h] - m_new)
            p = jnp.exp2(s - m_new)
            m_out.append(m_new)
            l_out.append(alpha * ls[h] + jnp.sum(p, axis=0, keepdims=True))
            alphas.append(alpha)
            ps.append(p.astype(MXU_DT))
        for h in range(N_HEADS):
            acc_ref[h] = alphas[h] * acc_ref[h] + _dot(vt, ps[h])
        return tuple(m_out), tuple(l_out)

    init = (tuple(jnp.full((1, tq), NEG_BIG, F32) for _ in range(N_HEADS)),
            tuple(jnp.zeros((1, tq), F32) for _ in range(N_HEADS)))
    _, lrow = lax.fori_loop(0, nc, p3, init)

    lo_half = lax.broadcasted_iota(I32, (tq, LANES), 1) < HEAD_DIM
    for j in range(N_HEADS // 2):
        ea = (acc_ref[2 * j] / lrow[2 * j]).T
        eb = (acc_ref[2 * j + 1] / lrow[2 * j + 1]).T
        if (2 * j) // (N_HEADS // N_KV_HEADS) == 0:
            out2 = jnp.where(lo_half, ea, pltpu.roll(eb, HEAD_DIM, 1))
        else:
            out2 = jnp.where(lo_half, pltpu.roll(ea, HEAD_DIM, 1), eb)
        o_ref[:, j * LANES:(j + 1) * LANES] = out2


def _dsa_prompt_t(qt, qit, smt, smr, kn, vt, nseq, seq, tq, ck):
    dq = N_HEADS * HEAD_DIM
    dk = N_KV_HEADS * HEAD_DIM
    nq = seq // tq
    nck = seq // ck
    ksel = min(TOP_K, seq // 4)
    assert ck >= ksel and ck % LANES == 0 and seq % ck == 0 and seq % tq == 0 and vt.shape[3] == ck
    nbits = max(1, int(math.ceil(math.log2(seq))))
    kern = functools.partial(_dsa_prompt_t_kernel, tq=tq, ck=ck, ksel=ksel, nbits=nbits)
    return pl.pallas_call(
        kern,
        grid=(nseq, nq),
        in_specs=[pl.BlockSpec((1, N_HEADS, LANES, tq), lambda s, i: (s, 0, 0, i)),
                  pl.BlockSpec((1, N_IDX_HEADS, LANES, tq), lambda s, i: (s, 0, 0, i)),
                  pl.BlockSpec((1, LANES, tq), lambda s, i: (s, 0, i)),
                  pl.BlockSpec((seq, LANES), lambda s, i: (s, 0)),
                  pl.BlockSpec((seq, dk), lambda s, i: (s, 0)),
                  pl.BlockSpec((1, nck, dk, ck), lambda s, i: (s, 0, 0, 0)),
                  pl.BlockSpec((ck, ck), lambda s, i: (0, 0))],
        out_specs=pl.BlockSpec((tq, dq), lambda s, i: (s * nq + i, 0)),
        out_shape=jax.ShapeDtypeStruct((nseq * seq, dq), F32),
        scratch_shapes=[pltpu.VMEM((nck, ck, tq), I32),
                        pltpu.VMEM((nck, ck, tq), BF16),
                        pltpu.VMEM((N_HEADS, dk, tq), F32),
                        pltpu.VMEM((1, tq), I32), pltpu.VMEM((1, tq), I32), pltpu.VMEM((1, tq), I32)],
        compiler_params=_cp(("arbitrary", "arbitrary")),
        name="dsa_prompt",
    )(qt, qit, smt, smr, kn, vt, jnp.asarray(np.tril(np.ones((ck, ck), np.float32)), dtype=BF16))


def _dsa_prompt_kernel(q_ref, qi_ref, smq_ref, smk_ref, k_ref, v_ref, o_ref,
                       keys_ref, qs_ref, qis_ref, m_ref, l_ref, acc_ref, thr_ref, y_ref,
                       *, tq, ck, ksel, nbits):
    i = pl.program_id(1)
    nc = (i * tq + tq + ck - 1) // ck
    ng = ck // LANES
    lane = lax.broadcasted_iota(I32, (tq, LANES), 1)
    lo = lane < HEAD_DIM
    rowg = i * tq + lax.broadcasted_iota(I32, (tq, LANES), 0)
    scale = HEAD_DIM ** -0.5

    for j in range(N_HEADS // 2):
        sl = slice(j * LANES, (j + 1) * LANES)
        qi2 = qi_ref[:, sl]
        qis_ref[2 * j] = jnp.where(lo, qi2, 0.0)
        qis_ref[2 * j + 1] = jnp.where(lo, pltpu.roll(qi2, HEAD_DIM, 1), 0.0)
        q2 = q_ref[:, sl] * scale
        q2r = pltpu.roll(q2, HEAD_DIM, 1)
        if (2 * j) // (N_HEADS // N_KV_HEADS) == 0:
            qs_ref[2 * j] = jnp.where(lo, q2, 0.0)
            qs_ref[2 * j + 1] = jnp.where(lo, q2r, 0.0)
        else:
            qs_ref[2 * j] = jnp.where(lo, 0.0, q2r)
            qs_ref[2 * j + 1] = jnp.where(lo, 0.0, q2)

    def p1(c, carry):
        kc = smk_ref[pl.ds(pl.multiple_of(c * ck, ck), ck), :]
        acc = jnp.zeros((tq, ck), F32)
        for h in range(N_IDX_HEADS):
            s = _nt_dot(qis_ref[h], kc)
            acc = acc + jnp.maximum(s, 0.0) * smq_ref[:, SM_WI + h:SM_WI + h + 1]
        sc = acc * IDX_SCALE
        for g in range(ng):
            colg = c * ck + g * LANES + lane
            scg = jnp.where(colg <= rowg, sc[:, g * LANES:(g + 1) * LANES], -jnp.inf)
            keys_ref[c, :, g * LANES:(g + 1) * LANES] = _sortable(scg)
        return carry

    lax.fori_loop(0, nc, p1, 0)

    def count(pred):
        def body(c, part):
            for g in range(ng):
                colg = c * ck + g * LANES + lane
                part = part + pred(keys_ref[c, :, g * LANES:(g + 1) * LANES], colg)
            return part

        part = lax.fori_loop(0, nc, body, jnp.zeros((tq, LANES), I32))
        return jnp.sum(part, axis=1, keepdims=True)

    def bit_step(t, thr):
        cand = thr + lax.shift_left(jnp.int32(1), 31 - t)
        cnt = count(lambda kv, colg: jnp.where(kv >= cand, 1, 0))
        return jnp.where(cnt >= ksel, cand, thr)

    thr = lax.fori_loop(0, 32, bit_step, jnp.full((tq, LANES), INT_MIN, I32))
    thr_ref[...] = thr

    n_gt = count(lambda kv, colg: jnp.where(kv > thr, 1, 0))
    n_eq = count(lambda kv, colg: jnp.where(kv == thr, 1, 0))
    need = ksel - n_gt
    y_ref[...] = jnp.full((tq, LANES), 2 ** 30, I32)
    excess = jnp.max(jnp.where(n_eq > need, 1, 0))

    @pl.when(excess > 0)
    def _():
        def y_step(t, y):
            cand = y + lax.shift_left(jnp.int32(1), nbits - 1 - t)
            g = count(lambda kv, colg: jnp.where(kv == thr, jnp.where(colg < cand, 1, 0), 0))
            return jnp.where(g < need, cand, y)

        y_ref[...] = lax.fori_loop(0, nbits, y_step, jnp.zeros((tq, LANES), I32))

    m_ref[...] = jnp.full(m_ref.shape, NEG_BIG, F32)
    l_ref[...] = jnp.zeros(l_ref.shape, F32)
    acc_ref[...] = jnp.zeros(acc_ref.shape, F32)

    def p3(c, carry):
        thr_v = thr_ref[...]
        y_v = y_ref[...]
        biases = []
        for g in range(ng):
            colg = c * ck + g * LANES + lane
            kv = keys_ref[c, :, g * LANES:(g + 1) * LANES]
            sel = jnp.where(kv > thr_v, 1, jnp.where(kv == thr_v, jnp.where(colg <= y_v, 1, 0), 0))
            sel = jnp.where(colg <= rowg, sel, 0)
            biases.append(jnp.where(sel > 0, 0.0, NEG_BIG))
        bias = jnp.concatenate(biases, axis=1)
        off = pl.multiple_of(c * ck, ck)
        kk = k_ref[pl.ds(off, ck), :]
        vv = v_ref[pl.ds(off, ck), :]
        for h in range(N_HEADS):
            s = _nt_dot(qs_ref[h], kk) + bias
            m_old = m_ref[h]
            m_new = jnp.maximum(m_old, jnp.max(s, axis=1, keepdims=True))
            alpha = jnp.exp(m_old - m_new)
            p = jnp.exp(s - m_new)
            l_ref[h] = alpha * l_ref[h] + jnp.sum(p, axis=1, keepdims=True)
            acc_ref[h] = alpha * acc_ref[h] + _dot(p, vv)
            m_ref[h] = m_new
        return carry

    lax.fori_loop(0, nc, p3, 0)

    for j in range(N_HEADS // 2):
        ea = acc_ref[2 * j] / l_ref[2 * j]
        eb = acc_ref[2 * j + 1] / l_ref[2 * j + 1]
        if (2 * j) // (N_HEADS // N_KV_HEADS) == 0:
            out2 = jnp.where(lo, ea, pltpu.roll(eb, HEAD_DIM, 1))
        else:
            out2 = jnp.where(lo, pltpu.roll(ea, HEAD_DIM, 1), eb)
        o_ref[:, j * LANES:(j + 1) * LANES] = out2


def _dsa_prompt(qn, qir, smr, kn, v, nseq, seq, tq, ck):
    dq = N_HEADS * HEAD_DIM
    dk = N_KV_HEADS * HEAD_DIM
    nq = seq // tq
    ksel = min(TOP_K, seq // 4)
    assert ck >= ksel and ck % LANES == 0 and seq % ck == 0 and seq % tq == 0
    nbits = max(1, int(math.ceil(math.log2(seq))))
    kern = functools.partial(_dsa_prompt_kernel, tq=tq, ck=ck, ksel=ksel, nbits=nbits)
    return pl.pallas_call(
        kern,
        grid=(nseq, nq),
        in_specs=[pl.BlockSpec((tq, dq), lambda s, i: (s * nq + i, 0)),
                  pl.BlockSpec((tq, dq), lambda s, i: (s * nq + i, 0)),
                  pl.BlockSpec((tq, LANES), lambda s, i: (s * nq + i, 0)),
                  pl.BlockSpec((seq, LANES), lambda s, i: (s, 0)),
                  pl.BlockSpec((seq, dk), lambda s, i: (s, 0)),
                  pl.BlockSpec((seq, dk), lambda s, i: (s, 0))],
        out_specs=pl.BlockSpec((tq, dq), lambda s, i: (s * nq + i, 0)),
        out_shape=jax.ShapeDtypeStruct((nseq * seq, dq), F32),
        scratch_shapes=[pltpu.VMEM((seq // ck, tq, ck), I32),
                        pltpu.VMEM((N_HEADS, tq, LANES), F32),
                        pltpu.VMEM((N_IDX_HEADS, tq, LANES), F32),
                        pltpu.VMEM((N_HEADS, tq, 1), F32),
                        pltpu.VMEM((N_HEADS, tq, 1), F32),
                        pltpu.VMEM((N_HEADS, tq, LANES), F32),
                        pltpu.VMEM((tq, LANES), I32),
                        pltpu.VMEM((tq, LANES), I32)],
        compiler_params=_cp(("arbitrary", "arbitrary")),
        name="dsa_prompt",
    )(qn, qir, smr, smr, kn, v)


def _dsa_sample_kernel(pt_ref, q8_ref, qi8_ref, wi_ref, kin_ref, kn_ref, vn_ref,
                       cki_hbm, ck_hbm, cv_hbm, o_ref, kib, kb, vb, sem,
                       *, n_pages, page, ksel, nbits):
    b = pl.program_id(0)
    past = n_pages * page

    def copies(p):
        pg = pt_ref[b, p]
        rows = pl.ds(pl.multiple_of(p * page, page), page)
        return (pltpu.make_async_copy(cki_hbm.at[pg], kib.at[rows, :], sem.at[0]),
                pltpu.make_async_copy(ck_hbm.at[pg], kb.at[rows, :], sem.at[1]),
                pltpu.make_async_copy(cv_hbm.at[pg], vb.at[rows, :], sem.at[2]))

    def start(p, carry):
        for cp in copies(p):
            cp.start()
        return carry

    def wait(p, carry):
        for cp in copies(p):
            cp.wait()
        return carry

    lax.fori_loop(0, n_pages, start, 0)
    lax.fori_loop(0, n_pages, wait, 0)

    wcol = wi_ref[0]
    qi8 = qi8_ref[0]
    s = _nt_dot(qi8, kib[...])
    sc = jnp.sum(jnp.maximum(s, 0.0) * wcol, axis=0, keepdims=True) * IDX_SCALE
    s_new = jnp.sum(qi8 * kin_ref[0], axis=1, keepdims=True)
    sc_new = jnp.sum(jnp.maximum(s_new, 0.0) * wcol, axis=0, keepdims=True) * IDX_SCALE
    keys = _sortable(sc)
    key_new = _sortable(sc_new)
    col = lax.broadcasted_iota(I32, (1, past), 1)

    def cnt(main, new):
        return jnp.sum(main, axis=1, keepdims=True) + new

    def bit_step(t, thr):
        cand = thr + lax.shift_left(jnp.int32(1), 31 - t)
        c = cnt(jnp.where(keys >= cand, 1, 0), jnp.where(key_new >= cand, 1, 0))
        return jnp.where(c >= ksel, cand, thr)

    thr = lax.fori_loop(0, 32, bit_step, jnp.full((1, 1), INT_MIN, I32))
    need = ksel - cnt(jnp.where(keys > thr, 1, 0), jnp.where(key_new > thr, 1, 0))

    def y_step(t, y):
        cand = y + lax.shift_left(jnp.int32(1), nbits - 1 - t)
        g = cnt(jnp.where(keys == thr, jnp.where(col < cand, 1, 0), 0),
                jnp.where(key_new == thr, jnp.where(past < cand, 1, 0), 0))
        return jnp.where(g < need, cand, y)

    y = lax.fori_loop(0, nbits, y_step, jnp.zeros((1, 1), I32))
    sel = jnp.where(keys > thr, 1, jnp.where(keys == thr, jnp.where(col <= y, 1, 0), 0))
    sel_new = jnp.where(key_new > thr, 1, jnp.where(key_new == thr, jnp.where(past <= y, 1, 0), 0))
    bias = jnp.where(sel > 0, 0.0, NEG_BIG)
    bias_new = jnp.where(sel_new > 0, 0.0, NEG_BIG)

    q8 = q8_ref[0] * (HEAD_DIM ** -0.5)
    sa = _nt_dot(q8, kb[...]) + bias
    sa_new = jnp.sum(q8 * kn_ref[0], axis=1, keepdims=True) + bias_new
    m = jnp.maximum(jnp.max(sa, axis=1, keepdims=True), sa_new)
    p = jnp.exp(sa - m)
    p_new = jnp.exp(sa_new - m)
    l = jnp.sum(p, axis=1, keepdims=True) + p_new
    o_ref[0] = (_dot(p, vb[...]) + p_new * vn_ref[0]) / l


def _dsa_sample(page_table, q8, qi8, wi, ki_new, k_new, v_new, cki, ck, cv):
    nb, n_pages = page_table.shape
    page = cki.shape[1]
    past = n_pages * page
    dk = N_KV_HEADS * HEAD_DIM
    ksel = min(TOP_K, (past + 1) // 4)
    nbits = int(math.floor(math.log2(past))) + 1
    kern = functools.partial(_dsa_sample_kernel, n_pages=n_pages, page=page, ksel=ksel, nbits=nbits)
    grid_spec = pltpu.PrefetchScalarGridSpec(
        num_scalar_prefetch=1,
        grid=(nb,),
        in_specs=[pl.BlockSpec((1, N_HEADS, dk), lambda b, pt: (b, 0, 0)),
                  pl.BlockSpec((1, N_IDX_HEADS, D_IDX), lambda b, pt: (b, 0, 0)),
                  pl.BlockSpec((1, N_IDX_HEADS, 1), lambda b, pt: (b, 0, 0)),
                  pl.BlockSpec((1, 1, D_IDX), lambda b, pt: (b, 0, 0)),
                  pl.BlockSpec((1, 1, dk), lambda b, pt: (b, 0, 0)),
                  pl.BlockSpec((1, 1, dk), lambda b, pt: (b, 0, 0)),
                  pl.BlockSpec(memory_space=pl.ANY),
                  pl.BlockSpec(memory_space=pl.ANY),
                  pl.BlockSpec(memory_space=pl.ANY)],
        out_specs=pl.BlockSpec((1, N_HEADS, dk), lambda b, pt: (b, 0, 0)),
        scratch_shapes=[pltpu.VMEM((past, D_IDX), F32),
                        pltpu.VMEM((past, dk), F32),
                        pltpu.VMEM((past, dk), F32),
                        pltpu.SemaphoreType.DMA((3,))],
    )
    return pl.pallas_call(
        kern,
        grid_spec=grid_spec,
        out_shape=jax.ShapeDtypeStruct((nb, N_HEADS, dk), F32),
        compiler_params=_cp(("arbitrary",)),
        name="dsa_sample",
    )(page_table, q8, qi8, wi, ki_new, k_new, v_new, cki, ck, cv)


def _page_copies(hbm, layer, pt_ref, b, buf, slot, sem, n_pages, page):
    return [pltpu.make_async_copy(hbm.at[layer, pt_ref[b, p]],
                                  buf.at[slot, :, p * page:(p + 1) * page], sem.at[slot])
            for p in range(n_pages)]


def _prefetch_pages(hbms, bufs, sems, layer, pt_ref, n_pages, page):
    b = pl.program_id(0)
    nb = pl.num_programs(0)
    slot = b % 2

    def start(bb, sl):
        for hbm, buf, sem in zip(hbms, bufs, sems):
            for cp in _page_copies(hbm, layer, pt_ref, bb, buf, sl, sem, n_pages, page):
                cp.start()

    @pl.when(b == 0)
    def _():
        start(0, 0)

    @pl.when(b + 1 < nb)
    def _():
        start(b + 1, 1 - slot)

    for hbm, buf, sem in zip(hbms, bufs, sems):
        for cp in _page_copies(hbm, layer, pt_ref, b, buf, slot, sem, n_pages, page):
            cp.wait()
    return slot


def _idx_sample_kernel(pt_ref, qi8_ref, wi_ref, kin_ref, cki_hbm, sc_ref, kibuf, sem,
                       *, layer, n_pages, page):
    past = n_pages * page
    slot = _prefetch_pages((cki_hbm,), (kibuf,), (sem,), layer, pt_ref, n_pages, page)
    wcol = wi_ref[0]
    qi8 = qi8_ref[0]
    s = _dot(qi8, kibuf[slot])
    sc_ref[0, :, 0:past] = jnp.sum(jnp.maximum(s, 0.0) * wcol, axis=0, keepdims=True) * IDX_SCALE
    s_new = jnp.sum(qi8 * kin_ref[0], axis=1, keepdims=True)
    sc_new = jnp.sum(jnp.maximum(s_new, 0.0) * wcol, axis=0, keepdims=True) * IDX_SCALE
    lane = lax.broadcasted_iota(I32, (1, LANES), 1)
    sc_ref[0, :, past:past + LANES] = jnp.where(lane == 0, sc_new, -jnp.inf)


def _idx_sample(page_table, qi8, wi, ki_new, cki_t, layer):
    nb, n_pages = page_table.shape
    page = cki_t.shape[3]
    past = n_pages * page
    kern = functools.partial(_idx_sample_kernel, layer=layer, n_pages=n_pages, page=page)
    grid_spec = pltpu.PrefetchScalarGridSpec(
        num_scalar_prefetch=1,
        grid=(nb,),
        in_specs=[pl.BlockSpec((1, N_IDX_HEADS, D_IDX), lambda b, pt: (b, 0, 0)),
                  pl.BlockSpec((1, N_IDX_HEADS, 1), lambda b, pt: (b, 0, 0)),
                  pl.BlockSpec((1, 1, D_IDX), lambda b, pt: (b, 0, 0)),
                  pl.BlockSpec(memory_space=pl.ANY)],
        out_specs=pl.BlockSpec((1, 1, past + LANES), lambda b, pt: (b, 0, 0)),
        scratch_shapes=[pltpu.VMEM((2, D_IDX, past), F32), pltpu.SemaphoreType.DMA((2,))],
    )
    return pl.pallas_call(
        kern, grid_spec=grid_spec,
        out_shape=jax.ShapeDtypeStruct((nb, 1, past + LANES), F32),
        compiler_params=_cp(("arbitrary",)),
        name="idx_sample",
    )(page_table, qi8, wi, ki_new, cki_t)


def _topk_bias_kernel(sc_ref, bias_ref, *, ksel, nbits):
    keys = _sortable(sc_ref[...])
    col = lax.broadcasted_iota(I32, keys.shape, 1)

    def cnt(hit):
        return jnp.sum(hit, axis=1, keepdims=True)

    def bit_step(t, thr):
        cand = thr + lax.shift_left(jnp.int32(1), 31 - t)
        return jnp.where(cnt(jnp.where(keys >= cand, 1, 0)) >= ksel, cand, thr)

    thr = lax.fori_loop(0, 32, bit_step, jnp.full((keys.shape[0], 1), INT_MIN, I32))
    need = ksel - cnt(jnp.where(keys > thr, 1, 0))

    def y_step(t, y):
        cand = y + lax.shift_left(jnp.int32(1), nbits - 1 - t)
        g = cnt(jnp.where(keys == thr, jnp.where(col < cand, 1, 0), 0))
        return jnp.where(g < need, cand, y)

    y = lax.fori_loop(0, nbits, y_step, jnp.zeros((keys.shape[0], 1), I32))
    sel = jnp.where(keys > thr, 1, jnp.where(keys == thr, jnp.where(col <= y, 1, 0), 0))
    bias_ref[...] = jnp.where(sel > 0, 0.0, NEG_BIG)


def _topk_bias(sc, ksel):
    nb, width = sc.shape
    nbits = int(math.floor(math.log2(width))) + 1
    kern = functools.partial(_topk_bias_kernel, ksel=ksel, nbits=nbits)
    return pl.pallas_call(
        kern, grid=(1,),
        in_specs=[pl.BlockSpec((nb, width), lambda i: (0, 0))],
        out_specs=pl.BlockSpec((nb, width), lambda i: (0, 0)),
        out_shape=jax.ShapeDtypeStruct((nb, width), F32),
        compiler_params=_cp(("arbitrary",)),
        name="topk_bias_sample",
    )(sc)


def _attn_sample_kernel(pt_ref, q8_ref, bias_ref, kn_ref, vn_ref, ck_hbm, cv_hbm, o_ref,
                        kbuf, vbuf, ksem, vsem, *, layer, n_pages, page):
    past = n_pages * page
    slot = _prefetch_pages((ck_hbm, cv_hbm), (kbuf, vbuf), (ksem, vsem), layer, pt_ref, n_pages, page)
    q8 = q8_ref[0] * (HEAD_DIM ** -0.5)
    sa = _dot(q8, kbuf[slot]) + bias_ref[0, :, 0:past]
    sa_new = jnp.sum(q8 * kn_ref[0], axis=1, keepdims=True) + bias_ref[0, :, past:past + 1]
    m = jnp.maximum(jnp.max(sa, axis=1, keepdims=True), sa_new)
    p = jnp.exp(sa - m)
    p_new = jnp.exp(sa_new - m)
    l = jnp.sum(p, axis=1, keepdims=True) + p_new
    o_ref[0] = (_nt_dot(p, vbuf[slot]) + p_new * vn_ref[0]) / l


def _attn_sample(page_table, q8, bias, k_new, v_new, ck_t, cv_t, layer):
    nb, n_pages = page_table.shape
    page = ck_t.shape[3]
    past = n_pages * page
    dk = N_KV_HEADS * HEAD_DIM
    kern = functools.partial(_attn_sample_kernel, layer=layer, n_pages=n_pages, page=page)
    grid_spec = pltpu.PrefetchScalarGridSpec(
        num_scalar_prefetch=1,
        grid=(nb,),
        in_specs=[pl.BlockSpec((1, N_HEADS, dk), lambda b, pt: (b, 0, 0)),
                  pl.BlockSpec((1, 1, past + LANES), lambda b, pt: (b, 0, 0)),
                  pl.BlockSpec((1, 1, dk), lambda b, pt: (b, 0, 0)),
                  pl.BlockSpec((1, 1, dk), lambda b, pt: (b, 0, 0)),
                  pl.BlockSpec(memory_space=pl.ANY),
                  pl.BlockSpec(memory_space=pl.ANY)],
        out_specs=pl.BlockSpec((1, N_HEADS, dk), lambda b, pt: (b, 0, 0)),
        scratch_shapes=[pltpu.VMEM((2, dk, past), F32), pltpu.VMEM((2, dk, past), F32),
                        pltpu.SemaphoreType.DMA((2,)), pltpu.SemaphoreType.DMA((2,))],
    )
    return pl.pallas_call(
        kern, grid_spec=grid_spec,
        out_shape=jax.ShapeDtypeStruct((nb, N_HEADS, dk), F32),
        compiler_params=_cp(("arbitrary",)),
        name="attn_sample",
    )(page_table, q8, bias, k_new, v_new, ck_t, cv_t)


def _ssd_prompt_kernel(xbc_ref, z_ref, sm_ref, cw_ref, cb_ref, dtb_ref, alog_ref, dskip_ref, ng_ref,
                       tril_ref, e64_ref, e128_ref, bmask_ref,
                       o_ref, cnew_ref, sst_ref, xbuf, st_ref, *, ts, kw, dxbc):
    halo = 8
    di = SSM_HEADS * SSM_HEAD_DIM
    dbc = SSM_GROUPS * D_STATE

    @pl.when(pl.program_id(1) == 0)
    def _():
        xbuf[0:halo, :] = jnp.zeros((halo, dxbc), F32)
        st_ref[...] = jnp.zeros(st_ref.shape, F32)

    xbuf[halo:halo + ts, :] = xbc_ref[...]
    first = halo - (kw - 1)
    conv = cb_ref[...] + cw_ref[0:1, :] * xbuf[first:first + ts, :]
    for j in range(1, kw):
        conv = conv + cw_ref[j:j + 1, :] * xbuf[first + j:first + j + ts, :]
    xc = _silu(conv)
    cnew_ref[0] = xbuf[ts + first:ts + halo, :]
    xbuf[0:halo, :] = xbuf[ts:ts + halo, :]

    lane = lax.broadcasted_iota(I32, (SSM_CHUNK, LANES), 1)
    head_lane = (lane[0:1, :] >= SM_DT) & (lane[0:1, :] < SM_DT + SSM_HEADS)
    a_row = jnp.where(head_lane, -jnp.exp(alog_ref[...]), 0.0)
    tri = lax.broadcasted_iota(I32, (SSM_CHUNK, SSM_CHUNK), 0) >= lax.broadcasted_iota(
        I32, (SSM_CHUNK, SSM_CHUNK), 1)
    glo = lane < D_STATE

    for k in range(ts // SSM_CHUNK):
        rows = slice(k * SSM_CHUNK, (k + 1) * SSM_CHUNK)
        dtf = _softplus(sm_ref[rows, :] + dtb_ref[...])
        adt = dtf * a_row
        a_cs = _sel_dot(tril_ref[...], adt)
        a_cs_t = a_cs.T
        acs_b = _dot_sel(a_cs, e128_ref[...])
        acs_f = _dot_sel(a_cs, e64_ref[...])
        dt_f = _dot_sel(dtf, e64_ref[...])
        alast_f = acs_f[SSM_CHUNK - 1:SSM_CHUNK, :]
        xs = xc[rows, 0:di]
        bm = xc[rows, di:di + dbc]
        cm = xc[rows, di + dbc:di + 2 * dbc]
        xdt = xs * dt_f
        xd = xdt * jnp.exp(alast_f - acs_f)
        bt = bm.T
        cb = (_dot(jnp.where(glo, cm, 0.0), bt), _dot(jnp.where(glo, 0.0, cm), bt))
        pairs = []
        for j in range(SSM_HEADS // 2):
            x2 = xdt[:, j * LANES:(j + 1) * LANES]
            acc = None
            for hh in range(2):
                h = 2 * j + hh
                seg = acs_b[:, h * LANES:(h + 1) * LANES] - a_cs_t[SM_DT + h:SM_DT + h + 1, :]
                lm = jnp.exp(jnp.where(tri, seg, -jnp.inf))
                sc = cb[h // (SSM_HEADS // SSM_GROUPS)] * lm
                xm = jnp.where(glo, x2, 0.0) if hh == 0 else jnp.where(glo, 0.0, x2)
                part = _dot(sc, xm)
                acc = part if acc is None else acc + part
            pairs.append(acc)
        y = jnp.concatenate(pairs, axis=1)
        y = y + _dot(cm, st_ref[...]) * jnp.exp(acs_f) + dskip_ref[...] * xs
        st_ref[...] = st_ref[...] * jnp.exp(alast_f) + bmask_ref[...] * _dot(bt, xd)
        yg = y * _silu(z_ref[rows, :])
        o_ref[rows, :] = _rms(yg) * ng_ref[...]
    sst_ref[0] = st_ref[...]


def _ssd_prompt(proj, cw, cb, dtb_row, alog_row, dskip_f, ng, consts, nseq, seq, ts):
    kw, dxbc = cw.shape
    di = SSM_HEADS * SSM_HEAD_DIM
    nt = seq // ts
    kern = functools.partial(_ssd_prompt_kernel, ts=ts, kw=kw, dxbc=dxbc)
    full = lambda a: pl.BlockSpec(a.shape, lambda s, i: (0,) * a.ndim)
    cs = (consts["tril"], consts["e64"], consts["e128"], consts["bmask"])
    return pl.pallas_call(
        kern,
        grid=(nseq, nt),
        in_specs=[pl.BlockSpec((ts, dxbc), lambda s, i: (s * nt + i, OFF["xbc"] // dxbc)),
                  pl.BlockSpec((ts, di), lambda s, i: (s * nt + i, OFF["z"] // di)),
                  pl.BlockSpec((ts, LANES), lambda s, i: (s * nt + i, OFF["small"] // LANES)),
                  full(cw), full(cb), full(dtb_row), full(alog_row), full(dskip_f), full(ng)]
                 + [full(c) for c in cs],
        out_specs=[pl.BlockSpec((ts, di), lambda s, i: (s * nt + i, 0)),
                   pl.BlockSpec((1, kw - 1, dxbc), lambda s, i: (s, 0, 0)),
                   pl.BlockSpec((1, SSM_GROUPS * D_STATE, di), lambda s, i: (s, 0, 0))],
        out_shape=[jax.ShapeDtypeStruct((nseq * seq, di), F32),
                   jax.ShapeDtypeStruct((nseq, kw - 1, dxbc), F32),
                   jax.ShapeDtypeStruct((nseq, SSM_GROUPS * D_STATE, di), F32)],
        scratch_shapes=[pltpu.VMEM((ts + 8, dxbc), F32),
                        pltpu.VMEM((SSM_GROUPS * D_STATE, di), F32)],
        compiler_params=_cp(("arbitrary", "arbitrary")),
        name="ssd_prompt",
    )(proj, proj, proj, cw, cb, dtb_row, alog_row, dskip_f, ng, *cs)


def _ssd_sample_kernel(xbc_ref, z_ref, sm_ref, st_ref, h0_ref, cw_ref, cb_ref, dtb_ref, alog_ref,
                       dskip_ref, ng_ref, e64_ref, o_ref, hn_ref, y_ref, *, nb, kw):
    di = SSM_HEADS * SSM_HEAD_DIM
    dbc = SSM_GROUPS * D_STATE
    conv = cb_ref[...] + cw_ref[kw - 1:kw, :] * xbc_ref[...]
    for j in range(kw - 1):
        conv = conv + cw_ref[j:j + 1, :] * st_ref[j]
    xc = _silu(conv)
    lane = lax.broadcasted_iota(I32, (1, LANES), 1)
    head_lane = (lane >= SM_DT) & (lane < SM_DT + SSM_HEADS)
    a_row = jnp.where(head_lane, -jnp.exp(alog_ref[...]), 0.0)
    dtf = _softplus(sm_ref[...] + dtb_ref[...])
    dec = jnp.exp(dtf * a_row)
    dt_f = _dot_sel(dtf, e64_ref[...])
    dec_f = _dot_sel(dec, e64_ref[...])
    xs = xc[:, 0:di]
    bm = xc[:, di:di + dbc]
    cm = xc[:, di + dbc:di + 2 * dbc]
    pad = jnp.zeros((LANES - nb, di), F32)
    xdt_t = jnp.concatenate([xs * dt_f, pad], axis=0).T
    dec_t = jnp.concatenate([dec_f, pad], axis=0).T
    bm_r = pltpu.roll(bm, D_STATE, 1)
    cm_r = pltpu.roll(cm, D_STATE, 1)
    rowi = lax.broadcasted_iota(I32, (di, D_STATE), 0)
    g0 = rowi < (SSM_HEADS // SSM_GROUPS) * SSM_HEAD_DIM
    lane_y = lax.broadcasted_iota(I32, (1, di), 1) < (SSM_HEADS // SSM_GROUPS) * SSM_HEAD_DIM
    row8 = lax.broadcasted_iota(I32, (8, D_STATE), 0)
    for b in range(nb):
        bsel = jnp.where(g0, bm[b:b + 1, 0:D_STATE], bm_r[b:b + 1, 0:D_STATE])
        hn = h0_ref[b] * dec_t[:, b:b + 1] + xdt_t[:, b:b + 1] * bsel
        hn_ref[b] = hn
        c2 = jnp.where(row8 == 0, cm[b:b + 1, 0:D_STATE],
                       jnp.where(row8 == 1, cm_r[b:b + 1, 0:D_STATE], 0.0))
        yr = _nt_dot(c2, hn)
        y_ref[b:b + 1, :] = jnp.where(lane_y, yr[0:1, :], yr[1:2, :])
    y = y_ref[...] + dskip_ref[...] * xs
    yg = y * _silu(z_ref[...])
    o_ref[...] = _rms(yg) * ng_ref[...]


def _ssd_sample(proj, st_t, h0, cw, cb, dtb_row, alog_row, dskip_f, ng, consts):
    kw, dxbc = cw.shape
    nb = proj.shape[0]
    di = SSM_HEADS * SSM_HEAD_DIM
    kern = functools.partial(_ssd_sample_kernel, nb=nb, kw=kw)
    full = lambda a: pl.BlockSpec(a.shape, lambda i: (0,) * a.ndim)
    return pl.pallas_call(
        kern,
        grid=(1,),
        in_specs=[pl.BlockSpec((nb, dxbc), lambda i: (0, OFF["xbc"] // dxbc)),
                  pl.BlockSpec((nb, di), lambda i: (0, OFF["z"] // di)),
                  pl.BlockSpec((nb, LANES), lambda i: (0, OFF["small"] // LANES)),
                  full(st_t), full(h0), full(cw), full(cb), full(dtb_row), full(alog_row),
                  full(dskip_f), full(ng), full(consts["e64"])],
        out_specs=[pl.BlockSpec((nb, di), lambda i: (0, 0)),
                   pl.BlockSpec(h0.shape, lambda i: (0, 0, 0))],
        out_shape=[jax.ShapeDtypeStruct((nb, di), F32), jax.ShapeDtypeStruct(h0.shape, F32)],
        scratch_shapes=[pltpu.VMEM((nb, di), F32)],
        compiler_params=_cp(("arbitrary",)),
        name="ssd_sample",
    )(proj, proj, proj, st_t, h0, cw, cb, dtb_row, alog_row, dskip_f, ng, consts["e64"])


def _mem_kv_kernel(x_ref, g_ref, w_ref, kg_ref, mk_ref, mv_ref):
    dm = mk_ref.shape[1]
    m = _dot(_rms(x_ref[...]) * g_ref[...], w_ref[...])
    for h in range(MEM_HEADS):
        sl = slice(h * MEM_HEAD_DIM, (h + 1) * MEM_HEAD_DIM)
        mk_ref[:, sl] = _rms(m[:, sl]) * kg_ref[...]
    mv_ref[...] = m[:, dm:]


def _mem_kv(mem2d, g, w, kg, rows):
    t, d = mem2d.shape
    dm = MEM_HEADS * MEM_HEAD_DIM
    return pl.pallas_call(
        _mem_kv_kernel,
        grid=(t // rows,),
        in_specs=[pl.BlockSpec((rows, d), lambda i: (i, 0)),
                  pl.BlockSpec((1, d), lambda i: (0, 0)),
                  pl.BlockSpec((d, 2 * dm), lambda i: (0, 0)),
                  pl.BlockSpec((1, MEM_HEAD_DIM), lambda i: (0, 0))],
        out_specs=[pl.BlockSpec((rows, dm), lambda i: (i, 0)),
                   pl.BlockSpec((rows, dm), lambda i: (i, 0))],
        out_shape=[jax.ShapeDtypeStruct((t, dm), F32), jax.ShapeDtypeStruct((t, dm), F32)],
        compiler_params=_cp(("parallel",)),
        name="mem_kv",
    )(mem2d, g, w, kg)


def _mem_attend_kernel(mq_ref, mk_ref, mv_ref, g_ref, o_ref, *, tm):
    mq = mq_ref[0]
    rows = max(tm, 8)
    if tm < rows:
        mq = jnp.broadcast_to(mq, (rows, mq.shape[1]))
    for h in range(MEM_HEADS):
        sl = slice(h * MEM_HEAD_DIM, (h + 1) * MEM_HEAD_DIM)
        qn = _rms(mq[:, sl]) * g_ref[...]
        s = _nt_dot(qn, mk_ref[0, :, sl]) * (MEM_HEAD_DIM ** -0.5)
        p = jnp.exp(s - jnp.max(s, axis=1, keepdims=True))
        o = _dot(p, mv_ref[0, :, sl]) / jnp.sum(p, axis=1, keepdims=True)
        o_ref[0, :, sl] = o[0:tm, :]


def _mem_attend(proj3, mk3, mv3, g, tm):
    nseq, seq, _ = proj3.shape
    nm = mk3.shape[1]
    dm = MEM_HEADS * MEM_HEAD_DIM
    kern = functools.partial(_mem_attend_kernel, tm=tm)
    return pl.pallas_call(
        kern,
        grid=(nseq, seq // tm),
        in_specs=[pl.BlockSpec((1, tm, dm), lambda s, i: (s, i, OFF["mq"] // dm)),
                  pl.BlockSpec((1, nm, dm), lambda s, i: (s, 0, 0)),
                  pl.BlockSpec((1, nm, dm), lambda s, i: (s, 0, 0)),
                  pl.BlockSpec((1, MEM_HEAD_DIM), lambda s, i: (0, 0))],
        out_specs=pl.BlockSpec((1, tm, dm), lambda s, i: (s, i, 0)),
        out_shape=jax.ShapeDtypeStruct((nseq, seq, dm), F32),
        compiler_params=_cp(("parallel", "arbitrary")),
        name="mem_attend",
    )(proj3, mk3, mv3, g)


def _merge_kernel(x_ref, ba_ref, bb_ref, bc_ref, bm_ref, gt_ref, wb_ref, wo_ref, o_ref):
    d = x_ref.shape[1]
    acc = None
    for n, br in enumerate((ba_ref, bb_ref, bc_ref, bm_ref)):
        term = jax.nn.sigmoid(gt_ref[:, n * d:(n + 1) * d]) * _dot(br[...].astype(MXU_DT), wb_ref[n])
        acc = term if acc is None else acc + term
    o_ref[...] = x_ref[...] + _dot(acc.astype(MXU_DT), wo_ref[...])


def _merge(x, proj, brs, wb, wo, tm):
    t, d = x.shape
    nbr, bw, _ = wb.shape
    return pl.pallas_call(
        _merge_kernel,
        grid=(t // tm,),
        in_specs=[pl.BlockSpec((tm, d), lambda i: (i, 0))]
                 + [pl.BlockSpec((tm, bw), lambda i: (i, 0))] * nbr
                 + [pl.BlockSpec((tm, nbr * d), lambda i: (i, OFF["gates"] // (nbr * d))),
                    pl.BlockSpec((nbr, bw, d), lambda i: (0, 0, 0)),
                    pl.BlockSpec((d, d), lambda i: (0, 0))],
        out_specs=pl.BlockSpec((tm, d), lambda i: (i, 0)),
        out_shape=jax.ShapeDtypeStruct((t, d), F32),
        compiler_params=_cp(("parallel",)),
        name="merge",
    )(x, *brs, proj, wb, wo)


def _ffn_prompt_kernel(x_ref, g_ref, wg_ref, wu_ref, cwg_ref, cwu_ref, cbg_ref, cbu_ref, wd_ref,
                       o_ref, unew_ref, h_ref, acc_ref, ubuf, carry, *, tm, tc, kw, nff):
    i = pl.program_id(1)
    c = pl.program_id(2)
    halo = 8
    first = halo - (kw - 1)

    @pl.when(c == 0)
    def _():
        h_ref[...] = (_rms(x_ref[...]) * g_ref[...]).astype(h_ref.dtype)
        acc_ref[...] = jnp.zeros(acc_ref.shape, F32)

    @pl.when(i == 0)
    def _():
        carry[c] = jnp.zeros((halo, 2 * tc), F32)

    h = h_ref[...]
    ubuf[0:halo, :] = carry[c]
    ubuf[halo:halo + tm, 0:tc] = _dot(h, wg_ref[...])
    ubuf[halo:halo + tm, tc:2 * tc] = _dot(h, wu_ref[...])
    carry[c] = ubuf[tm:tm + halo, :]
    fg =cbg_ref[...] + cwg_ref[0:1, :] * ubuf[first:first + tm, 0:tc]
    fu = cbu_ref[...] + cwu_ref[0:1, :] * ubuf[first:first + tm, tc:2 * tc]
    for j in range(1, kw):
        fg = fg + cwg_ref[j:j + 1, :] * ubuf[first + j:first + j + tm, 0:tc]
        fu = fu + cwu_ref[j:j + 1, :] * ubuf[first + j:first + j + tm, tc:2 * tc]
    acc_ref[...] += _dot((_silu(fg) * fu).astype(MXU_DT), wd_ref[...])

    @pl.when(c == nff - 1)
    def _():
        o_ref[...] = x_ref[...] + acc_ref[...]

    @pl.when((c == nff - 1) & (i == pl.num_programs(1) - 1))
    def _():
        for cc in range(nff):
            unew_ref[0, :, cc * tc:(cc + 1) * tc] = carry[cc, first:halo, 0:tc]
            unew_ref[0, :, (nff + cc) * tc:(nff + cc + 1) * tc] = carry[cc, first:halo, tc:2 * tc]


def _ffn_prompt(x, g, wup, cw, cb, wd, nseq, seq, tm, tc):
    t, d = x.shape
    dff = wd.shape[0]
    kw = cw.shape[0]
    nt = seq // tm
    nff = dff // tc
    kern = functools.partial(_ffn_prompt_kernel, tm=tm, tc=tc, kw=kw, nff=nff)
    return pl.pallas_call(
        kern,
        grid=(nseq, nt, nff),
        in_specs=[pl.BlockSpec((tm, d), lambda s, i, c: (s * nt + i, 0)),
                  pl.BlockSpec((1, d), lambda s, i, c: (0, 0)),
                  pl.BlockSpec((d, tc), lambda s, i, c: (0, c)),
                  pl.BlockSpec((d, tc), lambda s, i, c: (0, nff + c)),
                  pl.BlockSpec((kw, tc), lambda s, i, c: (0, c)),
                  pl.BlockSpec((kw, tc), lambda s, i, c: (0, nff + c)),
                  pl.BlockSpec((1, tc), lambda s, i, c: (0, c)),
                  pl.BlockSpec((1, tc), lambda s, i, c: (0, nff + c)),
                  pl.BlockSpec((tc, d), lambda s, i, c: (c, 0))],
        out_specs=[pl.BlockSpec((tm, d), lambda s, i, c: (s * nt + i, 0)),
                   pl.BlockSpec((1, kw - 1, 2 * dff), lambda s, i, c: (s, 0, 0))],
        out_shape=[jax.ShapeDtypeStruct((t, d), F32),
                   jax.ShapeDtypeStruct((nseq, kw - 1, 2 * dff), F32)],
        scratch_shapes=[pltpu.VMEM((tm, d), MXU_DT), pltpu.VMEM((tm, d), F32),
                        pltpu.VMEM((tm + 8, 2 * tc), F32), pltpu.VMEM((nff, 8, 2 * tc), F32)],
        compiler_params=_cp(("arbitrary", "arbitrary", "arbitrary")),
        name="ffn_prompt",
    )(x, g, wup, wup, cw, cw, cb, cb, wd)


def _ffn_sample_kernel(x_ref, g_ref, wg_ref, wu_ref, stg_ref, stu_ref, cwg_ref, cwu_ref, cbg_ref,
                       cbu_ref, wd_ref, o_ref, ug_ref, uu_ref, h_ref, acc_ref, *, kw):
    c = pl.program_id(0)

    @pl.when(c == 0)
    def _():
        h_ref[...] = (_rms(x_ref[...]) * g_ref[...]).astype(h_ref.dtype)
        acc_ref[...] = jnp.zeros(acc_ref.shape, F32)

    h = h_ref[...]
    ug = _dot(h, wg_ref[...])
    uu = _dot(h, wu_ref[...])
    ug_ref[...] = ug
    uu_ref[...] = uu
    fg = cbg_ref[...] + cwg_ref[kw - 1:kw, :] * ug
    fu = cbu_ref[...] + cwu_ref[kw - 1:kw, :] * uu
    for j in range(kw - 1):
        fg = fg + cwg_ref[j:j + 1, :] * stg_ref[j]
        fu = fu + cwu_ref[j:j + 1, :] * stu_ref[j]
    acc_ref[...] += _dot((_silu(fg) * fu).astype(MXU_DT), wd_ref[...])

    @pl.when(c == pl.num_programs(0) - 1)
    def _():
        o_ref[...] = x_ref[...] + acc_ref[...]


def _ffn_sample(x, g, wup, st_t, cw, cb, wd, tc):
    t, d = x.shape
    dff = wd.shape[0]
    kw = cw.shape[0]
    nff = dff // tc
    kern = functools.partial(_ffn_sample_kernel, kw=kw)
    return pl.pallas_call(
        kern,
        grid=(nff,),
        in_specs=[pl.BlockSpec((t, d), lambda c: (0, 0)),
                  pl.BlockSpec((1, d), lambda c: (0, 0)),
                  pl.BlockSpec((d, tc), lambda c: (0, c)),
                  pl.BlockSpec((d, tc), lambda c: (0, nff + c)),
                  pl.BlockSpec((kw - 1, t, tc), lambda c: (0, 0, c)),
                  pl.BlockSpec((kw - 1, t, tc), lambda c: (0, 0, nff + c)),
                  pl.BlockSpec((kw, tc), lambda c: (0, c)),
                  pl.BlockSpec((kw, tc), lambda c: (0, nff + c)),
                  pl.BlockSpec((1, tc), lambda c: (0, c)),
                  pl.BlockSpec((1, tc), lambda c: (0, nff + c)),
                  pl.BlockSpec((tc, d), lambda c: (c, 0))],
        out_specs=[pl.BlockSpec((t, d), lambda c: (0, 0)),
                   pl.BlockSpec((t, tc), lambda c: (0, c)),
                   pl.BlockSpec((t, tc), lambda c: (0, c))],
        out_shape=[jax.ShapeDtypeStruct((t, d), F32), jax.ShapeDtypeStruct((t, dff), F32),
                   jax.ShapeDtypeStruct((t, dff), F32)],
        scratch_shapes=[pltpu.VMEM((t, d), MXU_DT), pltpu.VMEM((t, d), F32)],
        compiler_params=_cp(("arbitrary",)),
        name="ffn_sample",
    )(x, g, wup, wup, st_t, st_t, cw, cw, cb, cb, wd)


def _constants():
    lane = np.arange(LANES)
    r = lane % HEAD_DIM
    rot = HEAD_DIM // 4
    half = rot // 2
    inv_freq = ROPE_THETA ** (-jnp.arange(half, dtype=F32) * (2.0 / rot))
    in_rot = r < rot
    invf = jnp.where(jnp.asarray(in_rot), inv_freq[jnp.asarray(r % half)], 0.0).astype(F32)[None, :]
    sgn = np.where(r < half, -1.0, np.where(in_rot, 1.0, 0.0)).astype(np.float32)[None, :]
    first_head = (lane < HEAD_DIM)[None, :]
    dq = N_HEADS * HEAD_DIM
    bd = (np.arange(dq)[:, None] // HEAD_DIM == np.arange(dq)[None, :] // HEAD_DIM)
    di = SSM_HEADS * SSM_HEAD_DIM
    e64 = np.zeros((LANES, di), np.float32)
    e128 = np.zeros((LANES, SSM_HEADS * LANES), np.float32)
    for h in range(SSM_HEADS):
        e64[SM_DT + h, h * SSM_HEAD_DIM:(h + 1) * SSM_HEAD_DIM] = 1.0
        e128[SM_DT + h, h * LANES:(h + 1) * LANES] = 1.0
    tril = np.tril(np.ones((SSM_CHUNK, SSM_CHUNK), np.float32))
    hpg = SSM_HEADS // SSM_GROUPS
    bmask = (np.arange(SSM_GROUPS * D_STATE)[:, None] // D_STATE
             == np.arange(di)[None, :] // (hpg * SSM_HEAD_DIM)).astype(np.float32)
    return dict(
        invf=invf, sgn=jnp.asarray(sgn),
        invf_s=jnp.where(jnp.asarray(first_head), invf, 0.0),
        sgn_s=jnp.asarray(np.where(first_head, sgn, 0.0).astype(np.float32)),
        bd=jnp.asarray(bd.astype(np.float32), dtype=BF16),
        e64=jnp.asarray(e64, dtype=BF16), e128=jnp.asarray(e128, dtype=BF16),
        tril=jnp.asarray(tril, dtype=BF16), bmask=jnp.asarray(bmask))


def _reorder_w_in(w_in):
    d = w_in.shape[0]
    sizes = dict(glu=1024, q=512, k=128, v=128, qi=512, ki=64, wi=8, z=512, xbc=768, dt=8, mq=512,
                 gates=4096)
    order_in = ["glu", "q", "k", "v", "qi", "ki", "wi", "z", "xbc", "dt", "mq", "gates"]
    parts, off = {}, 0
    for name in order_in:
        parts[name] = w_in[:, off:off + sizes[name]]
        off += sizes[name]
    assert off == w_in.shape[1]
    pad = jnp.zeros((d, LANES - sizes["ki"] - sizes["wi"] - sizes["dt"]), w_in.dtype)
    out = jnp.concatenate([parts[n] for n in ("glu", "q", "qi", "z", "mq", "xbc", "k", "v", "gates",
                                              "ki", "wi", "dt")] + [pad], axis=1)
    assert out.shape[1] == PROJ_COLS
    return out


def _pad_lanes(v, start):
    return jnp.zeros((1, LANES), F32).at[0, start:start + v.shape[0]].set(v)


def _tile(n, cap):
    return min(n, cap)


def _layer_params(l, prm, consts):
    p = dict(
        w_in=_reorder_w_in(prm["w_in"][l]).astype(MXU_DT),
        norm_mix_g=prm["norm_mix_g"][l][None, :],
        conv_a_w=prm["conv_a_w"][l], conv_a_b=prm["conv_a_b"][l][None, :],
        ln_a_g=prm["ln_a_g"][l][None, :], ln_a_b=prm["ln_a_b"][l][None, :],
        qg=jnp.tile(prm["q_norm_g"][l], N_HEADS)[None, :],
        kg=jnp.tile(prm["k_norm_g"][l], N_KV_HEADS)[None, :],
        ssm_conv_w=prm["ssm_conv_w"][l], ssm_conv_b=prm["ssm_conv_b"][l][None, :],
        dtb_row=_pad_lanes(prm["dt_bias"][l], SM_DT), alog_row=_pad_lanes(prm["a_log"][l], SM_DT),
        dskip_f=jnp.repeat(prm["d_skip"][l], SSM_HEAD_DIM)[None, :],
        ssm_norm_g=prm["ssm_norm_g"][l][None, :],
        mem_norm_g=prm["mem_norm_g"][l][None, :], w_mem_kv=prm["w_mem_kv"][l],
        mq_norm_g=prm["mq_norm_g"][l][None, :], mk_norm_g=prm["mk_norm_g"][l][None, :],
        w_branch=prm["w_branch"][l].astype(MXU_DT), w_out=prm["w_out"][l].astype(MXU_DT),
        norm_ffn_g=prm["norm_ffn_g"][l][None, :], w_ffn_up=prm["w_ffn_up"][l].astype(MXU_DT),
        ffn_conv_w=prm["ffn_conv_w"][l], ffn_conv_b=prm["ffn_conv_b"][l][None, :],
        w_ffn_down=prm["w_ffn_down"][l].astype(MXU_DT))
    return p


def _ssm_state_from_slab(slab):
    nseq = slab.shape[0]
    hpg = SSM_HEADS // SSM_GROUPS
    s = slab.reshape(nseq, SSM_GROUPS, D_STATE, SSM_HEADS, SSM_HEAD_DIM)
    per_head = [s[:, h // hpg, :, h, :] for h in range(SSM_HEADS)]
    return jnp.swapaxes(jnp.stack(per_head, axis=1), 2, 3)


def _prompt_layer(x, p, consts, mem2d, nseq, seq):
    t, d = x.shape
    nm = mem2d.shape[0] // nseq
    proj = _norm_proj(x, p["norm_mix_g"], p["w_in"], _tile(t, 1024), PROJ_COLS // 5)
    br_a, conf_new = _conf_prompt(proj, p["conv_a_w"], p["conv_a_b"], p["ln_a_g"], p["ln_a_b"],
                                  nseq, seq, _tile(seq, 512))
    ck = _tile(seq, 512)
    kn, v, kir, knb, smb, qt, qit, vt, smt = _dsa_prep_t(proj, p["qg"], p["kg"], consts, nseq, seq, ck)
    br_b = _dsa_prompt_t(qt, qit, smt, smb, knb, vt, nseq, seq, _tile(seq, 512), ck)
    br_c, sconv_new, sslab = _ssd_prompt(proj, p["ssm_conv_w"], p["ssm_conv_b"], p["dtb_row"],
                                         p["alog_row"], p["dskip_f"], p["ssm_norm_g"], consts,
                                         nseq, seq, _tile(seq, 512))
    mk, mv = _mem_kv(mem2d, p["mem_norm_g"], p["w_mem_kv"], p["mk_norm_g"], nm)
    dm = MEM_HEADS * MEM_HEAD_DIM
    br_m = _mem_attend(proj.reshape(nseq, seq, PROJ_COLS), mk.reshape(nseq, nm, dm),
                       mv.reshape(nseq, nm, dm), p["mq_norm_g"], _tile(seq, 512)).reshape(t, dm)
    x = _merge(x, proj, (br_a, br_b, br_c, br_m), p["w_branch"], p["w_out"], _tile(t, 256))
    x, ffn_new = _ffn_prompt(x, p["norm_ffn_g"], p["w_ffn_up"], p["ffn_conv_w"], p["ffn_conv_b"],
                             p["w_ffn_down"], nseq, seq, _tile(seq, 512), p["w_ffn_down"].shape[0] // 2)
    to_tok = lambda a: jnp.transpose(a.reshape(nseq, N_KV_HEADS, HEAD_DIM, seq), (0, 3, 1, 2))
    state = (to_tok(kn), to_tok(v), jnp.transpose(kir, (0, 2, 1)),
             mk.reshape(nseq, nm, MEM_HEADS, MEM_HEAD_DIM), mv.reshape(nseq, nm, MEM_HEADS, MEM_HEAD_DIM),
             conf_new, sconv_new, _ssm_state_from_slab(sslab), ffn_new)
    return x, state


def _sample_layer(x, p, consts, layer, page_table, ck_t, cv_t, cki_t, cmk, cmv, st_conf, st_sconv, st_ssm,
                  st_ffn):
    nb, d = x.shape
    n_pages = page_table.shape[1]
    page = ck_t.shape[3]
    past = n_pages * page
    dq = N_HEADS * HEAD_DIM
    dk = N_KV_HEADS * HEAD_DIM
    proj = _norm_proj(x, p["norm_mix_g"], p["w_in"], nb, PROJ_COLS // 5)
    br_a, a_new = _conf_sample(proj, jnp.swapaxes(st_conf, 0, 1), p["conv_a_w"], p["conv_a_b"],
                               p["ln_a_g"], p["ln_a_b"])
    qn, kn, v, qir, smr, kir = _dsa_prep(proj, p["qg"], p["kg"], consts, nb, 1, float(past))
    hpk = N_HEADS // N_KV_HEADS
    qh = qn.reshape(nb, N_HEADS, HEAD_DIM)
    grp = (np.arange(N_HEADS)[:, None] // hpk == np.arange(dk)[None, :] // HEAD_DIM)
    q8 = jnp.where(jnp.asarray(grp)[None], jnp.tile(qh, (1, 1, N_KV_HEADS)), 0.0)
    sc = _idx_sample(page_table, qir.reshape(nb, N_IDX_HEADS, D_IDX),
                     smr[:, SM_WI:SM_WI + N_IDX_HEADS].reshape(nb, N_IDX_HEADS, 1),
                     kir.reshape(nb, 1, D_IDX), cki_t, layer)
    bias = _topk_bias(sc.reshape(nb, past + LANES), min(TOP_K, (past + 1) // 4))
    o8 = _attn_sample(page_table, q8, bias.reshape(nb, 1, past + LANES), kn.reshape(nb, 1, dk),
                      v.reshape(nb, 1, dk), ck_t, cv_t, layer)
    o8 = o8.reshape(nb, N_HEADS, N_KV_HEADS, HEAD_DIM)
    br_b = jnp.stack([o8[:, h, h // hpk, :] for h in range(N_HEADS)], axis=1).reshape(nb, dq)
    h0 = st_ssm.reshape(nb, SSM_HEADS * SSM_HEAD_DIM, D_STATE)
    br_c, h_new = _ssd_sample(proj, jnp.swapaxes(st_sconv, 0, 1), h0, p["ssm_conv_w"], p["ssm_conv_b"],
                              p["dtb_row"], p["alog_row"], p["dskip_f"], p["ssm_norm_g"], consts)
    nm = cmk.shape[1]
    dm = MEM_HEADS * MEM_HEAD_DIM
    br_m = _mem_attend(proj.reshape(nb, 1, PROJ_COLS), cmk.reshape(nb, nm, dm), cmv.reshape(nb, nm, dm),
                       p["mq_norm_g"], 1).reshape(nb, dm)
    x = _merge(x, proj, (br_a, br_b, br_c, br_m), p["w_branch"], p["w_out"], nb)
    x, ug, uu = _ffn_sample(x, p["norm_ffn_g"], p["w_ffn_up"], jnp.swapaxes(st_ffn, 0, 1),
                            p["ffn_conv_w"], p["ffn_conv_b"], p["w_ffn_down"], 256)
    xbc_raw = proj[:, OFF["xbc"]:OFF["xbc"] + st_sconv.shape[-1]]
    state = (kn.reshape(nb, 1, N_KV_HEADS, HEAD_DIM), v.reshape(nb, 1, N_KV_HEADS, HEAD_DIM),
             kir.reshape(nb, 1, D_IDX),
             jnp.concatenate([st_conf[:, 1:], a_new[:, None]], axis=1),
             jnp.concatenate([st_sconv[:, 1:], xbc_raw[:, None]], axis=1),
             h_new.reshape(st_ssm.shape),
             jnp.concatenate([st_ffn[:, 1:], jnp.concatenate([ug, uu], axis=-1)[:, None]], axis=1))
    return x, state


def kernel(x_prompt, x_sample, cache_k, cache_v, cache_kidx, cache_mem_k, cache_mem_v, state_conformer, state_ssm_conv, state_ssm, state_ffn_conv, page_table, mem_prompt, norm_mix_g, w_in, conv_a_w, conv_a_b, ln_a_g, ln_a_b, q_norm_g, k_norm_g, ssm_conv_w, ssm_conv_b, dt_bias, a_log, d_skip, ssm_norm_g, mem_norm_g, w_mem_kv, mq_norm_g, mk_norm_g, w_branch, w_out, norm_ffn_g, w_ffn_up, ffn_conv_w, ffn_conv_b, w_ffn_down):
    prm = dict(norm_mix_g=norm_mix_g, w_in=w_in, conv_a_w=conv_a_w, conv_a_b=conv_a_b, ln_a_g=ln_a_g,
               ln_a_b=ln_a_b, q_norm_g=q_norm_g, k_norm_g=k_norm_g, ssm_conv_w=ssm_conv_w,
               ssm_conv_b=ssm_conv_b, dt_bias=dt_bias, a_log=a_log, d_skip=d_skip, ssm_norm_g=ssm_norm_g,
               mem_norm_g=mem_norm_g, w_mem_kv=w_mem_kv, mq_norm_g=mq_norm_g, mk_norm_g=mk_norm_g,
               w_branch=w_branch, w_out=w_out, norm_ffn_g=norm_ffn_g, w_ffn_up=w_ffn_up,
               ffn_conv_w=ffn_conv_w, ffn_conv_b=ffn_conv_b, w_ffn_down=w_ffn_down)
    depth = w_in.shape[0]
    nseq, seq, d = x_prompt.shape
    nb, dseq, _ = x_sample.shape
    assert dseq == 1
    consts = _constants()
    xp = x_prompt.reshape(nseq * seq, d)
    xs = x_sample.reshape(nb, d)
    mem2d = mem_prompt.reshape(nseq * mem_prompt.shape[1], d)
    n_phys, page = cache_k.shape[1], cache_k.shape[2]
    ck_t = jnp.transpose(cache_k, (0, 1, 3, 4, 2)).reshape(depth, n_phys, N_KV_HEADS * HEAD_DIM, page)
    cv_t = jnp.transpose(cache_v, (0, 1, 3, 4, 2)).reshape(depth, n_phys, N_KV_HEADS * HEAD_DIM, page)
    cki_t = jnp.transpose(cache_kidx, (0, 1, 3, 2))
    p_states, s_states = [], []
    for l in range(depth):
        p = _layer_params(l, prm, consts)
        xp, st = _prompt_layer(xp, p, consts, mem2d, nseq, seq)
        p_states.append(st)
        xs, st = _sample_layer(xs, p, consts, l, page_table, ck_t, cv_t, cki_t,
                               cache_mem_k[l], cache_mem_v[l], state_conformer[l], state_ssm_conv[l],
                               state_ssm[l], state_ffn_conv[l])
        s_states.append(st)
    stack = lambda states, k: jnp.stack([s[k] for s in states])
    return ((xp.reshape(nseq, seq, d), xs.reshape(nb, 1, d))
            + tuple(stack(p_states, k) for k in range(9))
            + tuple(stack(s_states, k) for k in range(7)))
```

```python
import functools
import math

import numpy as np
import jax
import jax.numpy as jnp
from jax import lax
from jax.experimental import pallas as pl
from jax.experimental.pallas import tpu as pltpu

F32 = jnp.float32
BF16 = jnp.bfloat16
I32 = jnp.int32
MXU_DT = BF16

EPS = 1e-6
ROPE_THETA = 500000.0
LANES = 128
V7X_VMEM_LIMIT = 56 * 1024 * 1024

N_HEADS = 8
HEAD_DIM = 64
N_KV_HEADS = 2
N_IDX_HEADS = 8
D_IDX = 64
TOP_K = 256
SSM_HEADS = 8
SSM_HEAD_DIM = 64
SSM_GROUPS = 2
D_STATE = 64
SSM_CHUNK = 128
MEM_HEADS = 4
MEM_HEAD_DIM = 128
IDX_SCALE = (D_IDX ** -0.5) * (N_IDX_HEADS ** -0.5)
NEG_BIG = -1e30
LOG2E = math.log2(math.e)
INT_MIN = -2 ** 31

SM_KI = 0
SM_WI = 64
SM_DT = 72

OFF = dict(glu=0, q=1024, qi=1536, z=2048, mq=2560, xbc=3072, k=3840, v=3968, gates=4096, small=8192)
PROJ_COLS = 8320


def _cp(sem):
    return pltpu.CompilerParams(dimension_semantics=sem, vmem_limit_bytes=V7X_VMEM_LIMIT)


def _nt_dot(a, b):
    return lax.dot_general(a, b, (((1,), (1,)), ((), ())), preferred_element_type=F32)


def _dot(a, b):
    return jnp.dot(a, b, preferred_element_type=F32)


def _split3(x):
    hi = x.astype(BF16)
    r = x - hi.astype(F32)
    mid = r.astype(BF16)
    lo = (r - mid.astype(F32)).astype(BF16)
    return hi, mid, lo


def _dot_sel(x, e):
    hi, mid, lo = _split3(x)
    return _dot(hi, e) + _dot(mid, e) + _dot(lo, e)


def _sel_dot(e, x):
    hi, mid, lo = _split3(x)
    return _dot(e, hi) + _dot(e, mid) + _dot(e, lo)


def _silu(x):
    return x * jax.nn.sigmoid(x)


def _softplus(x):
    return jnp.maximum(x, 0.0) + jnp.log1p(jnp.exp(-jnp.abs(x)))


def _rms(x):
    return x * lax.rsqrt(jnp.mean(x * x, axis=-1, keepdims=True) + EPS)


def _sortable(x):
    b = lax.bitcast_convert_type(x, I32)
    b = jnp.where(x == 0.0, 0, b)
    return jnp.where(b < 0, b ^ 0x7FFFFFFF, b)


def _norm_proj_kernel(x_ref, g_ref, w_ref, o_ref, h_ref):
    @pl.when(pl.program_id(1) == 0)
    def _():
        h_ref[...] = (_rms(x_ref[...]) * g_ref[...]).astype(h_ref.dtype)

    o_ref[...] = _dot(h_ref[...], w_ref[...])


def _norm_proj(x, g, w, tm, tn):
    t, d = x.shape
    n = w.shape[1]
    return pl.pallas_call(
        _norm_proj_kernel,
        grid=(t // tm, n // tn),
        in_specs=[pl.BlockSpec((tm, d), lambda i, j: (i, 0)),
                  pl.BlockSpec((1, d), lambda i, j: (0, 0)),
                  pl.BlockSpec((d, tn), lambda i, j: (0, j))],
        out_specs=pl.BlockSpec((tm, tn), lambda i, j: (i, j)),
        out_shape=jax.ShapeDtypeStruct((t, n), F32),
        scratch_shapes=[pltpu.VMEM((tm, d), MXU_DT)],
        compiler_params=_cp(("parallel", "arbitrary")),
        name="norm_proj",
    )(x, g, w)


def _layernorm_silu(u, g, b):
    xc = u - jnp.mean(u, axis=-1, keepdims=True)
    var = jnp.mean(xc * xc, axis=-1, keepdims=True)
    return _silu(xc * lax.rsqrt(var + EPS) * g + b)


def _conf_prompt_kernel(glu_ref, w_ref, b_ref, lg_ref, lb_ref, o_ref, cnew_ref, abuf, *, tm, cw, dc):
    halo = 32

    @pl.when(pl.program_id(1) == 0)
    def _():
        abuf[0:halo, :] = jnp.zeros((halo, dc), F32)

    glu = glu_ref[...]
    abuf[halo:halo + tm, :] = glu[:, :dc] * jax.nn.sigmoid(glu[:, dc:])
    first = halo - (cw - 1)
    u = b_ref[...] + w_ref[0:1, :] * abuf[first:first + tm, :]
    for j in range(1, cw):
        u = u + w_ref[j:j + 1, :] * abuf[first + j:first + j + tm, :]
    o_ref[...] = _layernorm_silu(u, lg_ref[...], lb_ref[...])
    cnew_ref[0] = abuf[tm + first:tm + halo, :]
    abuf[0:halo, :] = abuf[tm:tm + halo, :]


def _conf_prompt(proj, w, b, lg, lb, nseq, seq, tm):
    cw, dc = w.shape
    nt = seq // tm
    kern = functools.partial(_conf_prompt_kernel, tm=tm, cw=cw, dc=dc)
    return pl.pallas_call(
        kern,
        grid=(nseq, nt),
        in_specs=[pl.BlockSpec((tm, 2 * dc), lambda s, i: (s * nt + i, OFF["glu"] // (2 * dc))),
                  pl.BlockSpec((cw, dc), lambda s, i: (0, 0)),
                  pl.BlockSpec((1, dc), lambda s, i: (0, 0)),
                  pl.BlockSpec((1, dc), lambda s, i: (0, 0)),
                  pl.BlockSpec((1, dc), lambda s, i: (0, 0))],
        out_specs=[pl.BlockSpec((tm, dc), lambda s, i: (s * nt + i, 0)),
                   pl.BlockSpec((1, cw - 1, dc), lambda s, i: (s, 0, 0))],
        out_shape=[jax.ShapeDtypeStruct((nseq * seq, dc), F32),
                   jax.ShapeDtypeStruct((nseq, cw - 1, dc), F32)],
        scratch_shapes=[pltpu.VMEM((tm + 32, dc), F32)],
        compiler_params=_cp(("arbitrary", "arbitrary")),
        name="conformer_prompt",
    )(proj, w, b, lg, lb)


def _conf_sample_kernel(glu_ref, st_ref, w_ref, b_ref, lg_ref, lb_ref, o_ref, a_ref, *, cw, dc):
    glu = glu_ref[...]
    a = glu[:, :dc] * jax.nn.sigmoid(glu[:, dc:])
    u = b_ref[...] + w_ref[cw - 1:cw, :] * a
    for j in range(cw - 1):
        u = u + w_ref[j:j + 1, :] * st_ref[j]
    o_ref[...] = _layernorm_silu(u, lg_ref[...], lb_ref[...])
    a_ref[...] = a


def _conf_sample(proj, st_t, w, b, lg, lb):
    cw, dc = w.shape
    n = proj.shape[0]
    kern = functools.partial(_conf_sample_kernel, cw=cw, dc=dc)
    return pl.pallas_call(
        kern,
        grid=(1,),
        in_specs=[pl.BlockSpec((n, 2 * dc), lambda i: (0, OFF["glu"] // (2 * dc))),
                  pl.BlockSpec((cw - 1, n, dc), lambda i: (0, 0, 0)),
                  pl.BlockSpec((cw, dc), lambda i: (0, 0)),
                  pl.BlockSpec((1, dc), lambda i: (0, 0)),
                  pl.BlockSpec((1, dc), lambda i: (0, 0)),
                  pl.BlockSpec((1, dc), lambda i: (0, 0))],
        out_specs=[pl.BlockSpec((n, dc), lambda i: (0, 0)),
                   pl.BlockSpec((n, dc), lambda i: (0, 0))],
        out_shape=[jax.ShapeDtypeStruct((n, dc), F32), jax.ShapeDtypeStruct((n, dc), F32)],
        compiler_params=_cp(("arbitrary",)),
        name="conformer_sample",
    )(proj, st_t, w, b, lg, lb)


def _rope128(x, c, s):
    lane = lax.broadcasted_iota(I32, x.shape, 1) % HEAD_DIM
    partner = jnp.where(lane < 8, pltpu.roll(x, LANES - 8, 1), pltpu.roll(x, 8, 1))
    return x * c + partner * s


def _dsa_prep_kernel(q_ref, qi_ref, k_ref, v_ref, sm_ref, qg_ref, kg_ref, invf_ref, sgn_ref,
                     invf_s_ref, sgn_s_ref, bd_ref,
                     qn_ref, kn_ref, vo_ref, qir_ref, smr_ref, kir_ref, *, tm, seq, const_pos):
    if const_pos is None:
        base = (pl.program_id(0) * tm) % seq
        pos = (base + lax.broadcasted_iota(I32, (tm, LANES), 0)).astype(F32)
    else:
        pos = jnp.full((tm, LANES), const_pos, F32)
    ang = pos * invf_ref[...]
    c = jnp.cos(ang)
    s = jnp.sin(ang) * sgn_ref[...]
    ang_s = pos * invf_s_ref[...]
    c_s = jnp.cos(ang_s)
    s_s = jnp.sin(ang_s) * sgn_s_ref[...]

    def seg_rms(x, bd):
        x2 = x * x
        hi = x2.astype(BF16)
        lo = (x2 - hi.astype(F32)).astype(BF16)
        ms = (_dot(hi, bd) + _dot(lo, bd)) * (1.0 / HEAD_DIM)
        return x * lax.rsqrt(ms + EPS)

    qn = seg_rms(q_ref[...], bd_ref[...]) * qg_ref[...]
    qi = qi_ref[...]
    for j in range(q_ref.shape[1] // LANES):
        sl = slice(j * LANES, (j + 1) * LANES)
        qn_ref[:, sl] = _rope128(qn[:, sl], c, s)
        qir_ref[:, sl] = _rope128(qi[:, sl], c, s)
    kn = seg_rms(k_ref[...], bd_ref[0:LANES, 0:LANES]) * kg_ref[...]
    kn_ref[...] = _rope128(kn, c, s)
    vo_ref[...] = v_ref[...]
    smr = _rope128(sm_ref[...], c_s, s_s)
    smr_ref[...] = smr
    kir_ref[...] = smr[:, 0:D_IDX]


def _dsa_prep(proj, qg, kg, consts, tm, seq, const_pos):
    t = proj.shape[0]
    dq = N_HEADS * HEAD_DIM
    dk = N_KV_HEADS * HEAD_DIM
    kern = functools.partial(_dsa_prep_kernel, tm=tm, seq=seq, const_pos=const_pos)
    row = lambda w: pl.BlockSpec((1, w), lambda i: (0, 0))
    return pl.pallas_call(
        kern,
        grid=(t // tm,),
        in_specs=[pl.BlockSpec((tm, dq), lambda i: (i, OFF["q"] // dq)),
                  pl.BlockSpec((tm, dq), lambda i: (i, OFF["qi"] // dq)),
                  pl.BlockSpec((tm, dk), lambda i: (i, OFF["k"] // dk)),
                  pl.BlockSpec((tm, dk), lambda i: (i, OFF["v"] // dk)),
                  pl.BlockSpec((tm, LANES), lambda i: (i, OFF["small"] // LANES)),
                  row(dq), row(dk), row(LANES), row(LANES), row(LANES), row(LANES),
                  pl.BlockSpec((dq, dq), lambda i: (0, 0))],
        out_specs=[pl.BlockSpec((tm, dq), lambda i: (i, 0)),
                   pl.BlockSpec((tm, dk), lambda i: (i, 0)),
                   pl.BlockSpec((tm, dk), lambda i: (i, 0)),
                   pl.BlockSpec((tm, dq), lambda i: (i, 0)),
                   pl.BlockSpec((tm, LANES), lambda i: (i, 0)),
                   pl.BlockSpec((tm, D_IDX), lambda i: (i, 0))],
        out_shape=[jax.ShapeDtypeStruct((t, dq), F32), jax.ShapeDtypeStruct((t, dk), F32),
                   jax.ShapeDtypeStruct((t, dk), F32), jax.ShapeDtypeStruct((t, dq), F32),
                   jax.ShapeDtypeStruct((t, LANES), F32), jax.ShapeDtypeStruct((t, D_IDX), F32)],
        compiler_params=_cp(("parallel",)),
        name="dsa_prep",
    )(proj, proj, proj, proj, proj, qg, kg, consts["invf"], consts["sgn"], consts["invf_s"],
      consts["sgn_s"], consts["bd"])


def _dsa_prep_t_kernel(q_ref, qi_ref, k_ref, v_ref, sm_ref, qg_ref, kg_ref, invf_ref, sgn_ref,
                       invf_s_ref, sgn_s_ref, bd_ref,
                       kn_ref, vo_ref, kir_ref, knb_ref, smb_ref, qt_ref, qit_ref, vt_ref, smt_ref, *, tm):
    pos = (pl.program_id(1) * tm + lax.broadcasted_iota(I32, (tm, LANES), 0)).astype(F32)
    ang = pos * invf_ref[...]
    c = jnp.cos(ang)
    s = jnp.sin(ang) * sgn_ref[...]
    ang_s = pos * invf_s_ref[...]
    c_s = jnp.cos(ang_s)
    s_s = jnp.sin(ang_s) * sgn_s_ref[...]

    def seg_rms(x, bd):
        x2 = x * x
        hi = x2.astype(BF16)
        lo = (x2 - hi.astype(F32)).astype(BF16)
        ms = (_dot(hi, bd) + _dot(lo, bd)) * (1.0 / HEAD_DIM)
        return x * lax.rsqrt(ms + EPS)

    lo_half = lax.broadcasted_iota(I32, (tm, LANES), 1) < HEAD_DIM
    qn = seg_rms(q_ref[...], bd_ref[...]) * qg_ref[...]
    qi = qi_ref[...]
    for j in range(N_HEADS // 2):
        sl = slice(j * LANES, (j + 1) * LANES)
        q2 = _rope128(qn[:, sl], c, s) * (HEAD_DIM ** -0.5 * LOG2E)
        q2r = pltpu.roll(q2, HEAD_DIM, 1)
        if (2 * j) // (N_HEADS // N_KV_HEADS) == 0:
            qa, qb = jnp.where(lo_half, q2, 0.0), jnp.where(lo_half, q2r, 0.0)
        else:
            qa, qb = jnp.where(lo_half, 0.0, q2r), jnp.where(lo_half, 0.0, q2)
        qt_ref[0, 2 * j] = qa.T.astype(MXU_DT)
        qt_ref[0, 2 * j + 1] = qb.T.astype(MXU_DT)
        qi2 = _rope128(qi[:, sl], c, s)
        qit_ref[0, 2 * j] = jnp.where(lo_half, qi2, 0.0).T.astype(MXU_DT)
        qit_ref[0, 2 * j + 1] = jnp.where(lo_half, pltpu.roll(qi2, HEAD_DIM, 1), 0.0).T.astype(MXU_DT)
    kn = _rope128(seg_rms(k_ref[...], bd_ref[0:LANES, 0:LANES]) * kg_ref[...], c, s)
    kn_ref[0] = kn.T
    knb_ref[...] = kn.astype(MXU_DT)
    v_t = v_ref[...].T
    vo_ref[0] = v_t
    vt_ref[0, 0] = v_t.astype(MXU_DT)
    smr = _rope128(sm_ref[...], c_s, s_s)
    smb_ref[...] = smr.astype(MXU_DT)
    sm_t = smr.T
    kir_ref[0] = sm_t[0:D_IDX, :]
    smt_ref[0] = sm_t


def _dsa_prep_t(proj, qg, kg, consts, nseq, seq, tm):
    t = proj.shape[0]
    dq = N_HEADS * HEAD_DIM
    dk = N_KV_HEADS * HEAD_DIM
    nt = seq // tm
    kern = functools.partial(_dsa_prep_t_kernel, tm=tm)
    row = lambda w: pl.BlockSpec((1, w), lambda s, i: (0, 0))
    tok = lambda w, col: pl.BlockSpec((tm, w), lambda s, i: (s * nt + i, col))
    return pl.pallas_call(
        kern,
        grid=(nseq, nt),
        in_specs=[tok(dq, OFF["q"] // dq), tok(dq, OFF["qi"] // dq), tok(dk, OFF["k"] // dk),
                  tok(dk, OFF["v"] // dk), tok(LANES, OFF["small"] // LANES),
                  row(dq), row(dk), row(LANES), row(LANES), row(LANES), row(LANES),
                  pl.BlockSpec((dq, dq), lambda s, i: (0, 0))],
        out_specs=[pl.BlockSpec((1, dk, tm), lambda s, i: (s, 0, i)),
                   pl.BlockSpec((1, dk, tm), lambda s, i: (s, 0, i)),
                   pl.BlockSpec((1, D_IDX, tm), lambda s, i: (s, 0, i)),
                   tok(dk, 0), tok(LANES, 0),
                   pl.BlockSpec((1, N_HEADS, LANES, tm), lambda s, i: (s, 0, 0, i)),
                   pl.BlockSpec((1, N_IDX_HEADS, LANES, tm), lambda s, i: (s, 0, 0, i)),
                   pl.BlockSpec((1, 1, dk, tm), lambda s, i: (s, i, 0, 0)),
                   pl.BlockSpec((1, LANES, tm), lambda s, i: (s, 0, i))],
        out_shape=[jax.ShapeDtypeStruct((nseq, dk, seq), F32), jax.ShapeDtypeStruct((nseq, dk, seq), F32),
                   jax.ShapeDtypeStruct((nseq, D_IDX, seq), F32),
                   jax.ShapeDtypeStruct((t, dk), MXU_DT), jax.ShapeDtypeStruct((t, LANES), MXU_DT),
                   jax.ShapeDtypeStruct((nseq, N_HEADS, LANES, seq), MXU_DT),
                   jax.ShapeDtypeStruct((nseq, N_IDX_HEADS, LANES, seq), MXU_DT),
                   jax.ShapeDtypeStruct((nseq, nt, dk, tm), MXU_DT),
                   jax.ShapeDtypeStruct((nseq, LANES, seq), F32)],
        compiler_params=_cp(("parallel", "parallel")),
        name="dsa_prep_prompt",
    )(proj, proj, proj, proj, proj, qg, kg, consts["invf"], consts["sgn"], consts["invf_s"],
      consts["sgn_s"], consts["bd"])


def _dsa_prompt_t_kernel(qt_ref, qit_ref, smt_ref, smk_ref, k_ref, vt_ref, tri_ref, o_ref,
                         keys_ref, k16_ref, acc_ref, thr_ref, ngt_ref, neq_ref, *, tq, ck, ksel):
    i = pl.program_id(1)
    nc = (i * tq + tq + ck - 1) // ck
    qidx = i * tq + lax.broadcasted_iota(I32, (ck, tq), 1)
    krow = lax.broadcasted_iota(I32, (ck, tq), 0)

    def p1(c, carry):
        kc = smk_ref[pl.ds(pl.multiple_of(c * ck, ck), ck), :]
        acc = jnp.zeros((ck, tq), F32)
        for h in range(N_IDX_HEADS):
            s = _dot(kc, qit_ref[0, h])
            acc = acc + jnp.maximum(s, 0.0) * smt_ref[0, SM_WI + h:SM_WI + h + 1, :]
        causal = c * ck + krow <= qidx
        sc = acc * IDX_SCALE
        bits = jnp.where(sc == 0.0, 0, lax.bitcast_convert_type(sc, I32))
        keys_ref[c] = jnp.where(causal, jnp.where(bits < 0, bits ^ 0x7FFFFFFF, bits), INT_MIN)
        half = lax.bitcast_convert_type(jnp.where(causal, bits & -65536, -1), F32)
        k16_ref[c] = half.astype(BF16)
        return carry

    lax.fori_loop(0, nc, p1, 0)

    def count(pred):
        def body(c, part):
            hit = pred(keys_ref[c], c * ck + krow)
            return part + jnp.sum(hit.reshape(ck // 8, 8, tq), axis=0)

        part = lax.fori_loop(0, nc, body, jnp.zeros((8, tq), I32))
        return jnp.sum(part, axis=0, keepdims=True)

    def bisect(thr0, nsteps):
        def bit_step(t, thr):
            cand = thr + lax.shift_left(jnp.int32(1), nsteps - 1 - t)
            cnt = count(lambda kv, kidx: jnp.where(kv >= cand, 1, 0))
            return jnp.where(cnt >= ksel, cand, thr)

        return lax.fori_loop(0, nsteps, bit_step, thr0)

    one16 = jnp.ones((ck, tq), BF16)
    zero16 = jnp.zeros((ck, tq), BF16)

    def bit16_step(t, thr16):
        cand = thr16 + lax.shift_left(jnp.int32(1), 15 - t)
        raw = jnp.where(cand < 0, cand ^ 0x7FFF, cand)
        raw = jnp.where((raw > 0) & (raw < 0x80), 0x80, raw)
        cand_f = lax.bitcast_convert_type(lax.shift_left(raw, 16), F32).astype(BF16)

        def body(c, part):
            hit = jnp.where(k16_ref[c] >= cand_f, one16, zero16).reshape(ck // 16, 16, tq)
            tot = hit[0]
            for r in range(1, ck // 16):
                tot = tot + hit[r]
            return part + tot.astype(F32)

        part = lax.fori_loop(0, nc, body, jnp.zeros((16, tq), F32))
        cnt = jnp.sum(part, axis=0, keepdims=True)
        return jnp.where(cnt >= ksel, cand, thr16)

    thr16 = lax.fori_loop(0, 16, bit16_step, jnp.full((1, tq), -(2 ** 15), I32))
    thr = bisect(lax.shift_left(thr16, 16), 16)

    def tallies(thr):
        return (count(lambda kv, kidx: jnp.where(kv > thr, 1, 0)),
                count(lambda kv, kidx: jnp.where(kv == thr, 1, 0)))

    n_gt, n_eq = tallies(thr)
    thr_ref[...] = thr
    ngt_ref[...] = n_gt
    neq_ref[...] = n_eq
    missed = jnp.max(jnp.where((n_gt >= ksel) | (n_gt + n_eq < ksel), 1, 0))

    @pl.when(missed > 0)
    def _():
        thr_full = bisect(jnp.full((1, tq), INT_MIN, I32), 32)
        n_gt_full, n_eq_full = tallies(thr_full)
        thr_ref[...] = thr_full
        ngt_ref[...] = n_gt_full
        neq_ref[...] = n_eq_full

    thr = thr_ref[...]
    need = ksel - ngt_ref[...]
    excess = jnp.max(jnp.where((neq_ref[...] > need) & (thr > INT_MIN), 1, 0))

    @pl.when(excess > 0)
    def _():
        keep = jnp.where(thr > INT_MIN, need, 2 ** 30).astype(F32)

        def demote(c, seen):
            kv = keys_ref[c]
            tied = jnp.where(kv == thr, 1.0, 0.0)
            prefix = _dot(tri_ref[...], tied.astype(BF16))
            surplus = jnp.where(seen + prefix > keep, tied, 0.0)
            keys_ref[c] = jnp.where(surplus > 0.0, thr - 1, kv)
            return seen + prefix[ck - 1:ck, :]

        lax.fori_loop(0, nc, demote, jnp.zeros((1, tq), F32))

    thr_sel = jnp.maximum(thr, INT_MIN + 1)

    acc_ref[...] = jnp.zeros(acc_ref.shape, F32)

    def p3(c, carry):
        ms, ls = carry
        bias = jnp.where(keys_ref[c] >= thr_sel, 0.0, NEG_BIG)
        kk = k_ref[pl.ds(pl.multiple_of(c * ck, ck), ck), :]
        vt = vt_ref[0, c]
        m_out, l_out, ps, alphas = [], [], [], []
        for h in range(N_HEADS):
            s = _dot(kk, qt_ref[0, h]) + bias
            m_new = jnp.maximum(ms[h], jnp.max(s, axis=0, keepdims=True))
            alpha = jnp.exp2(ms[h] - m_new)
            p = jnp.exp2(s - m_new)
            m_out.append(m_new)
            l_out.append(alpha * ls[h] + jnp.sum(p, axis=0, keepdims=True))
            alphas.append(alpha)
            ps.append(p.astype(MXU_DT))
        for h in range(N_HEADS):
            acc_ref[h] = alphas[h] * acc_ref[h] + _dot(vt, ps[h])
        return tuple(m_out), tuple(l_out)

    init = (tuple(jnp.full((1, tq), NEG_BIG, F32) for _ in range(N_HEADS)),
            tuple(jnp.zeros((1, tq), F32) for _ in range(N_HEADS)))
    _, lrow = lax.fori_loop(0, nc, p3, init)

    lo_half = lax.broadcasted_iota(I32, (tq, LANES), 1) < HEAD_DIM
    for j in range(N_HEADS // 2):
        ea = (acc_ref[2 * j] / lrow[2 * j]).T
        eb = (acc_ref[2 * j + 1] / lrow[2 * j + 1]).T
        if (2 * j) // (N_HEADS // N_KV_HEADS) == 0:
            out2 = jnp.where(lo_half, ea, pltpu.roll(eb, HEAD_DIM, 1))
        else:
            out2 = jnp.where(lo_half, pltpu.roll(ea, HEAD_DIM, 1), eb)
        o_ref[:, j * LANES:(j + 1) * LANES] = out2


def _dsa_prompt_t(qt, qit, smt, smr, kn, vt, nseq, seq, tq, ck):
    dq = N_HEADS * HEAD_DIM
    dk = N_KV_HEADS * HEAD_DIM
    nq = seq // tq
    nck = seq // ck
    ksel = min(TOP_K, seq // 4)
    assert ck >= ksel and ck % LANES == 0 and seq % ck == 0 and seq % tq == 0 and vt.shape[3] == ck
    kern = functools.partial(_dsa_prompt_t_kernel, tq=tq, ck=ck, ksel=ksel)
    return pl.pallas_call(
        kern,
        grid=(nseq, nq),
        in_specs=[pl.BlockSpec((1, N_HEADS, LANES, tq), lambda s, i: (s, 0, 0, i)),
                  pl.BlockSpec((1, N_IDX_HEADS, LANES, tq), lambda s, i: (s, 0, 0, i)),
                  pl.BlockSpec((1, LANES, tq), lambda s, i: (s, 0, i)),
                  pl.BlockSpec((seq, LANES), lambda s, i: (s, 0)),
                  pl.BlockSpec((seq, dk), lambda s, i: (s, 0)),
                  pl.BlockSpec((1, nck, dk, ck), lambda s, i: (s, 0, 0, 0)),
                  pl.BlockSpec((ck, ck), lambda s, i: (0, 0))],
        out_specs=pl.BlockSpec((tq, dq), lambda s, i: (s * nq + i, 0)),
        out_shape=jax.ShapeDtypeStruct((nseq * seq, dq), F32),
        scratch_shapes=[pltpu.VMEM((nck, ck, tq), I32),
                        pltpu.VMEM((nck, ck, tq), BF16),
                        pltpu.VMEM((N_HEADS, dk, tq), F32),
                        pltpu.VMEM((1, tq), I32), pltpu.VMEM((1, tq), I32), pltpu.VMEM((1, tq), I32)],
        compiler_params=_cp(("arbitrary", "arbitrary")),
        name="dsa_prompt",
    )(qt, qit, smt, smr, kn, vt, jnp.asarray(np.tril(np.ones((ck, ck), np.float32)), dtype=BF16))


def _page_copies(hbm, layer, pt_ref, b, buf, slot, sem, n_pages, page):
    return [pltpu.make_async_copy(hbm.at[layer, pt_ref[b, p]],
                                  buf.at[slot, :, p * page:(p + 1) * page], sem.at[slot])
            for p in range(n_pages)]


def _prefetch_pages(hbms, bufs, sems, layer, pt_ref, n_pages, page):
    b = pl.program_id(0)
    nb = pl.num_programs(0)
    slot = b % 2

    def start(bb, sl):
        for hbm, buf, sem in zip(hbms, bufs, sems):
            for cp in _page_copies(hbm, layer, pt_ref, bb, buf, sl, sem, n_pages, page):
                cp.start()

    @pl.when(b == 0)
    def _():
        start(0, 0)

    @pl.when(b + 1 < nb)
    def _():
        start(b + 1, 1 - slot)

    for hbm, buf, sem in zip(hbms, bufs, sems):
        for cp in _page_copies(hbm, layer, pt_ref, b, buf, slot, sem, n_pages, page):
            cp.wait()
    return slot


def _idx_sample_kernel(pt_ref, qi8_ref, wi_ref, kin_ref, cki_hbm, sc_ref, kibuf, sem,
                       *, layer, n_pages, page):
    past = n_pages * page
    slot = _prefetch_pages((cki_hbm,), (kibuf,), (sem,), layer, pt_ref, n_pages, page)
    wcol = wi_ref[0]
    qi8 = qi8_ref[0]
    s = _dot(qi8, kibuf[slot])
    sc_ref[0, :, 0:past] = jnp.sum(jnp.maximum(s, 0.0) * wcol, axis=0, keepdims=True) * IDX_SCALE
    s_new = jnp.sum(qi8 * kin_ref[0], axis=1, keepdims=True)
    sc_new = jnp.sum(jnp.maximum(s_new, 0.0) * wcol, axis=0, keepdims=True) * IDX_SCALE
    lane = lax.broadcasted_iota(I32, (1, LANES), 1)
    sc_ref[0, :, past:past + LANES] = jnp.where(lane == 0, sc_new, -jnp.inf)


def _idx_sample(page_table, qi8, wi, ki_new, cki_t, layer):
    nb, n_pages = page_table.shape
    page = cki_t.shape[3]
    past = n_pages * page
    kern = functools.partial(_idx_sample_kernel, layer=layer, n_pages=n_pages, page=page)
    grid_spec = pltpu.PrefetchScalarGridSpec(
        num_scalar_prefetch=1,
        grid=(nb,),
        in_specs=[pl.BlockSpec((1, N_IDX_HEADS, D_IDX), lambda b, pt: (b, 0, 0)),
                  pl.BlockSpec((1, N_IDX_HEADS, 1), lambda b, pt: (b, 0, 0)),
                  pl.BlockSpec((1, 1, D_IDX), lambda b, pt: (b, 0, 0)),
                  pl.BlockSpec(memory_space=pl.ANY)],
        out_specs=pl.BlockSpec((1, 1, past + LANES), lambda b, pt: (b, 0, 0)),
        scratch_shapes=[pltpu.VMEM((2, D_IDX, past), F32), pltpu.SemaphoreType.DMA((2,))],
    )
    return pl.pallas_call(
        kern, grid_spec=grid_spec,
        out_shape=jax.ShapeDtypeStruct((nb, 1, past + LANES), F32),
        compiler_params=_cp(("arbitrary",)),
        name="idx_sample",
    )(page_table, qi8, wi, ki_new, cki_t)


def _topk_bias_kernel(sc_ref, bias_ref, *, ksel, nbits):
    keys = _sortable(sc_ref[...])
    col = lax.broadcasted_iota(I32, keys.shape, 1)

    def cnt(hit):
        return jnp.sum(hit, axis=1, keepdims=True)

    def bit_step(t, thr):
        cand = thr + lax.shift_left(jnp.int32(1), 31 - t)
        return jnp.where(cnt(jnp.where(keys >= cand, 1, 0)) >= ksel, cand, thr)

    thr = lax.fori_loop(0, 32, bit_step, jnp.full((keys.shape[0], 1), INT_MIN, I32))
    need = ksel - cnt(jnp.where(keys > thr, 1, 0))

    def y_step(t, y):
        cand = y + lax.shift_left(jnp.int32(1), nbits - 1 - t)
        g = cnt(jnp.where(keys == thr, jnp.where(col < cand, 1, 0), 0))
        return jnp.where(g < need, cand, y)

    y = lax.fori_loop(0, nbits, y_step, jnp.zeros((keys.shape[0], 1), I32))
    sel = jnp.where(keys > thr, 1, jnp.where(keys == thr, jnp.where(col <= y, 1, 0), 0))
    bias_ref[...] = jnp.where(sel > 0, 0.0, NEG_BIG)


def _topk_bias(sc, ksel):
    nb, width = sc.shape
    nbits = int(math.floor(math.log2(width))) + 1
    kern = functools.partial(_topk_bias_kernel, ksel=ksel, nbits=nbits)
    return pl.pallas_call(
        kern, grid=(1,),
        in_specs=[pl.BlockSpec((nb, width), lambda i: (0, 0))],
        out_specs=pl.BlockSpec((nb, width), lambda i: (0, 0)),
        out_shape=jax.ShapeDtypeStruct((nb, width), F32),
        compiler_params=_cp(("arbitrary",)),
        name="topk_bias_sample",
    )(sc)


def _attn_sample_kernel(pt_ref, q8_ref, bias_ref, kn_ref, vn_ref, ck_hbm, cv_hbm, o_ref,
                        kbuf, vbuf, ksem, vsem, *, layer, n_pages, page):
    past = n_pages * page
    slot = _prefetch_pages((ck_hbm, cv_hbm), (kbuf, vbuf), (ksem, vsem), layer, pt_ref, n_pages, page)
    q8 = q8_ref[0] * (HEAD_DIM ** -0.5)
    sa = _dot(q8, kbuf[slot]) + bias_ref[0, :, 0:past]
    sa_new = jnp.sum(q8 * kn_ref[0], axis=1, keepdims=True) + bias_ref[0, :, past:past + 1]
    m = jnp.maximum(jnp.max(sa, axis=1, keepdims=True), sa_new)
    p = jnp.exp(sa - m)
    p_new = jnp.exp(sa_new - m)
    l = jnp.sum(p, axis=1, keepdims=True) + p_new
    o_ref[0] = (_nt_dot(p, vbuf[slot]) + p_new * vn_ref[0]) / l


def _attn_sample(page_table, q8, bias, k_new, v_new, ck_t, cv_t, layer):
    nb, n_pages = page_table.shape
    page = ck_t.shape[3]
    past = n_pages * page
    dk = N_KV_HEADS * HEAD_DIM
    kern = functools.partial(_attn_sample_kernel, layer=layer, n_pages=n_pages, page=page)
    grid_spec = pltpu.PrefetchScalarGridSpec(
        num_scalar_prefetch=1,
        grid=(nb,),
        in_specs=[pl.BlockSpec((1, N_HEADS, dk), lambda b, pt: (b, 0, 0)),
                  pl.BlockSpec((1, 1, past + LANES), lambda b, pt: (b, 0, 0)),
                  pl.BlockSpec((1, 1, dk), lambda b, pt: (b, 0, 0)),
                  pl.BlockSpec((1, 1, dk), lambda b, pt: (b, 0, 0)),
                  pl.BlockSpec(memory_space=pl.ANY),
                  pl.BlockSpec(memory_space=pl.ANY)],
        out_specs=pl.BlockSpec((1, N_HEADS, dk), lambda b, pt: (b, 0, 0)),
        scratch_shapes=[pltpu.VMEM((2, dk, past), F32), pltpu.VMEM((2, dk, past), F32),
                        pltpu.SemaphoreType.DMA((2,)), pltpu.SemaphoreType.DMA((2,))],
    )
    return pl.pallas_call(
        kern, grid_spec=grid_spec,
        out_shape=jax.ShapeDtypeStruct((nb, N_HEADS, dk), F32),
        compiler_params=_cp(("arbitrary",)),
        name="attn_sample",
    )(page_table, q8, bias, k_new, v_new, ck_t, cv_t)


def _ssd_prompt_kernel(xbc_ref, z_ref, sm_ref, cw_ref, cb_ref, dtb_ref, alog_ref, dskip_ref, ng_ref,
                       tril_ref, e64_ref, e128_ref, bmask_ref,
                       o_ref, cnew_ref, sst_ref, xbuf, st_ref, *, ts, kw, dxbc):
    halo = 8
    di = SSM_HEADS * SSM_HEAD_DIM
    dbc = SSM_GROUPS * D_STATE

    @pl.when(pl.program_id(1) == 0)
    def _():
        xbuf[0:halo, :] = jnp.zeros((halo, dxbc), F32)
        st_ref[...] = jnp.zeros(st_ref.shape, F32)

    xbuf[halo:halo + ts, :] = xbc_ref[...]
    first = halo - (kw - 1)
    conv = cb_ref[...] + cw_ref[0:1, :] * xbuf[first:first + ts, :]
    for j in range(1, kw):
        conv = conv + cw_ref[j:j + 1, :] * xbuf[first + j:first + j + ts, :]
    xc = _silu(conv)
    cnew_ref[0] = xbuf[ts + first:ts + halo, :]
    xbuf[0:halo, :] = xbuf[ts:ts + halo, :]

    lane = lax.broadcasted_iota(I32, (SSM_CHUNK, LANES), 1)
    head_lane = (lane[0:1, :] >= SM_DT) & (lane[0:1, :] < SM_DT + SSM_HEADS)
    a_row = jnp.where(head_lane, -jnp.exp(alog_ref[...]), 0.0)
    tri = lax.broadcasted_iota(I32, (SSM_CHUNK, SSM_CHUNK), 0) >= lax.broadcasted_iota(
        I32, (SSM_CHUNK, SSM_CHUNK), 1)
    glo = lane < D_STATE

    for k in range(ts // SSM_CHUNK):
        rows = slice(k * SSM_CHUNK, (k + 1) * SSM_CHUNK)
        dtf = _softplus(sm_ref[rows, :] + dtb_ref[...])
        adt = dtf * a_row
        a_cs = _sel_dot(tril_ref[...], adt)
        a_cs_t = a_cs.T
        acs_b = _dot_sel(a_cs, e128_ref[...])
        acs_f = _dot_sel(a_cs, e64_ref[...])
        dt_f = _dot_sel(dtf, e64_ref[...])
        alast_f = acs_f[SSM_CHUNK - 1:SSM_CHUNK, :]
        xs = xc[rows, 0:di]
        bm = xc[rows, di:di + dbc]
        cm = xc[rows, di + dbc:di + 2 * dbc]
        xdt = xs * dt_f
        xd = xdt * jnp.exp(alast_f - acs_f)
        bt = bm.T
        cb = (_dot(jnp.where(glo, cm, 0.0), bt), _dot(jnp.where(glo, 0.0, cm), bt))
        pairs = []
        for j in range(SSM_HEADS // 2):
            x2 = xdt[:, j * LANES:(j + 1) * LANES]
            acc = None
            for hh in range(2):
                h = 2 * j + hh
                seg = acs_b[:, h * LANES:(h + 1) * LANES] - a_cs_t[SM_DT + h:SM_DT + h + 1, :]
                lm = jnp.exp(jnp.where(tri, seg, -jnp.inf))
                sc = cb[h // (SSM_HEADS // SSM_GROUPS)] * lm
                xm = jnp.where(glo, x2, 0.0) if hh == 0 else jnp.where(glo, 0.0, x2)
                part = _dot(sc, xm)
                acc = part if acc is None else acc + part
            pairs.append(acc)
        y = jnp.concatenate(pairs, axis=1)
        y = y + _dot(cm, st_ref[...]) * jnp.exp(acs_f) + dskip_ref[...] * xs
        st_ref[...] = st_ref[...] * jnp.exp(alast_f) + bmask_ref[...] * _dot(bt, xd)
        yg = y * _silu(z_ref[rows, :])
        o_ref[rows, :] = _rms(yg) * ng_ref[...]
    sst_ref[0] = st_ref[...]


def _ssd_prompt(proj, cw, cb, dtb_row, alog_row, dskip_f, ng, consts, nseq, seq, ts):
    kw, dxbc = cw.shape
    di = SSM_HEADS * SSM_HEAD_DIM
    nt = seq // ts
    kern = functools.partial(_ssd_prompt_kernel, ts=ts, kw=kw, dxbc=dxbc)
    full = lambda a: pl.BlockSpec(a.shape, lambda s, i: (0,) * a.ndim)
    cs = (consts["tril"], consts["e64"], consts["e128"], consts["bmask"])
    return pl.pallas_call(
        kern,
        grid=(nseq, nt),
        in_specs=[pl.BlockSpec((ts, dxbc), lambda s, i: (s * nt + i, OFF["xbc"] // dxbc)),
                  pl.BlockSpec((ts, di), lambda s, i: (s * nt + i, OFF["z"] // di)),
                  pl.BlockSpec((ts, LANES), lambda s, i: (s * nt + i, OFF["small"] // LANES)),
                  full(cw), full(cb), full(dtb_row), full(alog_row), full(dskip_f), full(ng)]
                 + [full(c) for c in cs],
        out_specs=[pl.BlockSpec((ts, di), lambda s, i: (s * nt + i, 0)),
                   pl.BlockSpec((1, kw - 1, dxbc), lambda s, i: (s, 0, 0)),
                   pl.BlockSpec((1, SSM_GROUPS * D_STATE, di), lambda s, i: (s, 0, 0))],
        out_shape=[jax.ShapeDtypeStruct((nseq * seq, di), F32),
                   jax.ShapeDtypeStruct((nseq, kw - 1, dxbc), F32),
                   jax.ShapeDtypeStruct((nseq, SSM_GROUPS * D_STATE, di), F32)],
        scratch_shapes=[pltpu.VMEM((ts + 8, dxbc), F32),
                        pltpu.VMEM((SSM_GROUPS * D_STATE, di), F32)],
        compiler_params=_cp(("arbitrary", "arbitrary")),
        name="ssd_prompt",
    )(proj, proj, proj, cw, cb, dtb_row, alog_row, dskip_f, ng, *cs)


def _ssd_sample_kernel(xbc_ref, z_ref, sm_ref, st_ref, h0_ref, cw_ref, cb_ref, dtb_ref, alog_ref,
                       dskip_ref, ng_ref, e64_ref, o_ref, hn_ref, y_ref, *, nb, kw):
    di = SSM_HEADS * SSM_HEAD_DIM
    dbc = SSM_GROUPS * D_STATE
    conv = cb_ref[...] + cw_ref[kw - 1:kw, :] * xbc_ref[...]
    for j in range(kw - 1):
        conv = conv + cw_ref[j:j + 1, :] * st_ref[j]
    xc = _silu(conv)
    lane = lax.broadcasted_iota(I32, (1, LANES), 1)
    head_lane = (lane >= SM_DT) & (lane < SM_DT + SSM_HEADS)
    a_row = jnp.where(head_lane, -jnp.exp(alog_ref[...]), 0.0)
    dtf = _softplus(sm_ref[...] + dtb_ref[...])
    dec = jnp.exp(dtf * a_row)
    dt_f = _dot_sel(dtf, e64_ref[...])
    dec_f = _dot_sel(dec, e64_ref[...])
    xs = xc[:, 0:di]
    bm = xc[:, di:di + dbc]
    cm = xc[:, di + dbc:di + 2 * dbc]
    pad = jnp.zeros((LANES - nb, di), F32)
    xdt_t = jnp.concatenate([xs * dt_f, pad], axis=0).T
    dec_t = jnp.concatenate([dec_f, pad], axis=0).T
    bm_r = pltpu.roll(bm, D_STATE, 1)
    cm_r = pltpu.roll(cm, D_STATE, 1)
    rowi = lax.broadcasted_iota(I32, (di, D_STATE), 0)
    g0 = rowi < (SSM_HEADS // SSM_GROUPS) * SSM_HEAD_DIM
    lane_y = lax.broadcasted_iota(I32, (1, di), 1) < (SSM_HEADS // SSM_GROUPS) * SSM_HEAD_DIM
    row8 = lax.broadcasted_iota(I32, (8, D_STATE), 0)
    for b in range(nb):
        bsel = jnp.where(g0, bm[b:b + 1, 0:D_STATE], bm_r[b:b + 1, 0:D_STATE])
        hn = h0_ref[b] * dec_t[:, b:b + 1] + xdt_t[:, b:b + 1] * bsel
        hn_ref[b] = hn
        c2 = jnp.where(row8 == 0, cm[b:b + 1, 0:D_STATE],
                       jnp.where(row8 == 1, cm_r[b:b + 1, 0:D_STATE], 0.0))
        yr = _nt_dot(c2, hn)
        y_ref[b:b + 1, :] = jnp.where(lane_y, yr[0:1, :], yr[1:2, :])
    y = y_ref[...] + dskip_ref[...] * xs
    yg = y * _silu(z_ref[...])
    o_ref[...] = _rms(yg) * ng_ref[...]


def _ssd_sample(proj, st_t, h0, cw, cb, dtb_row, alog_row, dskip_f, ng, consts):
    kw, dxbc = cw.shape
    nb = proj.shape[0]
    di = SSM_HEADS * SSM_HEAD_DIM
    kern = functools.partial(_ssd_sample_kernel, nb=nb, kw=kw)
    full = lambda a: pl.BlockSpec(a.shape, lambda i: (0,) * a.ndim)
    return pl.pallas_call(
        kern,
        grid=(1,),
        in_specs=[pl.BlockSpec((nb, dxbc), lambda i: (0, OFF["xbc"] // dxbc)),
                  pl.BlockSpec((nb, di), lambda i: (0, OFF["z"] // di)),
                  pl.BlockSpec((nb, LANES), lambda i: (0, OFF["small"] // LANES)),
                  full(st_t), full(h0), full(cw), full(cb), full(dtb_row), full(alog_row),
                  full(dskip_f), full(ng), full(consts["e64"])],
        out_specs=[pl.BlockSpec((nb, di), lambda i: (0, 0)),
                   pl.BlockSpec(h0.shape, lambda i: (0, 0, 0))],
        out_shape=[jax.ShapeDtypeStruct((nb, di), F32), jax.ShapeDtypeStruct(h0.shape, F32)],
        scratch_shapes=[pltpu.VMEM((nb, di), F32)],
        compiler_params=_cp(("arbitrary",)),
        name="ssd_sample",
    )(proj, proj, proj, st_t, h0, cw, cb, dtb_row, alog_row, dskip_f, ng, consts["e64"])


def _mem_kv_kernel(x_ref, g_ref, w_ref, kg_ref, mk_ref, mv_ref):
    dm = mk_ref.shape[1]
    m = _dot(_rms(x_ref[...]) * g_ref[...], w_ref[...])
    for h in range(MEM_HEADS):
        sl = slice(h * MEM_HEAD_DIM, (h + 1) * MEM_HEAD_DIM)
        mk_ref[:, sl] = _rms(m[:, sl]) * kg_ref[...]
    mv_ref[...] = m[:, dm:]


def _mem_kv(mem2d, g, w, kg, rows):
    t, d = mem2d.shape
    dm = MEM_HEADS * MEM_HEAD_DIM
    return pl.pallas_call(
        _mem_kv_kernel,
        grid=(t // rows,),
        in_specs=[pl.BlockSpec((rows, d), lambda i: (i, 0)),
                  pl.BlockSpec((1, d), lambda i: (0, 0)),
                  pl.BlockSpec((d, 2 * dm), lambda i: (0, 0)),
                  pl.BlockSpec((1, MEM_HEAD_DIM), lambda i: (0, 0))],
        out_specs=[pl.BlockSpec((rows, dm), lambda i: (i, 0)),
                   pl.BlockSpec((rows, dm), lambda i: (i, 0))],
        out_shape=[jax.ShapeDtypeStruct((t, dm), F32), jax.ShapeDtypeStruct((t, dm), F32)],
        compiler_params=_cp(("parallel",)),
        name="mem_kv",
    )(mem2d, g, w, kg)


def _mem_attend_kernel(mq_ref, mk_ref, mv_ref, g_ref, o_ref, *, tm):
    mq = mq_ref[0]
    rows = max(tm, 8)
    if tm < rows:
        mq = jnp.broadcast_to(mq, (rows, mq.shape[1]))
    for h in range(MEM_HEADS):
        sl = slice(h * MEM_HEAD_DIM, (h + 1) * MEM_HEAD_DIM)
        qn = _rms(mq[:, sl]) * g_ref[...]
        s = _nt_dot(qn, mk_ref[0, :, sl]) * (MEM_HEAD_DIM ** -0.5)
        p = jnp.exp(s - jnp.max(s, axis=1, keepdims=True))
        o = _dot(p, mv_ref[0, :, sl]) / jnp.sum(p, axis=1, keepdims=True)
        o_ref[0, :, sl] = o[0:tm, :]


def _mem_attend(proj3, mk3, mv3, g, tm):
    nseq, seq, _ = proj3.shape
    nm = mk3.shape[1]
    dm = MEM_HEADS * MEM_HEAD_DIM
    kern = functools.partial(_mem_attend_kernel, tm=tm)
    return pl.pallas_call(
        kern,
        grid=(nseq, seq // tm),
        in_specs=[pl.BlockSpec((1, tm, dm), lambda s, i: (s, i, OFF["mq"] // dm)),
                  pl.BlockSpec((1, nm, dm), lambda s, i: (s, 0, 0)),
                  pl.BlockSpec((1, nm, dm), lambda s, i: (s, 0, 0)),
                  pl.BlockSpec((1, MEM_HEAD_DIM), lambda s, i: (0, 0))],
        out_specs=pl.BlockSpec((1, tm, dm), lambda s, i: (s, i, 0)),
        out_shape=jax.ShapeDtypeStruct((nseq, seq, dm), F32),
        compiler_params=_cp(("parallel", "arbitrary")),
        name="mem_attend",
    )(proj3, mk3, mv3, g)


def _merge_kernel(x_ref, ba_ref, bb_ref, bc_ref, bm_ref, gt_ref, wb_ref, wo_ref, o_ref):
    d = x_ref.shape[1]
    acc = None
    for n, br in enumerate((ba_ref, bb_ref, bc_ref, bm_ref)):
        term = jax.nn.sigmoid(gt_ref[:, n * d:(n + 1) * d]) * _dot(br[...].astype(MXU_DT), wb_ref[n])
        acc = term if acc is None else acc + term
    o_ref[...] = x_ref[...] + _dot(acc.astype(MXU_DT), wo_ref[...])


def _merge(x, proj, brs, wb, wo, tm):
    t, d = x.shape
    nbr, bw, _ = wb.shape
    return pl.pallas_call(
        _merge_kernel,
        grid=(t // tm,),
        in_specs=[pl.BlockSpec((tm, d), lambda i: (i, 0))]
                 + [pl.BlockSpec((tm, bw), lambda i: (i, 0))] * nbr
                 + [pl.BlockSpec((tm, nbr * d), lambda i: (i, OFF["gates"] // (nbr * d))),
                    pl.BlockSpec((nbr, bw, d), lambda i: (0, 0, 0)),
                    pl.BlockSpec((d, d), lambda i: (0, 0))],
        out_specs=pl.BlockSpec((tm, d), lambda i: (i, 0)),
        out_shape=jax.ShapeDtypeStruct((t, d), F32),
        compiler_params=_cp(("parallel",)),
        name="merge",
    )(x, *brs, proj, wb, wo)


def _ffn_prompt_kernel(x_ref, g_ref, wg_ref, wu_ref, cwg_ref, cwu_ref, cbg_ref, cbu_ref, wd_ref,
                       o_ref, unew_ref, h_ref, acc_ref, ubuf, carry, *, tm, tc, kw, nff):
    i = pl.program_id(1)
    c = pl.program_id(2)
    halo = 8
    first = halo - (kw - 1)

    @pl.when(c == 0)
    def _():
        h_ref[...] = (_rms(x_ref[...]) * g_ref[...]).astype(h_ref.dtype)
        acc_ref[...] = jnp.zeros(acc_ref.shape, F32)

    @pl.when(i == 0)
    def _():
        carry[c] = jnp.zeros((halo, 2 * tc), F32)

    h = h_ref[...]
    ubuf[0:halo, :] = carry[c]
    ubuf[halo:halo + tm, 0:tc] = _dot(h, wg_ref[...])
    ubuf[halo:halo + tm, tc:2 * tc] = _dot(h, wu_ref[...])
    carry[c] = ubuf[tm:tm + halo, :]
    fg =cbg_ref[...] + cwg_ref[0:1, :] * ubuf[first:first + tm, 0:tc]
    fu = cbu_ref[...] + cwu_ref[0:1, :] * ubuf[first:first + tm, tc:2 * tc]
    for j in range(1, kw):
        fg = fg + cwg_ref[j:j + 1, :] * ubuf[first + j:first + j + tm, 0:tc]
        fu = fu + cwu_ref[j:j + 1, :] * ubuf[first + j:first + j + tm, tc:2 * tc]
    acc_ref[...] += _dot((_silu(fg) * fu).astype(MXU_DT), wd_ref[...])

    @pl.when(c == nff - 1)
    def _():
        o_ref[...] = x_ref[...] + acc_ref[...]

    @pl.when((c == nff - 1) & (i == pl.num_programs(1) - 1))
    def _():
        for cc in range(nff):
            unew_ref[0, :, cc * tc:(cc + 1) * tc] = carry[cc, first:halo, 0:tc]
            unew_ref[0, :, (nff + cc) * tc:(nff + cc + 1) * tc] = carry[cc, first:halo, tc:2 * tc]


def _ffn_prompt(x, g, wup, cw, cb, wd, nseq, seq, tm, tc):
    t, d = x.shape
    dff = wd.shape[0]
    kw = cw.shape[0]
    nt = seq // tm
    nff = dff // tc
    kern = functools.partial(_ffn_prompt_kernel, tm=tm, tc=tc, kw=kw, nff=nff)
    return pl.pallas_call(
        kern,
        grid=(nseq, nt, nff),
        in_specs=[pl.BlockSpec((tm, d), lambda s, i, c: (s * nt + i, 0)),
                  pl.BlockSpec((1, d), lambda s, i, c: (0, 0)),
                  pl.BlockSpec((d, tc), lambda s, i, c: (0, c)),
                  pl.BlockSpec((d, tc), lambda s, i, c: (0, nff + c)),
                  pl.BlockSpec((kw, tc), lambda s, i, c: (0, c)),
                  pl.BlockSpec((kw, tc), lambda s, i, c: (0, nff + c)),
                  pl.BlockSpec((1, tc), lambda s, i, c: (0, c)),
                  pl.BlockSpec((1, tc), lambda s, i, c: (0, nff + c)),
                  pl.BlockSpec((tc, d), lambda s, i, c: (c, 0))],
        out_specs=[pl.BlockSpec((tm, d), lambda s, i, c: (s * nt + i, 0)),
                   pl.BlockSpec((1, kw - 1, 2 * dff), lambda s, i, c: (s, 0, 0))],
        out_shape=[jax.ShapeDtypeStruct((t, d), F32),
                   jax.ShapeDtypeStruct((nseq, kw - 1, 2 * dff), F32)],
        scratch_shapes=[pltpu.VMEM((tm, d), MXU_DT), pltpu.VMEM((tm, d), F32),
                        pltpu.VMEM((tm + 8, 2 * tc), F32), pltpu.VMEM((nff, 8, 2 * tc), F32)],
        compiler_params=_cp(("arbitrary", "arbitrary", "arbitrary")),
        name="ffn_prompt",
    )(x, g, wup, wup, cw, cw, cb, cb, wd)


def _ffn_sample_kernel(x_ref, g_ref, wg_ref, wu_ref, stg_ref, stu_ref, cwg_ref, cwu_ref, cbg_ref,
                       cbu_ref, wd_ref, o_ref, ug_ref, uu_ref, h_ref, acc_ref, *, kw):
    c = pl.program_id(0)

    @pl.when(c == 0)
    def _():
        h_ref[...] = (_rms(x_ref[...]) * g_ref[...]).astype(h_ref.dtype)
        acc_ref[...] = jnp.zeros(acc_ref.shape, F32)

    h = h_ref[...]
    ug = _dot(h, wg_ref[...])
    uu = _dot(h, wu_ref[...])
    ug_ref[...] = ug
    uu_ref[...] = uu
    fg = cbg_ref[...] + cwg_ref[kw - 1:kw, :] * ug
    fu = cbu_ref[...] + cwu_ref[kw - 1:kw, :] * uu
    for j in range(kw - 1):
        fg = fg + cwg_ref[j:j + 1, :] * stg_ref[j]
        fu = fu + cwu_ref[j:j + 1, :] * stu_ref[j]
    acc_ref[...] += _dot((_silu(fg) * fu).astype(MXU_DT), wd_ref[...])

    @pl.when(c == pl.num_programs(0) - 1)
    def _():
        o_ref[...] = x_ref[...] + acc_ref[...]


def _ffn_sample(x, g, wup, st_t, cw, cb, wd, tc):
    t, d = x.shape
    dff = wd.shape[0]
    kw = cw.shape[0]
    nff = dff // tc
    kern = functools.partial(_ffn_sample_kernel, kw=kw)
    return pl.pallas_call(
        kern,
        grid=(nff,),
        in_specs=[pl.BlockSpec((t, d), lambda c: (0, 0)),
                  pl.BlockSpec((1, d), lambda c: (0, 0)),
                  pl.BlockSpec((d, tc), lambda c: (0, c)),
                  pl.BlockSpec((d, tc), lambda c: (0, nff + c)),
                  pl.BlockSpec((kw - 1, t, tc), lambda c: (0, 0, c)),
                  pl.BlockSpec((kw - 1, t, tc), lambda c: (0, 0, nff + c)),
                  pl.BlockSpec((kw, tc), lambda c: (0, c)),
                  pl.BlockSpec((kw, tc), lambda c: (0, nff + c)),
                  pl.BlockSpec((1, tc), lambda c: (0, c)),
                  pl.BlockSpec((1, tc), lambda c: (0, nff + c)),
                  pl.BlockSpec((tc, d), lambda c: (c, 0))],
        out_specs=[pl.BlockSpec((t, d), lambda c: (0, 0)),
                   pl.BlockSpec((t, tc), lambda c: (0, c)),
                   pl.BlockSpec((t, tc), lambda c: (0, c))],
        out_shape=[jax.ShapeDtypeStruct((t, d), F32), jax.ShapeDtypeStruct((t, dff), F32),
                   jax.ShapeDtypeStruct((t, dff), F32)],
        scratch_shapes=[pltpu.VMEM((t, d), MXU_DT), pltpu.VMEM((t, d), F32)],
        compiler_params=_cp(("arbitrary",)),
        name="ffn_sample",
    )(x, g, wup, wup, st_t, st_t, cw, cw, cb, cb, wd)


def _constants():
    lane = np.arange(LANES)
    r = lane % HEAD_DIM
    rot = HEAD_DIM // 4
    half = rot // 2
    inv_freq = ROPE_THETA ** (-jnp.arange(half, dtype=F32) * (2.0 / rot))
    in_rot = r < rot
    invf = jnp.where(jnp.asarray(in_rot), inv_freq[jnp.asarray(r % half)], 0.0).astype(F32)[None, :]
    sgn = np.where(r < half, -1.0, np.where(in_rot, 1.0, 0.0)).astype(np.float32)[None, :]
    first_head = (lane < HEAD_DIM)[None, :]
    dq = N_HEADS * HEAD_DIM
    bd = (np.arange(dq)[:, None] // HEAD_DIM == np.arange(dq)[None, :] // HEAD_DIM)
    di = SSM_HEADS * SSM_HEAD_DIM
    e64 = np.zeros((LANES, di), np.float32)
    e128 = np.zeros((LANES, SSM_HEADS * LANES), np.float32)
    for h in range(SSM_HEADS):
        e64[SM_DT + h, h * SSM_HEAD_DIM:(h + 1) * SSM_HEAD_DIM] = 1.0
        e128[SM_DT + h, h * LANES:(h + 1) * LANES] = 1.0
    tril = np.tril(np.ones((SSM_CHUNK, SSM_CHUNK), np.float32))
    hpg = SSM_HEADS // SSM_GROUPS
    bmask = (np.arange(SSM_GROUPS * D_STATE)[:, None] // D_STATE
             == np.arange(di)[None, :] // (hpg * SSM_HEAD_DIM)).astype(np.float32)
    return dict(
        invf=invf, sgn=jnp.asarray(sgn),
        invf_s=jnp.where(jnp.asarray(first_head), invf, 0.0),
        sgn_s=jnp.asarray(np.where(first_head, sgn, 0.0).astype(np.float32)),
        bd=jnp.asarray(bd.astype(np.float32), dtype=BF16),
        e64=jnp.asarray(e64, dtype=BF16), e128=jnp.asarray(e128, dtype=BF16),
        tril=jnp.asarray(tril, dtype=BF16), bmask=jnp.asarray(bmask))


def _reorder_w_in(w_in):
    d = w_in.shape[0]
    sizes = dict(glu=1024, q=512, k=128, v=128, qi=512, ki=64, wi=8, z=512, xbc=768, dt=8, mq=512,
                 gates=4096)
    order_in = ["glu", "q", "k", "v", "qi", "ki", "wi", "z", "xbc", "dt", "mq", "gates"]
    parts, off = {}, 0
    for name in order_in:
        parts[name] = w_in[:, off:off + sizes[name]]
        off += sizes[name]
    assert off == w_in.shape[1]
    pad = jnp.zeros((d, LANES - sizes["ki"] - sizes["wi"] - sizes["dt"]), w_in.dtype)
    out = jnp.concatenate([parts[n] for n in ("glu", "q", "qi", "z", "mq", "xbc", "k", "v", "gates",
                                              "ki", "wi", "dt")] + [pad], axis=1)
    assert out.shape[1] == PROJ_COLS
    return out


def _pad_lanes(v, start):
    return jnp.zeros((1, LANES), F32).at[0, start:start + v.shape[0]].set(v)


def _tile(n, cap):
    return min(n, cap)


def _layer_params(l, prm, consts):
    p = dict(
        w_in=_reorder_w_in(prm["w_in"][l]).astype(MXU_DT),
        norm_mix_g=prm["norm_mix_g"][l][None, :],
        conv_a_w=prm["conv_a_w"][l], conv_a_b=prm["conv_a_b"][l][None, :],
        ln_a_g=prm["ln_a_g"][l][None, :], ln_a_b=prm["ln_a_b"][l][None, :],
        qg=jnp.tile(prm["q_norm_g"][l], N_HEADS)[None, :],
        kg=jnp.tile(prm["k_norm_g"][l], N_KV_HEADS)[None, :],
        ssm_conv_w=prm["ssm_conv_w"][l], ssm_conv_b=prm["ssm_conv_b"][l][None, :],
        dtb_row=_pad_lanes(prm["dt_bias"][l], SM_DT), alog_row=_pad_lanes(prm["a_log"][l], SM_DT),
        dskip_f=jnp.repeat(prm["d_skip"][l], SSM_HEAD_DIM)[None, :],
        ssm_norm_g=prm["ssm_norm_g"][l][None, :],
        mem_norm_g=prm["mem_norm_g"][l][None, :], w_mem_kv=prm["w_mem_kv"][l],
        mq_norm_g=prm["mq_norm_g"][l][None, :], mk_norm_g=prm["mk_norm_g"][l][None, :],
        w_branch=prm["w_branch"][l].astype(MXU_DT), w_out=prm["w_out"][l].astype(MXU_DT),
        norm_ffn_g=prm["norm_ffn_g"][l][None, :], w_ffn_up=prm["w_ffn_up"][l].astype(MXU_DT),
        ffn_conv_w=prm["ffn_conv_w"][l], ffn_conv_b=prm["ffn_conv_b"][l][None, :],
        w_ffn_down=prm["w_ffn_down"][l].astype(MXU_DT))
    return p


def _ssm_state_from_slab(slab):
    nseq = slab.shape[0]
    hpg = SSM_HEADS // SSM_GROUPS
    s = slab.reshape(nseq, SSM_GROUPS, D_STATE, SSM_HEADS, SSM_HEAD_DIM)
    per_head = [s[:, h // hpg, :, h, :] for h in range(SSM_HEADS)]
    return jnp.swapaxes(jnp.stack(per_head, axis=1), 2, 3)


def _prompt_layer(x, p, consts, mem2d, nseq, seq):
    t, d = x.shape
    nm = mem2d.shape[0] // nseq
    proj = _norm_proj(x, p["norm_mix_g"], p["w_in"], _tile(t, 1024), PROJ_COLS // 5)
    br_a, conf_new = _conf_prompt(proj, p["conv_a_w"], p["conv_a_b"], p["ln_a_g"], p["ln_a_b"],
                                  nseq, seq, _tile(seq, 512))
    ck = _tile(seq, 512)
    kn, v, kir, knb, smb, qt, qit, vt, smt = _dsa_prep_t(proj, p["qg"], p["kg"], consts, nseq, seq, ck)
    br_b = _dsa_prompt_t(qt, qit, smt, smb, knb, vt, nseq, seq, _tile(seq, 512), ck)
    br_c, sconv_new, sslab = _ssd_prompt(proj, p["ssm_conv_w"], p["ssm_conv_b"], p["dtb_row"],
                                         p["alog_row"], p["dskip_f"], p["ssm_norm_g"], consts,
                                         nseq, seq, _tile(seq, 512))
    mk, mv = _mem_kv(mem2d, p["mem_norm_g"], p["w_mem_kv"], p["mk_norm_g"], nm)
    dm = MEM_HEADS * MEM_HEAD_DIM
    br_m = _mem_attend(proj.reshape(nseq, seq, PROJ_COLS), mk.reshape(nseq, nm, dm),
                       mv.reshape(nseq, nm, dm), p["mq_norm_g"], _tile(seq, 512)).reshape(t, dm)
    x = _merge(x, proj, (br_a, br_b, br_c, br_m), p["w_branch"], p["w_out"], _tile(t, 256))
    x, ffn_new = _ffn_prompt(x, p["norm_ffn_g"], p["w_ffn_up"], p["ffn_conv_w"], p["ffn_conv_b"],
                             p["w_ffn_down"], nseq, seq, _tile(seq, 512), p["w_ffn_down"].shape[0] // 2)
    to_tok = lambda a: jnp.transpose(a.reshape(nseq, N_KV_HEADS, HEAD_DIM, seq), (0, 3, 1, 2))
    state = (to_tok(kn), to_tok(v), jnp.transpose(kir, (0, 2, 1)),
             mk.reshape(nseq, nm, MEM_HEADS, MEM_HEAD_DIM), mv.reshape(nseq, nm, MEM_HEADS, MEM_HEAD_DIM),
             conf_new, sconv_new, _ssm_state_from_slab(sslab), ffn_new)
    return x, state


def _sample_layer(x, p, consts, layer, page_table, ck_t, cv_t, cki_t, cmk, cmv, st_conf, st_sconv, st_ssm,
                  st_ffn):
    nb, d = x.shape
    n_pages = page_table.shape[1]
    page = ck_t.shape[3]
    past = n_pages * page
    dq = N_HEADS * HEAD_DIM
    dk = N_KV_HEADS * HEAD_DIM
    proj = _norm_proj(x, p["norm_mix_g"], p["w_in"], nb, PROJ_COLS // 5)
    br_a, a_new = _conf_sample(proj, jnp.swapaxes(st_conf, 0, 1), p["conv_a_w"], p["conv_a_b"],
                               p["ln_a_g"], p["ln_a_b"])
    qn, kn, v, qir, smr, kir = _dsa_prep(proj, p["qg"], p["kg"], consts, nb, 1, float(past))
    hpk = N_HEADS // N_KV_HEADS
    qh = qn.reshape(nb, N_HEADS, HEAD_DIM)
    grp = (np.arange(N_HEADS)[:, None] // hpk == np.arange(dk)[None, :] // HEAD_DIM)
    q8 = jnp.where(jnp.asarray(grp)[None], jnp.tile(qh, (1, 1, N_KV_HEADS)), 0.0)
    sc = _idx_sample(page_table, qir.reshape(nb, N_IDX_HEADS, D_IDX),
                     smr[:, SM_WI:SM_WI + N_IDX_HEADS].reshape(nb, N_IDX_HEADS, 1),
                     kir.reshape(nb, 1, D_IDX), cki_t, layer)
    bias = _topk_bias(sc.reshape(nb, past + LANES), min(TOP_K, (past + 1) // 4))
    o8 = _attn_sample(page_table, q8, bias.reshape(nb, 1, past + LANES), kn.reshape(nb, 1, dk),
                      v.reshape(nb, 1, dk), ck_t, cv_t, layer)
    o8 = o8.reshape(nb, N_HEADS, N_KV_HEADS, HEAD_DIM)
    br_b = jnp.stack([o8[:, h, h // hpk, :] for h in range(N_HEADS)], axis=1).reshape(nb, dq)
    h0 = st_ssm.reshape(nb, SSM_HEADS * SSM_HEAD_DIM, D_STATE)
    br_c, h_new = _ssd_sample(proj, jnp.swapaxes(st_sconv, 0, 1), h0, p["ssm_conv_w"], p["ssm_conv_b"],
                              p["dtb_row"], p["alog_row"], p["dskip_f"], p["ssm_norm_g"], consts)
    nm = cmk.shape[1]
    dm = MEM_HEADS * MEM_HEAD_DIM
    br_m = _mem_attend(proj.reshape(nb, 1, PROJ_COLS), cmk.reshape(nb, nm, dm), cmv.reshape(nb, nm, dm),
                       p["mq_norm_g"], 1).reshape(nb, dm)
    x = _merge(x, proj, (br_a, br_b, br_c, br_m), p["w_branch"], p["w_out"], nb)
    x, ug, uu = _ffn_sample(x, p["norm_ffn_g"], p["w_ffn_up"], jnp.swapaxes(st_ffn, 0, 1),
                            p["ffn_conv_w"], p["ffn_conv_b"], p["w_ffn_down"], 256)
    xbc_raw = proj[:, OFF["xbc"]:OFF["xbc"] + st_sconv.shape[-1]]
    state = (kn.reshape(nb, 1, N_KV_HEADS, HEAD_DIM), v.reshape(nb, 1, N_KV_HEADS, HEAD_DIM),
             kir.reshape(nb, 1, D_IDX),
             jnp.concatenate([st_conf[:, 1:], a_new[:, None]], axis=1),
             jnp.concatenate([st_sconv[:, 1:], xbc_raw[:, None]], axis=1),
             h_new.reshape(st_ssm.shape),
             jnp.concatenate([st_ffn[:, 1:], jnp.concatenate([ug, uu], axis=-1)[:, None]], axis=1))
    return x, state


def kernel(x_prompt, x_sample, cache_k, cache_v, cache_kidx, cache_mem_k, cache_mem_v, state_conformer, state_ssm_conv, state_ssm, state_ffn_conv, page_table, mem_prompt, norm_mix_g, w_in, conv_a_w, conv_a_b, ln_a_g, ln_a_b, q_norm_g, k_norm_g, ssm_conv_w, ssm_conv_b, dt_bias, a_log, d_skip, ssm_norm_g, mem_norm_g, w_mem_kv, mq_norm_g, mk_norm_g, w_branch, w_out, norm_ffn_g, w_ffn_up, ffn_conv_w, ffn_conv_b, w_ffn_down):
    prm = dict(norm_mix_g=norm_mix_g, w_in=w_in, conv_a_w=conv_a_w, conv_a_b=conv_a_b, ln_a_g=ln_a_g,
               ln_a_b=ln_a_b, q_norm_g=q_norm_g, k_norm_g=k_norm_g, ssm_conv_w=ssm_conv_w,
               ssm_conv_b=ssm_conv_b, dt_bias=dt_bias, a_log=a_log, d_skip=d_skip, ssm_norm_g=ssm_norm_g,
               mem_norm_g=mem_norm_g, w_mem_kv=w_mem_kv, mq_norm_g=mq_norm_g, mk_norm_g=mk_norm_g,
               w_branch=w_branch, w_out=w_out, norm_ffn_g=norm_ffn_g, w_ffn_up=w_ffn_up,
               ffn_conv_w=ffn_conv_w, ffn_conv_b=ffn_conv_b, w_ffn_down=w_ffn_down)
    depth = w_in.shape[0]
    nseq, seq, d = x_prompt.shape
    nb, dseq, _ = x_sample.shape
    assert dseq == 1
    consts = _constants()
    xp = x_prompt.reshape(nseq * seq, d)
    xs = x_sample.reshape(nb, d)
    mem2d = mem_prompt.reshape(nseq * mem_prompt.shape[1], d)
    n_phys, page = cache_k.shape[1], cache_k.shape[2]
    ck_t = jnp.transpose(cache_k, (0, 1, 3, 4, 2)).reshape(depth, n_phys, N_KV_HEADS * HEAD_DIM, page)
    cv_t = jnp.transpose(cache_v, (0, 1, 3, 4, 2)).reshape(depth, n_phys, N_KV_HEADS * HEAD_DIM, page)
    cki_t = jnp.transpose(cache_kidx, (0, 1, 3, 2))
    p_states, s_states = [], []
    for l in range(depth):
        p = _layer_params(l, prm, consts)
        xp, st = _prompt_layer(xp, p, consts, mem2d, nseq, seq)
        p_states.append(st)
        xs, st = _sample_layer(xs, p, consts, l, page_table, ck_t, cv_t, cki_t,
                               cache_mem_k[l], cache_mem_v[l], state_conformer[l], state_ssm_conv[l],
                               state_ssm[l], state_ffn_conv[l])
        s_states.append(st)
    stack = lambda states, k: jnp.stack([s[k] for s in states])
    return ((xp.reshape(nseq, seq, d), xs.reshape(nb, 1, d))
            + tuple(stack(p_states, k) for k in range(9))
            + tuple(stack(s_states, k) for k in range(7)))
```
